```python
import jax
import jax.numpy as jnp
from jax import lax

D_MODEL = 1024
BATCH = 2
SEQ = 16384
DEPTH = 2

GRID_W = 64
CTX_LEN = 256
Q_BLOCK = 128
WINDOW = 128
ROPE_BASE = 10000.0
EPS = 1e-6
NEG_INF = -1e30

MLA_HEADS = 8
MLA_NOPE = 64
MLA_ROPE = 32
MLA_V = 64
MLA_Q_RANK = 256
MLA_KV_RANK = 256

GQA_HEADS = 8
GQA_KV_HEADS = 2
GQA_DIM = 64

WIN_HEADS = 8
WIN_KV_HEADS = 2
WIN_DIM = 64

CONV_CH = 512
CONV_K = 3

AB_MIX = MLA_HEADS * MLA_V + GQA_HEADS * GQA_DIM
AB_SPLIT = (MLA_Q_RANK, MLA_KV_RANK, MLA_ROPE,
            GQA_HEADS * GQA_DIM, GQA_KV_HEADS * GQA_DIM, GQA_KV_HEADS * GQA_DIM,
            AB_MIX)
AB_IN = sum(AB_SPLIT)
CD_MIX = WIN_HEADS * WIN_DIM + CONV_CH
CD_SPLIT = (WIN_HEADS * WIN_DIM, WIN_KV_HEADS * WIN_DIM, WIN_KV_HEADS * WIN_DIM,
            CONV_CH, CONV_CH, CONV_CH, CD_MIX)
CD_IN = sum(CD_SPLIT)

kernel_name = 'hybrid_mla_gqa_window_conv_dit'


def rmsnorm(x, gain=None):
    xf = x.astype(jnp.float32)
    y = xf * lax.rsqrt(jnp.mean(xf * xf, axis=-1, keepdims=True) + EPS)
    if gain is not None:
        y = y * gain.astype(jnp.float32)
    return y.astype(x.dtype)


def adaln(x, mod):
    shift, scale, gate = jnp.split(mod, 3, axis=-1)
    return rmsnorm(x) * (1.0 + scale) + shift, gate


def split_cols(t, sizes):
    out, off = [], 0
    for n in sizes:
        out.append(t[..., off:off + n])
        off += n
    return out


def grid_positions(n):
    rows = n // GRID_W
    row = jnp.repeat(jnp.arange(rows, dtype=jnp.float32), GRID_W)
    col = jnp.tile(jnp.arange(GRID_W, dtype=jnp.float32), rows)
    return row, col


def rope_1d(x, pos):
    half = x.shape[-1] // 2
    inv = ROPE_BASE ** (-jnp.arange(half, dtype=jnp.float32) / half)
    ang = pos[:, None] * inv[None, :]
    cos = jnp.cos(ang)[:, None, :].astype(x.dtype)
    sin = jnp.sin(ang)[:, None, :].astype(x.dtype)
    x1, x2 = x[..., :half], x[..., half:]
    return jnp.concatenate([x1 * cos - x2 * sin, x2 * cos + x1 * sin], axis=-1)


def axial_rope(x, row, col):
    half = x.shape[-1] // 2
    return jnp.concatenate([rope_1d(x[..., :half], row), rope_1d(x[..., half:], col)], axis=-1)


def rope_tail(x, rot_dim, row, col):
    return jnp.concatenate([x[..., :-rot_dim], axial_rope(x[..., -rot_dim:], row, col)], axis=-1)


def ctx_attention(q, k, v, sink=None):
    b, l, h, d = q.shape
    hk = k.shape[2]
    g = h // hk
    qg = q.reshape(b, l, hk, g, d)
    s = jnp.einsum('bqkgd,bnkd->bkgqn', qg, k, preferred_element_type=jnp.float32) * (d ** -0.5)
    if sink is not None:
        sk = jnp.broadcast_to(sink.reshape(hk, g).astype(jnp.float32)[None, :, :, None, None], (b, hk, g, l, 1))
        p = jax.nn.softmax(jnp.concatenate([s, sk], axis=-1), axis=-1)[..., :l]
    else:
        p = jax.nn.softmax(s, axis=-1)
    o = jnp.einsum('bkgqn,bnkd->bqkgd', p.astype(v.dtype), v)
    return o.reshape(b, l, h, v.shape[-1])


def joint_dense_attention(q, k, v, kc, vc):
    b, s, h, d = q.shape
    hk = k.shape[2]
    g = h // hk
    nb = s // Q_BLOCK
    kk = jnp.concatenate([kc, k], axis=1)
    vv = jnp.concatenate([vc, v], axis=1)
    qb = jnp.moveaxis(q.reshape(b, nb, Q_BLOCK, hk, g, d), 1, 0)

    def block(qi):
        sc = jnp.einsum('bqkgd,bnkd->bkgqn', qi, kk, preferred_element_type=jnp.float32) * (d ** -0.5)
        p = jax.nn.softmax(sc, axis=-1)
        return jnp.einsum('bkgqn,bnkd->bqkgd', p.astype(vv.dtype), vv)

    o = lax.map(block, qb)
    return jnp.moveaxis(o, 0, 1).reshape(b, s, h, v.shape[-1])


def joint_window_attention(q, k, v, kc, vc, sink):
    b, s, h, d = q.shape
    hk = k.shape[2]
    g = h // hk
    nb = s // Q_BLOCK
    l = kc.shape[1]
    pad = ((0, 0), (Q_BLOCK, Q_BLOCK), (0, 0), (0, 0))
    kp, vp = jnp.pad(k, pad), jnp.pad(v, pad)

    def bands(t):
        return jnp.concatenate(
            [t[:, j * Q_BLOCK: j * Q_BLOCK + s].reshape(b, nb, Q_BLOCK, hk, t.shape[-1]) for j in range(3)], axis=2)

    kb = jnp.moveaxis(bands(kp), 1, 0)
    vb = jnp.moveaxis(bands(vp), 1, 0)
    qb = jnp.moveaxis(q.reshape(b, nb, Q_BLOCK, hk, g, d), 1, 0)
    blk = jnp.arange(nb)[:, None, None] * Q_BLOCK
    qpos = blk + jnp.arange(Q_BLOCK)[None, :, None]
    kpos = blk - Q_BLOCK + jnp.arange(3 * Q_BLOCK)[None, None, :]
    mask = (jnp.abs(qpos - kpos) <= WINDOW) & (kpos >= 0) & (kpos < s)
    sink_l = sink.reshape(hk, g).astype(jnp.float32)
    scale = d ** -0.5

    def block(args):
        qi, ki, vi, mi = args
        s_loc = jnp.einsum('bqkgd,bmkd->bkgqm', qi, ki, preferred_element_type=jnp.float32) * scale
        s_loc = jnp.where(mi[None, None, None], s_loc, NEG_INF)
        s_ctx = jnp.einsum('bqkgd,blkd->bkgql', qi, kc, preferred_element_type=jnp.float32) * scale
        s_snk = jnp.broadcast_to(sink_l[None, :, :, None, None], (b, hk, g, Q_BLOCK, 1))
        p = jax.nn.softmax(jnp.concatenate([s_ctx, s_loc, s_snk], axis=-1), axis=-1)
        return (jnp.einsum('bkgql,blkd->bqkgd', p[..., :l].astype(vc.dtype), vc)
                + jnp.einsum('bkgqm,bmkd->bqkgd', p[..., l:l + 3 * Q_BLOCK].astype(vi.dtype), vi))

    o = lax.map(block, (qb, kb, vb, mask))
    return jnp.moveaxis(o, 0, 1).reshape(b, s, h, v.shape[-1])


def short_conv(u, w):
    ch = u.shape[-1]
    return lax.conv_general_dilated(u, w[:, None, :].astype(u.dtype), window_strides=(1,),
                                    padding=((CONV_K // 2, CONV_K // 2),),
                                    dimension_numbers=('NWC', 'WIO', 'NWC'), feature_group_count=ch)


def heads_ab(pp, cq_gain, ckv_gain, w_uq, w_ukv, q_gain, k_gain, qg_gain, kg_gain):
    cq, ckv, kr, gq, gk, gv, gate = split_cols(pp, AB_SPLIT)
    lead = pp.shape[:-1]
    qa = (rmsnorm(cq, cq_gain) @ w_uq).reshape(lead + (MLA_HEADS, MLA_NOPE + MLA_ROPE))
    kv = (rmsnorm(ckv, ckv_gain) @ w_ukv).reshape(lead + (MLA_HEADS, MLA_NOPE + MLA_V))
    kr_h = jnp.broadcast_to(kr[..., None, :], lead + (MLA_HEADS, MLA_ROPE))
    ka = jnp.concatenate([kv[..., :MLA_NOPE], kr_h], axis=-1)
    va = kv[..., MLA_NOPE:]
    qb = rmsnorm(gq.reshape(lead + (GQA_HEADS, GQA_DIM)), qg_gain)
    kb = rmsnorm(gk.reshape(lead + (GQA_KV_HEADS, GQA_DIM)), kg_gain)
    vb = gv.reshape(lead + (GQA_KV_HEADS, GQA_DIM))
    return rmsnorm(qa, q_gain), rmsnorm(ka, k_gain), va, qb, kb, vb, gate


def ab_layer(x, xc, mod, mod_c, w_in, w_out, cq_gain, ckv_gain, w_uq, w_ukv, q_gain, k_gain,
             qg_gain, kg_gain, row, col, update_ctx):
    b, s, _ = x.shape
    h, gate = adaln(x, mod)
    hc, gate_c = adaln(xc, mod_c)
    wts = (cq_gain, ckv_gain, w_uq, w_ukv, q_gain, k_gain, qg_gain, kg_gain)
    qa, ka, va, qb, kb, vb, g = heads_ab(h @ w_in, *wts)
    qac, kac, vac, qbc, kbc, vbc, gc = heads_ab(hc @ w_in, *wts)
    qa, ka = rope_tail(qa, MLA_ROPE, row, col), rope_tail(ka, MLA_ROPE, row, col)
    qb, kb = axial_rope(qb, row, col), axial_rope(kb, row, col)
    oa = joint_dense_attention(qa, ka, va, kac, vac).reshape(b, s, -1)
    ob = joint_dense_attention(qb, kb, vb, kbc, vbc).reshape(b, s, -1)
    mix = jnp.concatenate([oa, ob], axis=-1) * jax.nn.silu(g)
    x = x + gate * (mix @ w_out)
    if update_ctx:
        l = xc.shape[1]
        oac = ctx_attention(qac, kac, vac).reshape(b, l, -1)
        obc = ctx_attention(qbc, kbc, vbc).reshape(b, l, -1)
        mixc = jnp.concatenate([oac, obc], axis=-1) * jax.nn.silu(gc)
        xc = xc + gate_c * (mixc @ w_out)
    return x, xc


def heads_cd(pp, q_gain, k_gain):
    pq, pk, pv, gb, gcc, gh, gate = split_cols(pp, CD_SPLIT)
    lead = pp.shape[:-1]
    q = rmsnorm(pq.reshape(lead + (WIN_HEADS, WIN_DIM)), q_gain)
    k = rmsnorm(pk.reshape(lead + (WIN_KV_HEADS, WIN_DIM)), k_gain)
    v = pv.reshape(lead + (WIN_KV_HEADS, WIN_DIM))
    return q, k, v, gb, gcc, gh, gate


def cd_layer(x, xc, mod, mod_c, w_in, w_out, q_gain, k_gain, sink, conv_w, row, col, update_ctx):
    b, s, _ = x.shape
    h, gate = adaln(x, mod)
    hc, gate_c = adaln(xc, mod_c)
    q, k, v, gb, gcv, gh, g = heads_cd(h @ w_in, q_gain, k_gain)
    qc, kc, vc, gbc, gcvc, ghc, gc = heads_cd(hc @ w_in, q_gain, k_gain)
    q, k = axial_rope(q, row, col), axial_rope(k, row, col)
    oc = joint_window_attention(q, k, v, kc, vc, sink).reshape(b, s, -1)
    od = gb * short_conv(gcv * gh, conv_w)
    mix = jnp.concatenate([oc, od], axis=-1) * jax.nn.silu(g)
    x = x + gate * (mix @ w_out)
    if update_ctx:
        l = xc.shape[1]
        occ = ctx_attention(qc, kc, vc, sink).reshape(b, l, -1)
        odc = gbc * short_conv(gcvc * ghc, conv_w)
        mixc = jnp.concatenate([occ, odc], axis=-1) * jax.nn.silu(gc)
        xc = xc + gate_c * (mixc @ w_out)
    return x, xc


def setup_inputs(seed: int = 0) -> dict:
    key = jax.random.key(seed)
    ks = iter(jax.random.split(key, 32))
    f32 = jnp.float32
    n_ab = (DEPTH + 1) // 2
    n_cd = DEPTH // 2

    def nrm(shape, scale):
        return jax.random.normal(next(ks), shape, f32) * scale

    def gain(shape):
        return 1.0 + 0.1 * jax.random.normal(next(ks), shape, f32)

    return {
        'x': nrm((BATCH, SEQ, D_MODEL), 1.0),
        'c': nrm((BATCH, D_MODEL), 1.0),
        'ctx': nrm((BATCH, CTX_LEN, D_MODEL), 1.0),
        'c_ctx': nrm((D_MODEL,), 1.0),
        'mod_w': nrm((DEPTH, D_MODEL, 3 * D_MODEL), D_MODEL ** -0.5),
        'mod_b': nrm((DEPTH, 3 * D_MODEL), 0.02),
        'ab_w_in': nrm((n_ab, D_MODEL, AB_IN), D_MODEL ** -0.5),
        'ab_w_out': nrm((n_ab, AB_MIX, D_MODEL), AB_MIX ** -0.5),
        'mla_cq_gain': gain((n_ab, MLA_Q_RANK)),
        'mla_ckv_gain': gain((n_ab, MLA_KV_RANK)),
        'mla_w_uq': nrm((n_ab, MLA_Q_RANK, MLA_HEADS * (MLA_NOPE + MLA_ROPE)), MLA_Q_RANK ** -0.5),
        'mla_w_ukv': nrm((n_ab, MLA_KV_RANK, MLA_HEADS * (MLA_NOPE + MLA_V)), MLA_KV_RANK ** -0.5),
        'mla_q_gain': gain((n_ab, MLA_NOPE + MLA_ROPE)),
        'mla_k_gain': gain((n_ab, MLA_NOPE + MLA_ROPE)),
        'gqa_q_gain': gain((n_ab, GQA_DIM)),
        'gqa_k_gain': gain((n_ab, GQA_DIM)),
        'cd_w_in': nrm((n_cd, D_MODEL, CD_IN), D_MODEL ** -0.5),
        'cd_w_out': nrm((n_cd, CD_MIX, D_MODEL), CD_MIX ** -0.5),
        'win_q_gain': gain((n_cd, WIN_DIM)),
        'win_k_gain': gain((n_cd, WIN_DIM)),
        'win_sink': nrm((n_cd, WIN_HEADS), 0.5),
        'conv_w': nrm((n_cd, CONV_K, CONV_CH), CONV_K ** -0.5),
    }


def reference(x, c, ctx, c_ctx, mod_w, mod_b, ab_w_in, ab_w_out, mla_cq_gain, mla_ckv_gain,
              mla_w_uq, mla_w_ukv, mla_q_gain, mla_k_gain, gqa_q_gain, gqa_k_gain,
              cd_w_in, cd_w_out, win_q_gain, win_k_gain, win_sink, conv_w):
    n = x.shape[1]
    row, col = grid_positions(n)
    sc = jax.nn.silu(c)
    scc = jax.nn.silu(c_ctx)
    xc = ctx
    for i in range(DEPTH):
        mod = (sc @ mod_w[i] + mod_b[i])[:, None, :]
        mod_c = scc @ mod_w[i] + mod_b[i]
        j = i // 2
        update = i < DEPTH - 1
        if i % 2 == 0:
            x, xc = ab_layer(x, xc, mod, mod_c, ab_w_in[j], ab_w_out[j], mla_cq_gain[j], mla_ckv_gain[j],
                             mla_w_uq[j], mla_w_ukv[j], mla_q_gain[j], mla_k_gain[j],
                             gqa_q_gain[j], gqa_k_gain[j], row, col, update)
        else:
            x, xc = cd_layer(x, xc, mod, mod_c, cd_w_in[j], cd_w_out[j], win_q_gain[j], win_k_gain[j],
                             win_sink[j], conv_w[j], row, col, update)
    return x
```

```python
import functools
import math

import jax
import jax.numpy as jnp
from jax import lax
from jax.experimental import pallas as pl
from jax.experimental.pallas import tpu as pltpu

D_MODEL = 1024
GRID_W = 64
ROPE_BASE = 10000.0
EPS = 1e-6
NEG_INF = -1e30
WINDOW = 128

MLA_HEADS = 8
MLA_NOPE = 64
MLA_ROPE = 32
MLA_V = 64
MLA_QK = MLA_NOPE + MLA_ROPE
MLA_Q_RANK = 256
MLA_KV_RANK = 256
GQA_HEADS = 8
GQA_KV_HEADS = 2
GQA_DIM = 64
WIN_HEADS = 8
WIN_KV_HEADS = 2
WIN_DIM = 64
CONV_CH = 512
CONV_K = 3

HEAD_PAD = 128
LOG2E = math.log2(math.e)
VMEM_LIMIT_BYTES = 56 * 1024 * 1024

BF16 = jnp.bfloat16
F32 = jnp.float32


def _cparams(sem):
    return pltpu.CompilerParams(dimension_semantics=sem, vmem_limit_bytes=VMEM_LIMIT_BYTES)


def _dot(a, b):
    return jnp.dot(a, b, preferred_element_type=F32)


def _dot_nt(a, b):
    return lax.dot_general(a, b, (((1,), (1,)), ((), ())), preferred_element_type=F32)


def _silu(v):
    return v * (1.0 / (1.0 + jnp.exp(-v)))


def _mod_kernel(c_ref, w_ref, b_ref, o_ref):
    sc = _silu(c_ref[...])
    o_ref[...] = _dot(sc, w_ref[...]) + b_ref[...]


def _modulation(crows, mod_w, mod_b):
    depth, d, n = mod_w.shape
    tn = 1024
    return pl.pallas_call(
        _mod_kernel,
        grid=(depth, n // tn),
        in_specs=[
            pl.BlockSpec((8, d), lambda l, j: (0, 0)),
            pl.BlockSpec((None, d, tn), lambda l, j: (l, 0, j)),
            pl.BlockSpec((None, 1, tn), lambda l, j: (l, 0, j)),
        ],
        out_specs=pl.BlockSpec((None, 8, tn), lambda l, j: (l, 0, j)),
        out_shape=jax.ShapeDtypeStruct((depth, 8, n), F32),
        compiler_params=_cparams(("parallel", "parallel")),
        name="modulation",
    )(crows, mod_w, mod_b.reshape(depth, 1, n))


def _adaln_rows(x, shift, scale):
    ms = jnp.mean(x * x, axis=-1, keepdims=True)
    return (x * lax.rsqrt(ms + EPS)) * (1.0 + scale) + shift


def _rms_cols(v, gain_b, n):
    ss = jnp.sum(v * v, axis=0, keepdims=True)
    return v * lax.rsqrt(ss * (1.0 / n) + EPS) * gain_b


def _rope_cols(v, cos, sin, q):
    swap = jnp.concatenate([v[q:2 * q], v[0:q], v[3 * q:4 * q], v[2 * q:3 * q]], axis=0)
    return v * cos + swap * sin


def _proj_ab_kernel(x_ref, mod_ref, w_in_ref, cqg_ref, ckvg_ref, w_uq_ref, w_ukv_ref,
                    qg_ref, kg_ref, gqg_ref, gkg_ref, c32_ref, s32_ref, c64_ref, s64_ref,
                    qa_ref, ka_ref, va_ref, qb_ref, kb_ref, vb_ref, sg_ref):
    ts = x_ref.shape[0]
    h = _adaln_rows(x_ref[...], mod_ref[0:1, :], mod_ref[1:2, :]).astype(BF16)
    pp = _dot_nt(w_in_ref[...], h)

    cq = pp[0:256]
    ckv = pp[256:512]
    kr = pp[512:544]
    cqn = _rms_cols(cq, cqg_ref[...], MLA_Q_RANK).astype(BF16)
    ckvn = _rms_cols(ckv, ckvg_ref[...], MLA_KV_RANK).astype(BF16)
    qa = _dot(w_uq_ref[...], cqn)
    kv = _dot(w_ukv_ref[...], ckvn)
    qg = qg_ref[...]
    kg = kg_ref[...]
    c32 = c32_ref[...]
    s32 = s32_ref[...]
    kr_ss = jnp.sum(kr * kr, axis=0, keepdims=True)
    zpad = jnp.zeros((HEAD_PAD - MLA_QK, ts), BF16)
    q_scale = MLA_QK ** -0.5 * LOG2E
    for hd in range(MLA_HEADS):
        qh = _rms_cols(qa[hd * MLA_QK:(hd + 1) * MLA_QK], qg, MLA_QK)
        qr = _rope_cols(qh[MLA_NOPE:], c32, s32, MLA_ROPE // 4)
        qa_ref[hd, 0:MLA_NOPE, :] = (qh[0:MLA_NOPE] * q_scale).astype(BF16)
        qa_ref[hd, MLA_NOPE:MLA_QK, :] = (qr * q_scale).astype(BF16)
        qa_ref[hd, MLA_QK:HEAD_PAD, :] = zpad
        kn = kv[hd * 128:hd * 128 + MLA_NOPE]
        ss = jnp.sum(kn * kn, axis=0, keepdims=True) + kr_ss
        rs = lax.rsqrt(ss * (1.0 / MLA_QK) + EPS)
        krn = _rope_cols(kr * rs * kg[MLA_NOPE:], c32, s32, MLA_ROPE // 4)
        ka_ref[hd, 0:MLA_NOPE, :] = (kn * rs * kg[0:MLA_NOPE]).astype(BF16)
        ka_ref[hd, MLA_NOPE:MLA_QK, :] = krn.astype(BF16)
        ka_ref[hd, MLA_QK:HEAD_PAD, :] = zpad
        va_ref[hd] = kv[hd * 128 + MLA_NOPE:(hd + 1) * 128].astype(BF16)

    gq = pp[544:1056]
    gk = pp[1056:1184]
    gv = pp[1184:1312]
    c64 = c64_ref[...]
    s64 = s64_ref[...]
    gqg = gqg_ref[...]
    gkg = gkg_ref[...]
    g_scale = GQA_DIM ** -0.5 * LOG2E
    z64 = jnp.zeros((GQA_DIM, ts), BF16)
    grp = GQA_HEADS // GQA_KV_HEADS
    for hd in range(GQA_HEADS):
        qh = _rms_cols(gq[hd * GQA_DIM:(hd + 1) * GQA_DIM], gqg, GQA_DIM)
        qh = (_rope_cols(qh, c64, s64, GQA_DIM // 4) * g_scale).astype(BF16)
        for kvh in range(GQA_KV_HEADS):
            qb_ref[hd, kvh * GQA_DIM:(kvh + 1) * GQA_DIM, :] = qh if kvh == hd // grp else z64
    for kvh in range(GQA_KV_HEADS):
        kh = _rms_cols(gk[kvh * GQA_DIM:(kvh + 1) * GQA_DIM], gkg, GQA_DIM)
        kb_ref[kvh * GQA_DIM:(kvh + 1) * GQA_DIM, :] = _rope_cols(kh, c64, s64, GQA_DIM // 4).astype(BF16)
        vb_ref[kvh] = gv[kvh * GQA_DIM:(kvh + 1) * GQA_DIM].astype(BF16)
    sg_ref[...] = _silu(pp[1312:2336]).astype(BF16)


def _bcast_gain(g, ts):
    return jnp.broadcast_to(g.astype(F32)[:, None], (g.shape[0], ts))


def _proj_ab(x, mod2, w_in_t, w_uq_t, w_ukv_t, gains, tabs, ts):
    b, n, d = x.shape
    cqg, ckvg, qg, kg, gqg, gkg = [_bcast_gain(g, ts) for g in gains]
    c32, s32, c64, s64 = tabs
    full = lambda a: pl.BlockSpec(a.shape, lambda i, t: (0,) * a.ndim)
    tab = lambda a: pl.BlockSpec((a.shape[0], ts), lambda i, t: (0, t))
    outs = [
        jax.ShapeDtypeStruct((b, MLA_HEADS, HEAD_PAD, n), BF16),
        jax.ShapeDtypeStruct((b, MLA_HEADS, HEAD_PAD, n), BF16),
        jax.ShapeDtypeStruct((b, MLA_HEADS, MLA_V, n), BF16),
        jax.ShapeDtypeStruct((b, GQA_HEADS, HEAD_PAD, n), BF16),
        jax.ShapeDtypeStruct((b, GQA_KV_HEADS * GQA_DIM, n), BF16),
        jax.ShapeDtypeStruct((b, GQA_KV_HEADS, GQA_DIM, n), BF16),
        jax.ShapeDtypeStruct((b, D_MODEL, n), BF16),
    ]
    o4 = lambda s: pl.BlockSpec((None,) + s[1:3] + (ts,), lambda i, t: (i, 0, 0, t))
    o3 = lambda s: pl.BlockSpec((None, s[1], ts), lambda i, t: (i, 0, t))
    out_specs = [o4(o.shape) if len(o.shape) == 4 else o3(o.shape) for o in outs]
    return pl.pallas_call(
        _proj_ab_kernel,
        grid=(b, n // ts),
        in_specs=[
            pl.BlockSpec((None, ts, d), lambda i, t: (i, t, 0)),
            pl.BlockSpec((None, 2, d), lambda i, t: (i, 0, 0)),
            full(w_in_t), full(cqg), full(ckvg), full(w_uq_t), full(w_ukv_t),
            full(qg), full(kg), full(gqg), full(gkg),
            tab(c32), tab(s32), tab(c64), tab(s64),
        ],
        out_specs=out_specs,
        out_shape=outs,
        compiler_params=_cparams(("parallel", "parallel")),
        name="proj_ab",
    )(x, mod2, w_in_t, cqg, ckvg, w_uq_t, w_ukv_t, qg, kg, gqg, gkg, c32, s32, c64, s64)


def _flash_kernel(q_ref, k_ref, v_ref, kc_ref, vc_ref, o_ref, m_sc, l_sc, acc_sc, *, heads):
    j = pl.program_id(3)

    def block(kb, vb, first):
        for g in range(heads):
            s = _dot(kb, q_ref[g])
            mb = jnp.max(s, axis=0, keepdims=True)
            if first:
                m_new = mb
            else:
                m_old = m_sc[g]
                m_new = jnp.maximum(m_old, mb)
            p = jnp.exp2(s - m_new)
            ps = jnp.sum(p, axis=0, keepdims=True)
            pv = _dot(vb, p.astype(BF16))
            if first:
                l_sc[g] = ps
                acc_sc[g] = pv
            else:
                alpha = jnp.exp2(m_old - m_new)
                l_sc[g] = alpha * l_sc[g] + ps
                acc_sc[g] = alpha * acc_sc[g] + pv
            m_sc[g] = m_new

    @pl.when(j == 0)
    def _():
        block(kc_ref[...], vc_ref[...], True)

    block(k_ref[...], v_ref[...], False)

    @pl.when(j == pl.num_programs(3) - 1)
    def _():
        for g in range(heads):
            o_ref[g] = (acc_sc[g] / l_sc[g]).astype(o_ref.dtype)


def _flash(q_t, k, v_t, kc, vc_t, *, heads, tq, tk, name):
    b, hq, _, s = q_t.shape
    hk = k.shape[1]
    hv, dv = v_t.shape[1], v_t.shape[2]
    lc = kc.shape[2]
    kidx = (lambda h: h) if hk == hv else (lambda h: 0)
    return pl.pallas_call(
        functools.partial(_flash_kernel, heads=heads),
        grid=(b, hv, s // tq, s // tk),
        in_specs=[
            pl.BlockSpec((None, heads, HEAD_PAD, tq), lambda i, h, qi, j: (i, h, 0, qi)),
            pl.BlockSpec((None, None, tk, HEAD_PAD), lambda i, h, qi, j: (i, kidx(h), j, 0)),
            pl.BlockSpec((None, None, dv, tk), lambda i, h, qi, j: (i, h, 0, j)),
            pl.BlockSpec((None, None, lc, HEAD_PAD), lambda i, h, qi, j: (i, kidx(h), 0, 0)),
            pl.BlockSpec((None, None, dv, lc), lambda i, h, qi, j: (i, h, 0, 0)),
        ],
        out_specs=pl.BlockSpec((None, heads, dv, tq), lambda i, h, qi, j: (i, h, 0, qi)),
        out_shape=jax.ShapeDtypeStruct((b, hq, dv, s), BF16),
        scratch_shapes=[
            pltpu.VMEM((heads, 1, tq), F32),
            pltpu.VMEM((heads, 1, tq), F32),
            pltpu.VMEM((heads, dv, tq), F32),
        ],
        compiler_params=_cparams(("parallel", "parallel", "parallel", "arbitrary")),
        name=name,
    )(q_t, k, v_t, kc, vc_t)


def _ctx_attn_kernel(q_ref, k_ref, v_ref, *rest, heads, use_sink):
    if use_sink:
        sink_ref, o_ref = rest
    else:
        (o_ref,) = rest
    hv = pl.program_id(1)
    kb = k_ref[...]
    vb = v_ref[...]
    for g in range(heads):
        s = _dot(kb, q_ref[g])
        m = jnp.max(s, axis=0, keepdims=True)
        if use_sink:
            sk = sink_ref[hv * heads + g] * LOG2E
            m = jnp.maximum(m, sk)
        p = jnp.exp2(s - m)
        den = jnp.sum(p, axis=0, keepdims=True)
        if use_sink:
            den = den + jnp.exp2(sk - m)
        o_ref[g] = (_dot(vb, p.astype(BF16)) / den).astype(o_ref.dtype)


def _ctx_attn(q_t, k, v_t, sink, *, heads, name):
    b, hq, _, l = q_t.shape
    hk = k.shape[1]
    hv, dv = v_t.shape[1], v_t.shape[2]
    kidx = (lambda h: h) if hk == hv else (lambda h: 0)
    in_specs = [
        pl.BlockSpec((None, heads, HEAD_PAD, l), lambda i, h: (i, h, 0, 0)),
        pl.BlockSpec((None, None, l, HEAD_PAD), lambda i, h: (i, kidx(h), 0, 0)),
        pl.BlockSpec((None, None, dv, l), lambda i, h: (i, h, 0, 0)),
    ]
    args = [q_t, k, v_t]
    if sink is not None:
        in_specs.append(pl.BlockSpec(memory_space=pltpu.SMEM))
        args.append(sink)
    return pl.pallas_call(
        functools.partial(_ctx_attn_kernel, heads=heads, use_sink=sink is not None),
        grid=(b, hv),
        in_specs=in_specs,
        out_specs=pl.BlockSpec((None, heads, dv, l), lambda i, h: (i, h, 0, 0)),
        out_shape=jax.ShapeDtypeStruct((b, hq, dv, l), BF16),
        compiler_params=_cparams(("parallel", "parallel")),
        name=name,
    )(*args)


def _out_ab_kernel(x_ref, mix_ref, sg_ref, gate_ref, w_ref, o_ref):
    m = mix_ref[...] * sg_ref[...]
    y_t = _dot(w_ref[...], m)
    o_ref[...] = x_ref[...] + gate_ref[...] * y_t.T


def _out_ab(x, mix_t, sg_t, gate, w_out_t, ts):
    b, n, d = x.shape
    return pl.pallas_call(
        _out_ab_kernel,
        grid=(b, n // ts),
        in_specs=[
            pl.BlockSpec((None, ts, d), lambda i, t: (i, t, 0)),
            pl.BlockSpec((None, d, ts), lambda i, t: (i, 0, t)),
            pl.BlockSpec((None, d, ts), lambda i, t: (i, 0, t)),
            pl.BlockSpec((None, 1, d), lambda i, t: (i, 0, 0)),
            pl.BlockSpec((d, d), lambda i, t: (0, 0)),
        ],
        out_specs=pl.BlockSpec((None, ts, d), lambda i, t: (i, t, 0)),
        out_shape=jax.ShapeDtypeStruct((b, n, d), F32),
        compiler_params=_cparams(("parallel", "parallel")),
        name="out_ab",
    )(x, mix_t, sg_t, gate, w_out_t)


def _proj_cd_kernel(x_ref, mod_ref, w_in_ref, qg_ref, kg_ref, c64_ref, s64_ref,
                    q_ref, k_ref, v_ref, gb_ref, u_ref, sg_ref):
    ts = x_ref.shape[0]
    h = _adaln_rows(x_ref[...], mod_ref[0:1, :], mod_ref[1:2, :]).astype(BF16)
    pp = _dot_nt(w_in_ref[...], h)
    c64 = c64_ref[...]
    s64 = s64_ref[...]
    qg = qg_ref[...]
    kg = kg_ref[...]
    scale = WIN_DIM ** -0.5 * LOG2E
    z64 = jnp.zeros((WIN_DIM, ts), BF16)
    grp = WIN_HEADS // WIN_KV_HEADS
    for hd in range(WIN_HEADS):
        qh = _rms_cols(pp[hd * WIN_DIM:(hd + 1) * WIN_DIM], qg, WIN_DIM)
        qh = (_rope_cols(qh, c64, s64, WIN_DIM // 4) * scale).astype(BF16)
        for kvh in range(WIN_KV_HEADS):
            q_ref[hd, kvh * WIN_DIM:(kvh + 1) * WIN_DIM, :] = qh if kvh == hd // grp else z64
    for kvh in range(WIN_KV_HEADS):
        kh = _rms_cols(pp[512 + kvh * WIN_DIM:512 + (kvh + 1) * WIN_DIM], kg, WIN_DIM)
        k_ref[kvh * WIN_DIM:(kvh + 1) * WIN_DIM, :] = _rope_cols(kh, c64, s64, WIN_DIM // 4).astype(BF16)
        v_ref[kvh] = pp[640 + kvh * WIN_DIM:640 + (kvh + 1) * WIN_DIM].astype(BF16)
    gb_ref[...] = pp[768:1280].astype(BF16)
    u_ref[...] = (pp[1280:1792] * pp[1792:2304]).astype(BF16)
    sg_ref[...] = _silu(pp[2304:3328]).astype(BF16)


def _proj_cd(x, mod2, w_in_t, gains, tabs, ts):
    b, n, d = x.shape
    qg, kg = [_bcast_gain(g, ts) for g in gains]
    c64, s64 = tabs
    full = lambda a: pl.BlockSpec(a.shape, lambda i, t: (0,) * a.ndim)
    tab = lambda a: pl.BlockSpec((a.shape[0], ts), lambda i, t: (0, t))
    outs = [
        jax.ShapeDtypeStruct((b, WIN_HEADS, HEAD_PAD, n), BF16),
        jax.ShapeDtypeStruct((b, WIN_KV_HEADS * WIN_DIM, n), BF16),
        jax.ShapeDtypeStruct((b, WIN_KV_HEADS, WIN_DIM, n), BF16),
        jax.ShapeDtypeStruct((b, CONV_CH, n), BF16),
        jax.ShapeDtypeStruct((b, CONV_CH, n), BF16),
        jax.ShapeDtypeStruct((b, D_MODEL, n), BF16),
    ]
    o4 = lambda s: pl.BlockSpec((None,) + s[1:3] + (ts,), lambda i, t: (i, 0, 0, t))
    o3 = lambda s: pl.BlockSpec((None, s[1], ts), lambda i, t: (i, 0, t))
    out_specs = [o4(o.shape) if len(o.shape) == 4 else o3(o.shape) for o in outs]
    return pl.pallas_call(
        _proj_cd_kernel,
        grid=(b, n // ts),
        in_specs=[
            pl.BlockSpec((None, ts, d), lambda i, t: (i, t, 0)),
            pl.BlockSpec((None, 2, d), lambda i, t: (i, 0, 0)),
            full(w_in_t), full(qg), full(kg), tab(c64), tab(s64),
        ],
        out_specs=out_specs,
        out_shape=outs,
        compiler_params=_cparams(("parallel", "parallel")),
        name="proj_cd",
    )(x, mod2, w_in_t, qg, kg, c64, s64)


def _win_kernel(q_ref, kp_ref, kc_ref, kn_ref, vp_ref, vcur_ref, vn_ref, kctx_ref, vctx_ref, sink_ref,
                o_ref, *, heads, tq):
    hv = pl.program_id(1)
    qi = pl.program_id(2)
    nq = pl.num_programs(2)
    row_c = lax.broadcasted_iota(jnp.int32, (tq, tq), 0)
    col_c = lax.broadcasted_iota(jnp.int32, (tq, tq), 1)
    mask_c = jnp.abs(col_c - row_c) <= WINDOW
    row_h = lax.broadcasted_iota(jnp.int32, (WINDOW, tq), 0)
    col_h = lax.broadcasted_iota(jnp.int32, (WINDOW, tq), 1)
    mask_p = (col_h <= row_h) & (qi > 0)
    mask_n = (col_h >= row_h + (tq - WINDOW)) & (qi < nq - 1)
    kctx = kctx_ref[...]
    kp = kp_ref[...]
    kc = kc_ref[...]
    kn = kn_ref[...]
    for g in range(heads):
        q = q_ref[g]
        s_x = _dot(kctx, q)
        s_c = jnp.where(mask_c, _dot(kc, q), NEG_INF)
        s_p = jnp.where(mask_p, _dot(kp, q), NEG_INF)
        s_n = jnp.where(mask_n, _dot(kn, q), NEG_INF)
        sk = sink_ref[hv * heads + g] * LOG2E
        m = jnp.maximum(jnp.maximum(jnp.max(s_x, axis=0, keepdims=True), jnp.max(s_c, axis=0, keepdims=True)),
                        jnp.maximum(jnp.max(s_p, axis=0, keepdims=True), jnp.max(s_n, axis=0, keepdims=True)))
        m = jnp.maximum(m, sk)
        p_x = jnp.exp2(s_x - m)
        p_c = jnp.exp2(s_c - m)
        p_p = jnp.exp2(s_p - m)
        p_n = jnp.exp2(s_n - m)
        den = (jnp.sum(p_x, axis=0, keepdims=True) + jnp.sum(p_c, axis=0, keepdims=True)
               + jnp.sum(p_p, axis=0, keepdims=True) + jnp.sum(p_n, axis=0, keepdims=True)
               + jnp.exp2(sk - m))
        acc = (_dot(vctx_ref[...], p_x.astype(BF16)) + _dot(vcur_ref[...], p_c.astype(BF16))
               + _dot(vp_ref[...], p_p.astype(BF16)) + _dot(vn_ref[...], p_n.astype(BF16)))
        o_ref[g] = (acc / den).astype(o_ref.dtype)


def _win_attn(q_t, k, v_t, kc, vc_t, sink, *, tq):
    b, hq, _, s = q_t.shape
    hv, dv = v_t.shape[1], v_t.shape[2]
    heads = hq // hv
    lc = kc.shape[2]
    r = tq // WINDOW
    nb = s // WINDOW
    prev = lambda qi: jnp.maximum(qi * r - 1, 0)
    nxt = lambda qi: jnp.minimum((qi + 1) * r, nb - 1)
    return pl.pallas_call(
        functools.partial(_win_kernel, heads=heads, tq=tq),
        grid=(b, hv, s // tq),
        in_specs=[
            pl.BlockSpec((None, heads, HEAD_PAD, tq), lambda i, h, qi: (i, h, 0, qi)),
            pl.BlockSpec((None, None, WINDOW, HEAD_PAD), lambda i, h, qi: (i, 0, prev(qi), 0)),
            pl.BlockSpec((None, None, tq, HEAD_PAD), lambda i, h, qi: (i, 0, qi, 0)),
            pl.BlockSpec((None, None, WINDOW, HEAD_PAD), lambda i, h, qi: (i, 0, nxt(qi), 0)),
            pl.BlockSpec((None, None, dv, WINDOW), lambda i, h, qi: (i, h, 0, prev(qi))),
            pl.BlockSpec((None, None, dv, tq), lambda i, h, qi: (i, h, 0, qi)),
            pl.BlockSpec((None, None, dv, WINDOW), lambda i, h, qi: (i, h, 0, nxt(qi))),
            pl.BlockSpec((None, None, lc, HEAD_PAD), lambda i, h, qi: (i, 0, 0, 0)),
            pl.BlockSpec((None, None, dv, lc), lambda i, h, qi: (i, h, 0, 0)),
            pl.BlockSpec(memory_space=pltpu.SMEM),
        ],
        out_specs=pl.BlockSpec((None, heads, dv, tq), lambda i, h, qi: (i, h, 0, qi)),
        out_shape=jax.ShapeDtypeStruct((b, hq, dv, s), BF16),
        compiler_params=_cparams(("parallel", "parallel", "parallel")),
        name="win_attn",
    )(q_t, k, k, k, v_t, v_t, v_t, kc, vc_t, sink)


def _out_cd_kernel(x_ref, oc_ref, gb_ref, up_ref, u_ref, un_ref, cw_ref, sg_ref, gate_ref, w_ref, o_ref, *, ts):
    t = pl.program_id(1)
    nt = pl.num_programs(1)
    u = u_ref[...].astype(F32)
    lane = lax.broadcasted_iota(jnp.int32, (CONV_CH, ts), 1)
    halo_l = pltpu.roll(up_ref[...].astype(F32), 1, axis=1)
    halo_l = jnp.where(t > 0, halo_l, 0.0)
    halo_r = pltpu.roll(un_ref[...].astype(F32), 127, axis=1)
    halo_r = jnp.where(t < nt - 1, halo_r, 0.0)
    reps = ts // 128
    um = jnp.where(lane == 0, jnp.tile(halo_l, (1, reps)), pltpu.roll(u, 1, axis=1))
    up = jnp.where(lane == ts - 1, jnp.tile(halo_r, (1, reps)), pltpu.roll(u, ts - 1, axis=1))
    w0 = jnp.tile(cw_ref[0], (1, reps))
    w1 = jnp.tile(cw_ref[1], (1, reps))
    w2 = jnp.tile(cw_ref[2], (1, reps))
    od = gb_ref[...].astype(F32) * (w0 * um + w1 * u + w2 * up)
    sg = sg_ref[...]
    half = D_MODEL - CONV_CH
    m = jnp.concatenate([oc_ref[...] * sg[0:half], od.astype(BF16) * sg[half:]], axis=0)
    y_t = _dot(w_ref[...], m)
    o_ref[...] = x_ref[...] + gate_ref[...] * y_t.T


def _out_cd(x, oc_t, gb_t, u_t, conv_wb, sg_t, gate, w_out_t, ts):
    b, n, d = x.shape
    r = ts // 128
    nb = n // 128
    return pl.pallas_call(
        functools.partial(_out_cd_kernel, ts=ts),
        grid=(b, n // ts),
        in_specs=[
            pl.BlockSpec((None, ts, d), lambda i, t: (i, t, 0)),
            pl.BlockSpec((None, d - CONV_CH, ts), lambda i, t: (i, 0, t)),
            pl.BlockSpec((None, CONV_CH, ts), lambda i, t: (i, 0, t)),
            pl.BlockSpec((None, CONV_CH, 128), lambda i, t: (i, 0, jnp.maximum(t * r - 1, 0))),
            pl.BlockSpec((None, CONV_CH, ts), lambda i, t: (i, 0, t)),
            pl.BlockSpec((None, CONV_CH, 128), lambda i, t: (i, 0, jnp.minimum((t + 1) * r, nb - 1))),
            pl.BlockSpec((CONV_K, CONV_CH, 128), lambda i, t: (0, 0, 0)),
            pl.BlockSpec((None, d, ts), lambda i, t: (i, 0, t)),
            pl.BlockSpec((None, 1, d), lambda i, t: (i, 0, 0)),
            pl.BlockSpec((d, d), lambda i, t: (0, 0)),
        ],
        out_specs=pl.BlockSpec((None, ts, d), lambda i, t: (i, t, 0)),
        out_shape=jax.ShapeDtypeStruct((b, n, d), F32),
        compiler_params=_cparams(("parallel", "parallel")),
        name="out_cd",
    )(x, oc_t, gb_t, u_t, u_t, u_t, conv_wb, sg_t, gate, w_out_t)


def _rope_tables(n, dd, identity=False):
    q = dd // 4
    if identity:
        return jnp.ones((dd, n), F32), jnp.zeros((dd, n), F32)
    rows = n // GRID_W
    row = jnp.repeat(jnp.arange(rows, dtype=F32), GRID_W)
    col = jnp.tile(jnp.arange(GRID_W, dtype=F32), rows)
    inv = ROPE_BASE ** (-jnp.arange(q, dtype=F32) / q)
    ar = inv[:, None] * row[None, :]
    ac = inv[:, None] * col[None, :]
    cos = jnp.concatenate([jnp.cos(ar), jnp.cos(ar), jnp.cos(ac), jnp.cos(ac)], axis=0)
    sin = jnp.concatenate([-jnp.sin(ar), jnp.sin(ar), -jnp.sin(ac), jnp.sin(ac)], axis=0)
    return cos, sin


def _tile_rows(n, cap):
    t = cap
    while n % t:
        t //= 2
    return t


def kernel(x, c, ctx, c_ctx, mod_w, mod_b, ab_w_in, ab_w_out, mla_cq_gain, mla_ckv_gain, mla_w_uq, mla_w_ukv,
           mla_q_gain, mla_k_gain, gqa_q_gain, gqa_k_gain, cd_w_in, cd_w_out, win_q_gain, win_k_gain,
           win_sink, conv_w):
    b, s, d = x.shape
    lc = ctx.shape[1]
    ts_x = _tile_rows(s, 512)
    ts_c = _tile_rows(lc, 256)
    tq = _tile_rows(s, 512)
    tk = _tile_rows(s, 512)
    tq_w = _tile_rows(s, 256)

    crows = jnp.zeros((8, d), F32).at[0:b].set(c).at[b].set(c_ctx)
    mods = _modulation(crows, mod_w, mod_b)

    def split_mod(layer):
        m = mods[layer]
        shift, scale, gate = m[:, 0:d], m[:, d:2 * d], m[:, 2 * d:3 * d]
        mod_x = jnp.stack([shift[0:b], scale[0:b]], axis=1)
        mod_c = jnp.broadcast_to(jnp.stack([shift[b], scale[b]], axis=0)[None], (b, 2, d))
        gate_x = gate[0:b][:, None, :]
        gate_c = jnp.broadcast_to(gate[b][None, None, :], (b, 1, d))
        return mod_x, mod_c, gate_x, gate_c

    tabs_x = _rope_tables(s, MLA_ROPE) + _rope_tables(s, GQA_DIM)
    tabs_c = _rope_tables(lc, MLA_ROPE, True) + _rope_tables(lc, GQA_DIM, True)

    mod_x, mod_c, gate_x, gate_c = split_mod(0)
    w_in_t = ab_w_in[0].T.astype(BF16)
    w_uq_t = mla_w_uq[0].T.astype(BF16)
    w_ukv_t = mla_w_ukv[0].T.astype(BF16)
    w_out_t = ab_w_out[0].T.astype(BF16)
    gains = (mla_cq_gain[0], mla_ckv_gain[0], mla_q_gain[0], mla_k_gain[0], gqa_q_gain[0], gqa_k_gain[0])
    qa, ka, va, qb, kb, vb, sg = _proj_ab(x, mod_x, w_in_t, w_uq_t, w_ukv_t, gains, tabs_x, ts_x)
    qac, kac, vac, qbc, kbc, vbc, sgc = _proj_ab(ctx, mod_c, w_in_t, w_uq_t, w_ukv_t, gains, tabs_c, ts_c)
    ka_n = jnp.swapaxes(ka, 2, 3)
    kac_n = jnp.swapaxes(kac, 2, 3)
    kb_n = jnp.swapaxes(kb, 1, 2)[:, None]
    kbc_n = jnp.swapaxes(kbc, 1, 2)[:, None]
    oa = _flash(qa, ka_n, va, kac_n, vac, heads=1, tq=tq, tk=tk, name="flash_mla")
    ob = _flash(qb, kb_n, vb, kbc_n, vbc, heads=GQA_HEADS // GQA_KV_HEADS, tq=tq, tk=tk, name="flash_gqa")
    mix_t = jnp.concatenate([oa.reshape(b, -1, s), ob.reshape(b, -1, s)], axis=1)
    x1 = _out_ab(x, mix_t, sg, gate_x, w_out_t, ts_x)
    oac = _ctx_attn(qac, kac_n, vac, None, heads=1, name="ctx_mla")
    obc = _ctx_attn(qbc, kbc_n, vbc, None, heads=GQA_HEADS // GQA_KV_HEADS, name="ctx_gqa")
    mixc_t = jnp.concatenate([oac.reshape(b, -1, lc), obc.reshape(b, -1, lc)], axis=1)
    xc1 = _out_ab(ctx, mixc_t, sgc, gate_c, w_out_t, ts_c)

    mod_x, mod_c, gate_x, gate_c = split_mod(1)
    w_in_t = cd_w_in[0].T.astype(BF16)
    w_out_t = cd_w_out[0].T.astype(BF16)
    gains = (win_q_gain[0], win_k_gain[0])
    q, k, v, gb, u, sg = _proj_cd(x1, mod_x, w_in_t, gains, tabs_x[2:], ts_x)
    _, kc, vc, _, _, _ = _proj_cd(xc1, mod_c, w_in_t, gains, tabs_c[2:], ts_c)
    k_n = jnp.swapaxes(k, 1, 2)[:, None]
    kc_n = jnp.swapaxes(kc, 1, 2)[:, None]
    oc = _win_attn(q, k_n, v, kc_n, vc, win_sink[0].astype(F32), tq=tq_w)
    conv_wb = jnp.broadcast_to(conv_w[0].astype(F32)[:, :, None], (CONV_K, CONV_CH, 128))
    return _out_cd(x1, oc.reshape(b, -1, s), gb, u, conv_wb, sg, gate_x, w_out_t, ts_x)
```

```python
import functools
import math

import jax
import jax.numpy as jnp
from jax import lax
from jax.experimental import pallas as pl
from jax.experimental.pallas import tpu as pltpu

D_MODEL = 1024
GRID_W = 64
ROPE_BASE = 10000.0
EPS = 1e-6
NEG_INF = -1e30
WINDOW = 128

MLA_HEADS = 8
MLA_NOPE = 64
MLA_ROPE = 32
MLA_V = 64
MLA_QK = MLA_NOPE + MLA_ROPE
MLA_Q_RANK = 256
MLA_KV_RANK = 256
GQA_HEADS = 8
GQA_KV_HEADS = 2
GQA_DIM = 64
WIN_HEADS = 8
WIN_KV_HEADS = 2
WIN_DIM = 64
CONV_CH = 512
CONV_K = 3

HEAD_PAD = 128
ONES_ROWS = 16
LOG2E = math.log2(math.e)
VMEM_LIMIT_BYTES = 56 * 1024 * 1024

BF16 = jnp.bfloat16
F32 = jnp.float32


def _cparams(sem, flags=None):
    return pltpu.CompilerParams(dimension_semantics=sem, vmem_limit_bytes=VMEM_LIMIT_BYTES, flags=flags)


FLASH_FLAGS = None


def _dot(a, b):
    return jnp.dot(a, b, preferred_element_type=F32)


def _dot_nt(a, b):
    return lax.dot_general(a, b, (((1,), (1,)), ((), ())), preferred_element_type=F32)


def _silu(v):
    return v * (1.0 / (1.0 + jnp.exp(-v)))


def _mod_kernel(c_ref, w_ref, b_ref, o_ref):
    sc = _silu(c_ref[...])
    o_ref[...] = _dot(sc, w_ref[...]) + b_ref[...]


def _modulation(crows, mod_w, mod_b):
    depth, d, n = mod_w.shape
    tn = 1024
    return pl.pallas_call(
        _mod_kernel,
        grid=(depth, n // tn),
        in_specs=[
            pl.BlockSpec((8, d), lambda l, j: (0, 0)),
            pl.BlockSpec((None, d, tn), lambda l, j: (l, 0, j)),
            pl.BlockSpec((None, 1, tn), lambda l, j: (l, 0, j)),
        ],
        out_specs=pl.BlockSpec((None, 8, tn), lambda l, j: (l, 0, j)),
        out_shape=jax.ShapeDtypeStruct((depth, 8, n), F32),
        compiler_params=_cparams(("parallel", "parallel")),
        name="modulation",
    )(crows, mod_w, mod_b.reshape(depth, 1, n))


def _adaln_rows(x, shift, scale):
    ms = jnp.mean(x * x, axis=-1, keepdims=True)
    return (x * lax.rsqrt(ms + EPS)) * (1.0 + scale) + shift


def _rms_cols(v, gain_b, n):
    ss = jnp.sum(v * v, axis=0, keepdims=True)
    return v * lax.rsqrt(ss * (1.0 / n) + EPS) * gain_b


def _rope_cols(v, cos, sin, q):
    swap = jnp.concatenate([v[q:2 * q], v[0:q], v[3 * q:4 * q], v[2 * q:3 * q]], axis=0)
    return v * cos + swap * sin


def _proj_ab_kernel(x_ref, mod_ref, w_in_ref, cqg_ref, ckvg_ref, w_uq_ref, w_ukv_ref,
                    qg_ref, kg_ref, gqg_ref, gkg_ref, c32_ref, s32_ref, c64_ref, s64_ref,
                    qa_ref, ka_ref, va_ref, qb_ref, kb_ref, vb_ref, sg_ref):
    ts = x_ref.shape[0]
    h = _adaln_rows(x_ref[...], mod_ref[0:1, :], mod_ref[1:2, :]).astype(BF16)
    pp = _dot_nt(w_in_ref[...], h)

    cq = pp[0:256]
    ckv = pp[256:512]
    kr = pp[512:544]
    cqn = _rms_cols(cq, cqg_ref[...], MLA_Q_RANK).astype(BF16)
    ckvn = _rms_cols(ckv, ckvg_ref[...], MLA_KV_RANK).astype(BF16)
    qa = _dot(w_uq_ref[...], cqn)
    kv = _dot(w_ukv_ref[...], ckvn)
    qg = qg_ref[...]
    kg = kg_ref[...]
    c32 = c32_ref[...]
    s32 = s32_ref[...]
    kr_ss = jnp.sum(kr * kr, axis=0, keepdims=True)
    zpad = jnp.zeros((HEAD_PAD - MLA_QK, ts), BF16)
    q_scale = MLA_QK ** -0.5 * LOG2E
    for hd in range(MLA_HEADS):
        qh = _rms_cols(qa[hd * MLA_QK:(hd + 1) * MLA_QK], qg, MLA_QK)
        qr = _rope_cols(qh[MLA_NOPE:], c32, s32, MLA_ROPE // 4)
        qa_ref[hd, 0:MLA_NOPE, :] = (qh[0:MLA_NOPE] * q_scale).astype(BF16)
        qa_ref[hd, MLA_NOPE:MLA_QK, :] = (qr * q_scale).astype(BF16)
        qa_ref[hd, MLA_QK:HEAD_PAD, :] = zpad
        kn = kv[hd * 128:hd * 128 + MLA_NOPE]
        ss = jnp.sum(kn * kn, axis=0, keepdims=True) + kr_ss
        rs = lax.rsqrt(ss * (1.0 / MLA_QK) + EPS)
        krn = _rope_cols(kr * rs * kg[MLA_NOPE:], c32, s32, MLA_ROPE // 4)
        ka_ref[hd, 0:MLA_NOPE, :] = (kn * rs * kg[0:MLA_NOPE]).astype(BF16)
        ka_ref[hd, MLA_NOPE:MLA_QK, :] = krn.astype(BF16)
        ka_ref[hd, MLA_QK:HEAD_PAD, :] = zpad
        va_ref[hd] = kv[hd * 128 + MLA_NOPE:(hd + 1) * 128].astype(BF16)

    gq = pp[544:1056]
    gk = pp[1056:1184]
    gv = pp[1184:1312]
    c64 = c64_ref[...]
    s64 = s64_ref[...]
    gqg = gqg_ref[...]
    gkg = gkg_ref[...]
    g_scale = GQA_DIM ** -0.5 * LOG2E
    z64 = jnp.zeros((GQA_DIM, ts), BF16)
    grp = GQA_HEADS // GQA_KV_HEADS
    for hd in range(GQA_HEADS):
        qh = _rms_cols(gq[hd * GQA_DIM:(hd + 1) * GQA_DIM], gqg, GQA_DIM)
        qh = (_rope_cols(qh, c64, s64, GQA_DIM // 4) * g_scale).astype(BF16)
        for kvh in range(GQA_KV_HEADS):
            qb_ref[hd, kvh * GQA_DIM:(kvh + 1) * GQA_DIM, :] = qh if kvh == hd // grp else z64
    for kvh in range(GQA_KV_HEADS):
        kh = _rms_cols(gk[kvh * GQA_DIM:(kvh + 1) * GQA_DIM], gkg, GQA_DIM)
        kb_ref[kvh * GQA_DIM:(kvh + 1) * GQA_DIM, :] = _rope_cols(kh, c64, s64, GQA_DIM // 4).astype(BF16)
        vb_ref[kvh] = gv[kvh * GQA_DIM:(kvh + 1) * GQA_DIM].astype(BF16)
    sg_ref[...] = _silu(pp[1312:2336]).astype(BF16)


def _bcast_gain(g, ts):
    return jnp.broadcast_to(g.astype(F32)[:, None], (g.shape[0], ts))


def _proj_ab(x, mod2, w_in_t, w_uq_t, w_ukv_t, gains, tabs, ts):
    b, n, d = x.shape
    cqg, ckvg, qg, kg, gqg, gkg = [_bcast_gain(g, ts) for g in gains]
    c32, s32, c64, s64 = tabs
    full = lambda a: pl.BlockSpec(a.shape, lambda i, t: (0,) * a.ndim)
    tab = lambda a: pl.BlockSpec((a.shape[0], ts), lambda i, t: (0, t))
    outs = [
        jax.ShapeDtypeStruct((b, MLA_HEADS, HEAD_PAD, n), BF16),
        jax.ShapeDtypeStruct((b, MLA_HEADS, HEAD_PAD, n), BF16),
        jax.ShapeDtypeStruct((b, MLA_HEADS, MLA_V, n), BF16),
        jax.ShapeDtypeStruct((b, GQA_HEADS, HEAD_PAD, n), BF16),
        jax.ShapeDtypeStruct((b, GQA_KV_HEADS * GQA_DIM, n), BF16),
        jax.ShapeDtypeStruct((b, GQA_KV_HEADS, GQA_DIM, n), BF16),
        jax.ShapeDtypeStruct((b, D_MODEL, n), BF16),
    ]
    o4 = lambda s: pl.BlockSpec((None,) + s[1:3] + (ts,), lambda i, t: (i, 0, 0, t))
    o3 = lambda s: pl.BlockSpec((None, s[1], ts), lambda i, t: (i, 0, t))
    out_specs = [o4(o.shape) if len(o.shape) == 4 else o3(o.shape) for o in outs]
    return pl.pallas_call(
        _proj_ab_kernel,
        grid=(b, n // ts),
        in_specs=[
            pl.BlockSpec((None, ts, d), lambda i, t: (i, t, 0)),
            pl.BlockSpec((None, 2, d), lambda i, t: (i, 0, 0)),
            full(w_in_t), full(cqg), full(ckvg), full(w_uq_t), full(w_ukv_t),
            full(qg), full(kg), full(gqg), full(gkg),
            tab(c32), tab(s32), tab(c64), tab(s64),
        ],
        out_specs=out_specs,
        out_shape=outs,
        compiler_params=_cparams(("parallel", "parallel")),
        name="proj_ab",
    )(x, mod2, w_in_t, cqg, ckvg, w_uq_t, w_ukv_t, qg, kg, gqg, gkg, c32, s32, c64, s64)


def _interleave(*gens):
    live = [g for g in gens if g is not None]
    while live:
        nxt = []
        for g in live:
            try:
                next(g)
                nxt.append(g)
            except StopIteration:
                pass
        live = nxt


def _flash_kernel(q_ref, k_ref, v_ref, o_ref, m_sc, acc_sc, s_sc, *, heads, kc):
    j = pl.program_id(2)
    tk = k_ref.shape[1]
    tq = q_ref.shape[2]
    dv = o_ref.shape[1]
    nc = tk // kc
    gk = k_ref.shape[0]
    gv = v_ref.shape[0]

    @pl.when(j == 0)
    def _():
        m_sc[...] = jnp.full(m_sc.shape, NEG_INF, F32)
        acc_sc[...] = jnp.zeros(acc_sc.shape, F32)

    stats = {}
    token = [None]

    def pass1(g):
        slot = g % 2
        q = q_ref[g]
        mx = None
        for c in range(nc):
            kch = k_ref[g * gk // heads, c * kc:(c + 1) * kc, :]
            if token[0] is not None:
                kw = pltpu.bitcast(kch, jnp.uint32).reshape(kc // 16, 8, HEAD_PAD) | token[0][None]
                kch = pltpu.bitcast(kw.reshape(kc // 2, HEAD_PAD), BF16)
            s = _dot(kch, q)
            s_sc[slot, c * kc:(c + 1) * kc, :] = s
            part = jnp.max(s.reshape(kc // 8, 8, tq), axis=0)
            mx = part if mx is None else jnp.maximum(mx, part)
            yield
        m_old = m_sc[g]
        m_new = jnp.maximum(m_old, jnp.max(mx, axis=0, keepdims=True))
        m_sc[g] = m_new
        stats[g] = (m_new, jnp.exp2(m_old - m_new))

    def pass2(g):
        slot = g % 2
        m_new, alpha = stats[g]
        mb = jnp.broadcast_to(m_new, (8, tq))
        pv = None
        for c in range(nc):
            s = s_sc[slot, c * kc:(c + 1) * kc, :]
            p = jnp.exp2(s.reshape(kc // 8, 8, tq) - mb[None])
            word = pltpu.bitcast(p[0, :, 0:HEAD_PAD], jnp.uint32)
            token[0] = lax.shift_right_logical(lax.shift_right_logical(word, jnp.uint32(16)), jnp.uint32(16))
            d = _dot(v_ref[g * gv // heads, :, c * kc:(c + 1) * kc], p.reshape(kc, tq).astype(BF16))
            pv = d if pv is None else pv + d
            yield
        acc_sc[g] = alpha * acc_sc[g] + pv

    for i in range(heads + 1):
        _interleave(pass2(i - 1) if i > 0 else None, pass1(i) if i < heads else None)

    @pl.when(j == pl.num_programs(2) - 1)
    def _():
        for g in range(heads):
            o_ref[g] = (acc_sc[g, 0:dv, :] / acc_sc[g, dv:dv + 1, :]).astype(o_ref.dtype)


def _flash(q_t, k, v_t, *, tq, tk, kc, name):
    b, hq, _, s = q_t.shape
    hk, n = k.shape[1], k.shape[2]
    hv, dv = v_t.shape[1], v_t.shape[2]
    v_t = jnp.concatenate([v_t, jnp.ones((b, hv, ONES_ROWS, n), v_t.dtype)], axis=2)
    return pl.pallas_call(
        functools.partial(_flash_kernel, heads=hq, kc=kc),
        grid=(b, s // tq, n // tk),
        in_specs=[
            pl.BlockSpec((None, hq, HEAD_PAD, tq), lambda i, qi, j: (i, 0, 0, qi)),
            pl.BlockSpec((None, hk, tk, HEAD_PAD), lambda i, qi, j: (i, 0, j, 0)),
            pl.BlockSpec((None, hv, dv + ONES_ROWS, tk), lambda i, qi, j: (i, 0, 0, j)),
        ],
        out_specs=pl.BlockSpec((None, hq, dv, tq), lambda i, qi, j: (i, 0, 0, qi)),
        out_shape=jax.ShapeDtypeStruct((b, hq, dv, s), BF16),
        scratch_shapes=[
            pltpu.VMEM((hq, 1, tq), F32),
            pltpu.VMEM((hq, dv + ONES_ROWS, tq), F32),
            pltpu.VMEM((2, tk, tq), F32),
        ],
        compiler_params=_cparams(("parallel", "parallel", "arbitrary"), FLASH_FLAGS),
        name=name,
    )(q_t, k, v_t)


P_LIMIT = 2.0 ** 100
FIRST_REF_KEYS = 128


def _stream_kernel(q_ref, k_ref, v_ref, o_ref, m_sc, acc_sc, pv_sc, *, heads, kc, ahead):
    j = pl.program_id(2)
    tk = k_ref.shape[1]
    tq = q_ref.shape[2]
    dv = o_ref.shape[1]
    nc = tk // kc
    gk = k_ref.shape[0]
    gv = v_ref.shape[0]
    cur = lax.rem(j, 2)
    nxt = 1 - cur

    @pl.when(j == 0)
    def _():
        for g in range(heads):
            s0 = _dot(k_ref[g * gk // heads, 0:FIRST_REF_KEYS, :], q_ref[g])
            m_sc[0, g] = jnp.max(s0, axis=0, keepdims=True)
        acc_sc[0] = jnp.zeros(acc_sc.shape[1:], F32)

    units = [(g, c) for g in range(heads) for c in range(nc)]
    pending = {}

    def emit_scores(u):
        g, c = u
        pending[u] = _dot(k_ref[g * gk // heads, c * kc:(c + 1) * kc, :], q_ref[g])

    worst = None
    issued = 0
    pmx = None
    for i, (g, c) in enumerate(units):
        while issued <= min(i + ahead, len(units) - 1):
            emit_scores(units[issued])
            issued += 1
        s = pending.pop((g, c))
        m_ref = m_sc[cur, g]
        mb = jnp.broadcast_to(m_ref, (8, tq))
        p = jnp.exp2(s.reshape(kc // 8, 8, tq) - mb[None]).reshape(kc, tq).astype(BF16)
        part = jnp.max(p.reshape(kc // 16, 16, tq), axis=0)
        pmx = part if c == 0 else jnp.maximum(pmx, part)
        d = _dot(v_ref[g * gv // heads, :, c * kc:(c + 1) * kc], p)
        if c == 0:
            pv_sc[g] = d
        else:
            pv_sc[g] += d
        if c == nc - 1:
            pm = jnp.maximum(jnp.max(pmx.astype(F32), axis=0, keepdims=True), 1.0)
            worst = pm if worst is None else jnp.maximum(worst, pm)
            m_sc[nxt, g] = m_ref + jnp.log2(pm)
            acc_sc[nxt, g] = (acc_sc[cur, g] + pv_sc[g]) * (1.0 / pm)

    redo = jnp.logical_not(jnp.max(worst) <= P_LIMIT)

    @pl.when(redo)
    def _():
        def head(g, carry):
            q = q_ref[g]
            gki = g * gk // heads
            gvi = g * gv // heads
            m_old = m_sc[cur, g]

            def cmax(c, mx):
                s = _dot(k_ref[gki, pl.ds(pl.multiple_of(c * kc, kc), kc), :], q)
                return jnp.maximum(mx, jnp.max(s, axis=0, keepdims=True))

            m_new = lax.fori_loop(0, nc, cmax, m_old)

            def cacc(c, pv):
                ks = pl.ds(pl.multiple_of(c * kc, kc), kc)
                s = _dot(k_ref[gki, ks, :], q)
                return pv + _dot(v_ref[gvi, :, ks], jnp.exp2(s - m_new).astype(BF16))

            pv = lax.fori_loop(0, nc, cacc, jnp.zeros(acc_sc.shape[2:], F32))
            m_sc[nxt, g] = m_new
            acc_sc[nxt, g] = acc_sc[cur, g] * jnp.exp2(m_old - m_new) + pv
            return carry

        lax.fori_loop(0, heads, head, 0)

    @pl.when(j == pl.num_programs(2) - 1)
    def _():
        for g in range(heads):
            o_ref[g] = (acc_sc[nxt, g, 0:dv, :] / acc_sc[nxt, g, dv:dv + 1, :]).astype(o_ref.dtype)


def _stream_attn(q_t, k, v_t, *, tq, tk, kc, ahead, name):
    b, hq, _, s = q_t.shape
    hk, n = k.shape[1], k.shape[2]
    hv, dv = v_t.shape[1], v_t.shape[2]
    v_t = jnp.concatenate([v_t, jnp.ones((b, hv, ONES_ROWS, n), v_t.dtype)], axis=2)
    return pl.pallas_call(
        functools.partial(_stream_kernel, heads=hq, kc=kc, ahead=ahead),
        grid=(b, s // tq, n // tk),
        in_specs=[
            pl.BlockSpec((None, hq, HEAD_PAD, tq), lambda i, qi, j: (i, 0, 0, qi)),
            pl.BlockSpec((None, hk, tk, HEAD_PAD), lambda i, qi, j: (i, 0, j, 0)),
            pl.BlockSpec((None, hv, dv + ONES_ROWS, tk), lambda i, qi, j: (i, 0, 0, j)),
        ],
        out_specs=pl.BlockSpec((None, hq, dv, tq), lambda i, qi, j: (i, 0, 0, qi)),
        out_shape=jax.ShapeDtypeStruct((b, hq, dv, s), BF16),
        scratch_shapes=[
            pltpu.VMEM((2, hq, 1, tq), F32),
            pltpu.VMEM((2, hq, dv + ONES_ROWS, tq), F32),
            pltpu.VMEM((hq, dv + ONES_ROWS, tq), F32),
        ],
        compiler_params=_cparams(("parallel", "parallel", "arbitrary"), FLASH_FLAGS),
        name=name,
    )(q_t, k, v_t)


def _ctx_attn_kernel(q_ref, k_ref, v_ref, *rest, heads, use_sink):
    if use_sink:
        sink_ref, o_ref = rest
    else:
        (o_ref,) = rest
    hv = pl.program_id(1)
    kb = k_ref[...]
    vb = v_ref[...]
    for g in range(heads):
        s = _dot(kb, q_ref[g])
        m = jnp.max(s, axis=0, keepdims=True)
        if use_sink:
            sk = sink_ref[hv * heads + g] * LOG2E
            m = jnp.maximum(m, sk)
        p = jnp.exp2(s - m)
        den = jnp.sum(p, axis=0, keepdims=True)
        if use_sink:
            den = den + jnp.exp2(sk - m)
        o_ref[g] = (_dot(vb, p.astype(BF16)) / den).astype(o_ref.dtype)


def _ctx_attn(q_t, k, v_t, sink, *, heads, name):
    b, hq, _, l = q_t.shape
    hk = k.shape[1]
    hv, dv = v_t.shape[1], v_t.shape[2]
    kidx = (lambda h: h) if hk == hv else (lambda h: 0)
    in_specs = [
        pl.BlockSpec((None, heads, HEAD_PAD, l), lambda i, h: (i, h, 0, 0)),
        pl.BlockSpec((None, None, l, HEAD_PAD), lambda i, h: (i, kidx(h), 0, 0)),
        pl.BlockSpec((None, None, dv, l), lambda i, h: (i, h, 0, 0)),
    ]
    args = [q_t, k, v_t]
    if sink is not None:
        in_specs.append(pl.BlockSpec(memory_space=pltpu.SMEM))
        args.append(sink)
    return pl.pallas_call(
        functools.partial(_ctx_attn_kernel, heads=heads, use_sink=sink is not None),
        grid=(b, hv),
        in_specs=in_specs,
        out_specs=pl.BlockSpec((None, heads, dv, l), lambda i, h: (i, h, 0, 0)),
        out_shape=jax.ShapeDtypeStruct((b, hq, dv, l), BF16),
        compiler_params=_cparams(("parallel", "parallel")),
        name=name,
    )(*args)


def _out_ab_kernel(x_ref, mix_ref, sg_ref, gate_ref, w_ref, o_ref):
    m = mix_ref[...] * sg_ref[...]
    y_t = _dot(w_ref[...], m)
    o_ref[...] = x_ref[...] + gate_ref[...] * y_t.T


def _out_ab(x, mix_t, sg_t, gate, w_out_t, ts):
    b, n, d = x.shape
    return pl.pallas_call(
        _out_ab_kernel,
        grid=(b, n // ts),
        in_specs=[
            pl.BlockSpec((None, ts, d), lambda i, t: (i, t, 0)),
            pl.BlockSpec((None, d, ts), lambda i, t: (i, 0, t)),
            pl.BlockSpec((None, d, ts), lambda i, t: (i, 0, t)),
            pl.BlockSpec((None, 1, d), lambda i, t: (i, 0, 0)),
            pl.BlockSpec((d, d), lambda i, t: (0, 0)),
        ],
        out_specs=pl.BlockSpec((None, ts, d), lambda i, t: (i, t, 0)),
        out_shape=jax.ShapeDtypeStruct((b, n, d), F32),
        compiler_params=_cparams(("parallel", "parallel")),
        name="out_ab",
    )(x, mix_t, sg_t, gate, w_out_t)


def _proj_cd_kernel(x_ref, mod_ref, w_in_ref, qg_ref, kg_ref, c64_ref, s64_ref,
                    q_ref, k_ref, v_ref, gb_ref, u_ref, sg_ref):
    ts = x_ref.shape[0]
    h = _adaln_rows(x_ref[...], mod_ref[0:1, :], mod_ref[1:2, :]).astype(BF16)
    pp = _dot_nt(w_in_ref[...], h)
    c64 = c64_ref[...]
    s64 = s64_ref[...]
    qg = qg_ref[...]
    kg = kg_ref[...]
    scale = WIN_DIM ** -0.5 * LOG2E
    z64 = jnp.zeros((WIN_DIM, ts), BF16)
    grp = WIN_HEADS // WIN_KV_HEADS
    for hd in range(WIN_HEADS):
        qh = _rms_cols(pp[hd * WIN_DIM:(hd + 1) * WIN_DIM], qg, WIN_DIM)
        qh = (_rope_cols(qh, c64, s64, WIN_DIM // 4) * scale).astype(BF16)
        for kvh in range(WIN_KV_HEADS):
            q_ref[hd, kvh * WIN_DIM:(kvh + 1) * WIN_DIM, :] = qh if kvh == hd // grp else z64
    for kvh in range(WIN_KV_HEADS):
        kh = _rms_cols(pp[512 + kvh * WIN_DIM:512 + (kvh + 1) * WIN_DIM], kg, WIN_DIM)
        k_ref[kvh * WIN_DIM:(kvh + 1) * WIN_DIM, :] = _rope_cols(kh, c64, s64, WIN_DIM // 4).astype(BF16)
        v_ref[kvh] = pp[640 + kvh * WIN_DIM:640 + (kvh + 1) * WIN_DIM].astype(BF16)
    gb_ref[...] = pp[768:1280].astype(BF16)
    u_ref[...] = (pp[1280:1792] * pp[1792:2304]).astype(BF16)
    sg_ref[...] = _silu(pp[2304:3328]).astype(BF16)


def _proj_cd(x, mod2, w_in_t, gains, tabs, ts):
    b, n, d = x.shape
    qg, kg = [_bcast_gain(g, ts) for g in gains]
    c64, s64 = tabs
    full = lambda a: pl.BlockSpec(a.shape, lambda i, t: (0,) * a.ndim)
    tab = lambda a: pl.BlockSpec((a.shape[0], ts), lambda i, t: (0, t))
    outs = [
        jax.ShapeDtypeStruct((b, WIN_HEADS, HEAD_PAD, n), BF16),
        jax.ShapeDtypeStruct((b, WIN_KV_HEADS * WIN_DIM, n), BF16),
        jax.ShapeDtypeStruct((b, WIN_KV_HEADS, WIN_DIM, n), BF16),
        jax.ShapeDtypeStruct((b, CONV_CH, n), BF16),
        jax.ShapeDtypeStruct((b, CONV_CH, n), BF16),
        jax.ShapeDtypeStruct((b, D_MODEL, n), BF16),
    ]
    o4 = lambda s: pl.BlockSpec((None,) + s[1:3] + (ts,), lambda i, t: (i, 0, 0, t))
    o3 = lambda s: pl.BlockSpec((None, s[1], ts), lambda i, t: (i, 0, t))
    out_specs = [o4(o.shape) if len(o.shape) == 4 else o3(o.shape) for o in outs]
    return pl.pallas_call(
        _proj_cd_kernel,
        grid=(b, n // ts),
        in_specs=[
            pl.BlockSpec((None, ts, d), lambda i, t: (i, t, 0)),
            pl.BlockSpec((None, 2, d), lambda i, t: (i, 0, 0)),
            full(w_in_t), full(qg), full(kg), tab(c64), tab(s64),
        ],
        out_specs=out_specs,
        out_shape=outs,
        compiler_params=_cparams(("parallel", "parallel")),
        name="proj_cd",
    )(x, mod2, w_in_t, qg, kg, c64, s64)


def _win_kernel(q_ref, kp_ref, kc_ref, kn_ref, vp_ref, vcur_ref, vn_ref, kctx_ref, vctx_ref, sink_ref,
                o_ref, *, heads, tq):
    hv = pl.program_id(1)
    qi = pl.program_id(2)
    nq = pl.num_programs(2)
    row_c = lax.broadcasted_iota(jnp.int32, (tq, tq), 0)
    col_c = lax.broadcasted_iota(jnp.int32, (tq, tq), 1)
    mask_c = jnp.abs(col_c - row_c) <= WINDOW
    row_h = lax.broadcasted_iota(jnp.int32, (WINDOW, tq), 0)
    col_h = lax.broadcasted_iota(jnp.int32, (WINDOW, tq), 1)
    mask_p = (col_h <= row_h) & (qi > 0)
    mask_n = (col_h >= row_h + (tq - WINDOW)) & (qi < nq - 1)
    kctx = kctx_ref[...]
    kp = kp_ref[...]
    kc = kc_ref[...]
    kn = kn_ref[...]
    for g in range(heads):
        q = q_ref[g]
        s_x = _dot(kctx, q)
        s_c = jnp.where(mask_c, _dot(kc, q), NEG_INF)
        s_p = jnp.where(mask_p, _dot(kp, q), NEG_INF)
        s_n = jnp.where(mask_n, _dot(kn, q), NEG_INF)
        sk = sink_ref[hv * heads + g] * LOG2E
        m = jnp.maximum(jnp.maximum(jnp.max(s_x, axis=0, keepdims=True), jnp.max(s_c, axis=0, keepdims=True)),
                        jnp.maximum(jnp.max(s_p, axis=0, keepdims=True), jnp.max(s_n, axis=0, keepdims=True)))
        m = jnp.maximum(m, sk)
        p_x = jnp.exp2(s_x - m)
        p_c = jnp.exp2(s_c - m)
        p_p = jnp.exp2(s_p - m)
        p_n = jnp.exp2(s_n - m)
        den = (jnp.sum(p_x, axis=0, keepdims=True) + jnp.sum(p_c, axis=0, keepdims=True)
               + jnp.sum(p_p, axis=0, keepdims=True) + jnp.sum(p_n, axis=0, keepdims=True)
               + jnp.exp2(sk - m))
        acc = (_dot(vctx_ref[...], p_x.astype(BF16)) + _dot(vcur_ref[...], p_c.astype(BF16))
               + _dot(vp_ref[...], p_p.astype(BF16)) + _dot(vn_ref[...], p_n.astype(BF16)))
        o_ref[g] = (acc / den).astype(o_ref.dtype)


def _win_attn(q_t, k, v_t, kc, vc_t, sink, *, tq):
    b, hq, _, s = q_t.shape
    hv, dv = v_t.shape[1], v_t.shape[2]
    heads = hq // hv
    lc = kc.shape[2]
    r = tq // WINDOW
    nb = s // WINDOW
    prev = lambda qi: jnp.maximum(qi * r - 1, 0)
    nxt = lambda qi: jnp.minimum((qi + 1) * r, nb - 1)
    return pl.pallas_call(
        functools.partial(_win_kernel, heads=heads, tq=tq),
        grid=(b, hv, s // tq),
        in_specs=[
            pl.BlockSpec((None, heads, HEAD_PAD, tq), lambda i, h, qi: (i, h, 0, qi)),
            pl.BlockSpec((None, None, WINDOW, HEAD_PAD), lambda i, h, qi: (i, 0, prev(qi), 0)),
            pl.BlockSpec((None, None, tq, HEAD_PAD), lambda i, h, qi: (i, 0, qi, 0)),
            pl.BlockSpec((None, None, WINDOW, HEAD_PAD), lambda i, h, qi: (i, 0, nxt(qi), 0)),
            pl.BlockSpec((None, None, dv, WINDOW), lambda i, h, qi: (i, h, 0, prev(qi))),
            pl.BlockSpec((None, None, dv, tq), lambda i, h, qi: (i, h, 0, qi)),
            pl.BlockSpec((None, None, dv, WINDOW), lambda i, h, qi: (i, h, 0, nxt(qi))),
            pl.BlockSpec((None, None, lc, HEAD_PAD), lambda i, h, qi: (i, 0, 0, 0)),
            pl.BlockSpec((None, None, dv, lc), lambda i, h, qi: (i, h, 0, 0)),
            pl.BlockSpec(memory_space=pltpu.SMEM),
        ],
        out_specs=pl.BlockSpec((None, heads, dv, tq), lambda i, h, qi: (i, h, 0, qi)),
        out_shape=jax.ShapeDtypeStruct((b, hq, dv, s), BF16),
        compiler_params=_cparams(("parallel", "parallel", "parallel")),
        name="win_attn",
    )(q_t, k, k, k, v_t, v_t, v_t, kc, vc_t, sink)


def _out_cd_kernel(x_ref, oc_ref, gb_ref, up_ref, u_ref, un_ref, cw_ref, sg_ref, gate_ref, w_ref, o_ref, *, ts):
    t = pl.program_id(1)
    nt = pl.num_programs(1)
    u = u_ref[...].astype(F32)
    lane = lax.broadcasted_iota(jnp.int32, (CONV_CH, ts), 1)
    halo_l = pltpu.roll(up_ref[...].astype(F32), 1, axis=1)
    halo_l = jnp.where(t > 0, halo_l, 0.0)
    halo_r = pltpu.roll(un_ref[...].astype(F32), 127, axis=1)
    halo_r = jnp.where(t < nt - 1, halo_r, 0.0)
    reps = ts // 128
    um = jnp.where(lane == 0, jnp.tile(halo_l, (1, reps)), pltpu.roll(u, 1, axis=1))
    up = jnp.where(lane == ts - 1, jnp.tile(halo_r, (1, reps)), pltpu.roll(u, ts - 1, axis=1))
    w0 = jnp.tile(cw_ref[0], (1, reps))
    w1 = jnp.tile(cw_ref[1], (1, reps))
    w2 = jnp.tile(cw_ref[2], (1, reps))
    od = gb_ref[...].astype(F32) * (w0 * um + w1 * u + w2 * up)
    sg = sg_ref[...]
    half = D_MODEL - CONV_CH
    m = jnp.concatenate([oc_ref[...] * sg[0:half], od.astype(BF16) * sg[half:]], axis=0)
    y_t = _dot(w_ref[...], m)
    o_ref[...] = x_ref[...] + gate_ref[...] * y_t.T


def _out_cd(x, oc_t, gb_t, u_t, conv_wb, sg_t, gate, w_out_t, ts):
    b, n, d = x.shape
    r = ts // 128
    nb = n // 128
    return pl.pallas_call(
        functools.partial(_out_cd_kernel, ts=ts),
        grid=(b, n // ts),
        in_specs=[
            pl.BlockSpec((None, ts, d), lambda i, t: (i, t, 0)),
            pl.BlockSpec((None, d - CONV_CH, ts), lambda i, t: (i, 0, t)),
            pl.BlockSpec((None, CONV_CH, ts), lambda i, t: (i, 0, t)),
            pl.BlockSpec((None, CONV_CH, 128), lambda i, t: (i, 0, jnp.maximum(t * r - 1, 0))),
            pl.BlockSpec((None, CONV_CH, ts), lambda i, t: (i, 0, t)),
            pl.BlockSpec((None, CONV_CH, 128), lambda i, t: (i, 0, jnp.minimum((t + 1) * r, nb - 1))),
            pl.BlockSpec((CONV_K, CONV_CH, 128), lambda i, t: (0, 0, 0)),
            pl.BlockSpec((None, d, ts), lambda i, t: (i, 0, t)),
            pl.BlockSpec((None, 1, d), lambda i, t: (i, 0, 0)),
            pl.BlockSpec((d, d), lambda i, t: (0, 0)),
        ],
        out_specs=pl.BlockSpec((None, ts, d), lambda i, t: (i, t, 0)),
        out_shape=jax.ShapeDtypeStruct((b, n, d), F32),
        compiler_params=_cparams(("parallel", "parallel")),
        name="out_cd",
    )(x, oc_t, gb_t, u_t, u_t, u_t, conv_wb, sg_t, gate, w_out_t)


def _rope_tables(n, dd, identity=False):
    q = dd // 4
    if identity:
        return jnp.ones((dd, n), F32), jnp.zeros((dd, n), F32)
    rows = n // GRID_W
    row = jnp.repeat(jnp.arange(rows, dtype=F32), GRID_W)
    col = jnp.tile(jnp.arange(GRID_W, dtype=F32), rows)
    inv = ROPE_BASE ** (-jnp.arange(q, dtype=F32) / q)
    ar = inv[:, None] * row[None, :]
    ac = inv[:, None] * col[None, :]
    cos = jnp.concatenate([jnp.cos(ar), jnp.cos(ar), jnp.cos(ac), jnp.cos(ac)], axis=0)
    sin = jnp.concatenate([-jnp.sin(ar), jnp.sin(ar), -jnp.sin(ac), jnp.sin(ac)], axis=0)
    return cos, sin


def _tile_rows(n, cap):
    t = cap
    while n % t:
        t //= 2
    return t


def _key_tile(n, cap, kc):
    t = (cap // kc) * kc
    while n % t:
        t -= kc
    return t


def kernel(x, c, ctx, c_ctx, mod_w, mod_b, ab_w_in, ab_w_out, mla_cq_gain, mla_ckv_gain, mla_w_uq, mla_w_ukv,
           mla_q_gain, mla_k_gain, gqa_q_gain, gqa_k_gain, cd_w_in, cd_w_out, win_q_gain, win_k_gain,
           win_sink, conv_w):
    b, s, d = x.shape
    lc = ctx.shape[1]
    ts_x = _tile_rows(s, 512)
    ts_c = _tile_rows(lc, 256)
    tq = _tile_rows(s, 512)
    tk = _key_tile(s + lc, 1280, 128)
    kc = 256 if tk % 256 == 0 else tk
    ahead = 2
    tq_w = _tile_rows(s, 256)

    crows = jnp.zeros((8, d), F32).at[0:b].set(c).at[b].set(c_ctx)
    mods = _modulation(crows, mod_w, mod_b)

    def split_mod(layer):
        m = mods[layer]
        shift, scale, gate = m[:, 0:d], m[:, d:2 * d], m[:, 2 * d:3 * d]
        mod_x = jnp.stack([shift[0:b], scale[0:b]], axis=1)
        mod_c = jnp.broadcast_to(jnp.stack([shift[b], scale[b]], axis=0)[None], (b, 2, d))
        gate_x = gate[0:b][:, None, :]
        gate_c = jnp.broadcast_to(gate[b][None, None, :], (b, 1, d))
        return mod_x, mod_c, gate_x, gate_c

    tabs_x = _rope_tables(s, MLA_ROPE) + _rope_tables(s, GQA_DIM)
    tabs_c = _rope_tables(lc, MLA_ROPE, True) + _rope_tables(lc, GQA_DIM, True)

    mod_x, mod_c, gate_x, gate_c = split_mod(0)
    w_in_t = ab_w_in[0].T.astype(BF16)
    w_uq_t = mla_w_uq[0].T.astype(BF16)
    w_ukv_t = mla_w_ukv[0].T.astype(BF16)
    w_out_t = ab_w_out[0].T.astype(BF16)
    gains = (mla_cq_gain[0], mla_ckv_gain[0], mla_q_gain[0], mla_k_gain[0], gqa_q_gain[0], gqa_k_gain[0])
    qa, ka, va, qb, kb, vb, sg = _proj_ab(x, mod_x, w_in_t, w_uq_t, w_ukv_t, gains, tabs_x, ts_x)
    qac, kac, vac, qbc, kbc, vbc, sgc = _proj_ab(ctx, mod_c, w_in_t, w_uq_t, w_ukv_t, gains, tabs_c, ts_c)
    ka_n = jnp.swapaxes(ka, 2, 3)
    kac_n = jnp.swapaxes(kac, 2, 3)
    kb_n = jnp.swapaxes(kb, 1, 2)[:, None]
    kbc_n = jnp.swapaxes(kbc, 1, 2)[:, None]
    cat = lambda a, c, ax: jnp.concatenate([a, c], axis=ax)
    oa = _stream_attn(qa, cat(kac_n, ka_n, 2), cat(vac, va, 3), tq=tq, tk=tk, kc=kc, ahead=ahead, name="flash_mla")
    ob = _stream_attn(qb, cat(kbc_n, kb_n, 2), cat(vbc, vb, 3), tq=tq, tk=tk, kc=kc, ahead=ahead, name="flash_gqa")
    mix_t = jnp.concatenate([oa.reshape(b, -1, s), ob.reshape(b, -1, s)], axis=1)
    x1 = _out_ab(x, mix_t, sg, gate_x, w_out_t, ts_x)
    oac = _ctx_attn(qac, kac_n, vac, None, heads=1, name="ctx_mla")
    obc = _ctx_attn(qbc, kbc_n, vbc, None, heads=GQA_HEADS // GQA_KV_HEADS, name="ctx_gqa")
    mixc_t = jnp.concatenate([oac.reshape(b, -1, lc), obc.reshape(b, -1, lc)], axis=1)
    xc1 = _out_ab(ctx, mixc_t, sgc, gate_c, w_out_t, ts_c)

    mod_x, mod_c, gate_x, gate_c = split_mod(1)
    w_in_t = cd_w_in[0].T.astype(BF16)
    w_out_t = cd_w_out[0].T.astype(BF16)
    gains = (win_q_gain[0], win_k_gain[0])
    q, k, v, gb, u, sg = _proj_cd(x1, mod_x, w_in_t, gains, tabs_x[2:], ts_x)
    _, kc, vc, _, _, _ = _proj_cd(xc1, mod_c, w_in_t, gains, tabs_c[2:], ts_c)
    k_n = jnp.swapaxes(k, 1, 2)[:, None]
    kc_n = jnp.swapaxes(kc, 1, 2)[:, None]
    oc = _win_attn(q, k_n, v, kc_n, vc, win_sink[0].astype(F32), tq=tq_w)
    conv_wb = jnp.broadcast_to(conv_w[0].astype(F32)[:, :, None], (CONV_K, CONV_CH, 128))
    return _out_cd(x1, oc.reshape(b, -1, s), gb, u, conv_wb, sg, gate_x, w_out_t, ts_x)
```

```python
import functools
import math

import jax
import jax.numpy as jnp
from jax import lax
from jax.experimental import pallas as pl
from jax.experimental.pallas import tpu as pltpu

D_MODEL = 1024
GRID_W = 64
ROPE_BASE = 10000.0
EPS = 1e-6
NEG_INF = -1e30
WINDOW = 128

MLA_HEADS = 8
MLA_NOPE = 64
MLA_ROPE = 32
MLA_V = 64
MLA_QK = MLA_NOPE + MLA_ROPE
MLA_Q_RANK = 256
MLA_KV_RANK = 256
GQA_HEADS = 8
GQA_KV_HEADS = 2
GQA_DIM = 64
WIN_HEADS = 8
WIN_KV_HEADS = 2
WIN_DIM = 64
CONV_CH = 512
CONV_K = 3

LANES = 128
HEAD_PAD = 128
STAT_ROWS = 8
LOG2E = math.log2(math.e)
VMEM_LIMIT_BYTES = 56 * 1024 * 1024

BF16 = jnp.bfloat16
F32 = jnp.float32


def _cparams(sem):
    return pltpu.CompilerParams(dimension_semantics=sem, vmem_limit_bytes=VMEM_LIMIT_BYTES)


def _dot(a, b):
    return jnp.dot(a, b, preferred_element_type=F32)


def _dot_nt(a, b):
    return lax.dot_general(a, b, (((1,), (1,)), ((), ())), preferred_element_type=F32)


def _silu(v):
    return v * (1.0 / (1.0 + jnp.exp(-v)))


def _mod_kernel(c_ref, w_ref, b_ref, o_ref):
    sc = _silu(c_ref[...])
    o_ref[...] = _dot(sc, w_ref[...]) + b_ref[...]


def _modulation(crows, mod_w, mod_b):
    depth, d, n = mod_w.shape
    tn = 1024
    return pl.pallas_call(
        _mod_kernel,
        grid=(depth, n // tn),
        in_specs=[
            pl.BlockSpec((8, d), lambda l, j: (0, 0)),
            pl.BlockSpec((None, d, tn), lambda l, j: (l, 0, j)),
            pl.BlockSpec((None, 1, tn), lambda l, j: (l, 0, j)),
        ],
        out_specs=pl.BlockSpec((None, 8, tn), lambda l, j: (l, 0, j)),
        out_shape=jax.ShapeDtypeStruct((depth, 8, n), F32),
        compiler_params=_cparams(("parallel", "parallel")),
        name="modulation",
    )(crows, mod_w, mod_b.reshape(depth, 1, n))


def _adaln_rows(x, shift, scale):
    ms = jnp.mean(x * x, axis=-1, keepdims=True)
    return (x * lax.rsqrt(ms + EPS)) * (1.0 + scale) + shift


def _rms_cols(v, gain_b, n):
    ss = jnp.sum(v * v, axis=0, keepdims=True)
    return v * lax.rsqrt(ss * (1.0 / n) + EPS) * gain_b


def _rope_cols(v, cos, sin, q):
    swap = jnp.concatenate([v[q:2 * q], v[0:q], v[3 * q:4 * q], v[2 * q:3 * q]], axis=0)
    return v * cos + swap * sin


def _joint_rows(x_ref, c_ref, mod_ref, nx):
    rows = jnp.where(pl.program_id(1) < nx, x_ref[...], c_ref[...])
    return _adaln_rows(rows, mod_ref[0:1, :], mod_ref[1:2, :]).astype(BF16)


def _joint_specs(nx, nc_t, ts, d):
    return [
        pl.BlockSpec((None, ts, d), lambda i, t: (i, jnp.minimum(t, nx - 1), 0)),
        pl.BlockSpec((None, ts, d), lambda i, t: (i, jnp.clip(t - nx, 0, nc_t - 1), 0)),
        pl.BlockSpec((None, None, 2, d), lambda i, t: (i, jnp.where(t < nx, 0, 1), 0, 0)),
    ]


def _proj_ab_kernel(x_ref, c_ref, mod_ref, w_in_ref, cqg_ref, ckvg_ref, w_uq_ref, w_ukv_ref,
                    qg_ref, kg_ref, gqg_ref, gkg_ref, c32_ref, s32_ref, c64_ref, s64_ref,
                    qa_ref, ka_ref, va_ref, qb_ref, kb_ref, vb_ref, sg_ref, *, nx):
    ts = x_ref.shape[0]
    h = _joint_rows(x_ref, c_ref, mod_ref, nx)
    pp = _dot_nt(w_in_ref[...], h)

    cq = pp[0:256]
    ckv = pp[256:512]
    kr = pp[512:544]
    cqn = _rms_cols(cq, cqg_ref[...], MLA_Q_RANK).astype(BF16)
    ckvn = _rms_cols(ckv, ckvg_ref[...], MLA_KV_RANK).astype(BF16)
    qa = _dot(w_uq_ref[...], cqn)
    kv = _dot(w_ukv_ref[...], ckvn)
    qg = qg_ref[...]
    kg = kg_ref[...]
    c32 = c32_ref[...]
    s32 = s32_ref[...]
    kr_ss = jnp.sum(kr * kr, axis=0, keepdims=True)
    zpad = jnp.zeros((HEAD_PAD - MLA_QK, ts), F32)
    q_scale = MLA_QK ** -0.5 * LOG2E
    for hd in range(MLA_HEADS):
        qh = _rms_cols(qa[hd * MLA_QK:(hd + 1) * MLA_QK], qg, MLA_QK)
        qr = _rope_cols(qh[MLA_NOPE:], c32, s32, MLA_ROPE // 4)
        qa_ref[hd, 0:MLA_NOPE, :] = (qh[0:MLA_NOPE] * q_scale).astype(BF16)
        qa_ref[hd, MLA_NOPE:MLA_QK, :] = (qr * q_scale).astype(BF16)
        qa_ref[hd, MLA_QK:HEAD_PAD, :] = zpad.astype(BF16)
        kn = kv[hd * 128:hd * 128 + MLA_NOPE]
        ss = jnp.sum(kn * kn, axis=0, keepdims=True) + kr_ss
        rs = lax.rsqrt(ss * (1.0 / MLA_QK) + EPS)
        krn = _rope_cols(kr * rs * kg[MLA_NOPE:], c32, s32, MLA_ROPE // 4)
        k_cols = jnp.concatenate([kn * rs * kg[0:MLA_NOPE], krn, zpad], axis=0)
        ka_ref[hd] = k_cols.T.astype(BF16)
        va_ref[hd] = kv[hd * 128 + MLA_NOPE:(hd + 1) * 128].astype(BF16)

    gq = pp[544:1056]
    gk = pp[1056:1184]
    gv = pp[1184:1312]
    c64 = c64_ref[...]
    s64 = s64_ref[...]
    gqg = gqg_ref[...]
    gkg = gkg_ref[...]
    g_scale = GQA_DIM ** -0.5 * LOG2E
    z64 = jnp.zeros((GQA_DIM, ts), BF16)
    grp = GQA_HEADS // GQA_KV_HEADS
    for hd in range(GQA_HEADS):
        qh = _rms_cols(gq[hd * GQA_DIM:(hd + 1) * GQA_DIM], gqg, GQA_DIM)
        qh = (_rope_cols(qh, c64, s64, GQA_DIM // 4) * g_scale).astype(BF16)
        for kvh in range(GQA_KV_HEADS):
            qb_ref[hd, kvh * GQA_DIM:(kvh + 1) * GQA_DIM, :] = qh if kvh == hd // grp else z64
    k_heads = []
    for kvh in range(GQA_KV_HEADS):
        kh = _rms_cols(gk[kvh * GQA_DIM:(kvh + 1) * GQA_DIM], gkg, GQA_DIM)
        k_heads.append(_rope_cols(kh, c64, s64, GQA_DIM // 4))
        vb_ref[kvh] = gv[kvh * GQA_DIM:(kvh + 1) * GQA_DIM].astype(BF16)
    kb_ref[...] = jnp.concatenate(k_heads, axis=0).T.astype(BF16)
    sg_ref[...] = _silu(pp[1312:2336]).astype(BF16)


def _bcast_gain(g, ts):
    return jnp.broadcast_to(g.astype(F32)[:, None], (g.shape[0], ts))


def _proj_ab(x, ctx, mods, w_in_t, w_uq_t, w_ukv_t, gains, tabs, ts):
    b, s, d = x.shape
    lc = ctx.shape[1]
    n = s + lc
    nx, nc_t = s // ts, lc // ts
    cqg, ckvg, qg, kg, gqg, gkg = [_bcast_gain(g, ts) for g in gains]
    c32, s32, c64, s64 = tabs
    full = lambda a: pl.BlockSpec(a.shape, lambda i, t: (0,) * a.ndim)
    tab = lambda a: pl.BlockSpec((a.shape[0], ts), lambda i, t: (0, t))
    feat4 = lambda h, r: (jax.ShapeDtypeStruct((b, h, r, n), BF16),
                          pl.BlockSpec((None, h, r, ts), lambda i, t: (i, 0, 0, t)))
    rows4 = lambda h: (jax.ShapeDtypeStruct((b, h, n, HEAD_PAD), BF16),
                       pl.BlockSpec((None, h, ts, HEAD_PAD), lambda i, t: (i, 0, t, 0)))
    outs = [
        feat4(MLA_HEADS, HEAD_PAD),
        rows4(MLA_HEADS),
        feat4(MLA_HEADS, MLA_V),
        feat4(GQA_HEADS, HEAD_PAD),
        (jax.ShapeDtypeStruct((b, n, HEAD_PAD), BF16),
         pl.BlockSpec((None, ts, HEAD_PAD), lambda i, t: (i, t, 0))),
        feat4(GQA_KV_HEADS, GQA_DIM),
        (jax.ShapeDtypeStruct((b, D_MODEL, n), BF16),
         pl.BlockSpec((None, D_MODEL, ts), lambda i, t: (i, 0, t))),
    ]
    return pl.pallas_call(
        functools.partial(_proj_ab_kernel, nx=nx),
        grid=(b, nx + nc_t),
        in_specs=_joint_specs(nx, nc_t, ts, d) + [
            full(w_in_t), full(cqg), full(ckvg), full(w_uq_t), full(w_ukv_t),
            full(qg), full(kg), full(gqg), full(gkg),
            tab(c32), tab(s32), tab(c64), tab(s64),
        ],
        out_specs=[o[1] for o in outs],
        out_shape=[o[0] for o in outs],
        compiler_params=_cparams(("parallel", "parallel")),
        name="proj_ab",
    )(x, ctx, mods, w_in_t, cqg, ckvg, w_uq_t, w_ukv_t, qg, kg, gqg, gkg, c32, s32, c64, s64)


P_LIMIT = 2.0 ** 100
FIRST_REF_KEYS = 128


def _stream_kernel(q_ref, k_ref, v_ref, o_ref, m_sc, acc_sc, pv_sc, *, heads, kc, ahead):
    j = pl.program_id(2)
    tk = k_ref.shape[1]
    tq = q_ref.shape[2]
    dv = o_ref.shape[1]
    nc = tk // kc
    gk = k_ref.shape[0]
    gv = v_ref.shape[0]
    cur = lax.rem(j, 2)
    nxt = 1 - cur

    @pl.when(j == 0)
    def _():
        for g in range(heads):
            s0 = _dot(k_ref[g * gk // heads, 0:FIRST_REF_KEYS, :], q_ref[g])
            m_sc[0, g] = jnp.max(s0, axis=0, keepdims=True)
        acc_sc[0] = jnp.zeros(acc_sc.shape[1:], F32)

    units = [(g, c) for g in range(heads) for c in range(nc)]
    pending = {}

    def emit_scores(u):
        g, c = u
        pending[u] = _dot(k_ref[g * gk // heads, c * kc:(c + 1) * kc, :], q_ref[g])

    worst = None
    issued = 0
    pmx = None
    lsum = None
    for i, (g, c) in enumerate(units):
        while issued <= min(i + ahead, len(units) - 1):
            emit_scores(units[issued])
            issued += 1
        s = pending.pop((g, c))
        m_ref = m_sc[cur, g]
        mb = jnp.broadcast_to(m_ref, (8, tq))
        p32 = jnp.exp2(s.reshape(kc // 8, 8, tq) - mb[None])
        lpart = jnp.sum(p32, axis=0)
        lsum = lpart if c == 0 else lsum + lpart
        p = p32.reshape(kc, tq).astype(BF16)
        part = jnp.max(p.reshape(kc // 16, 16, tq), axis=0)
        pmx = part if c == 0 else jnp.maximum(pmx, part)
        d = _dot(v_ref[g * gv // heads, :, c * kc:(c + 1) * kc], p)
        if c == 0:
            pv_sc[g] = d
        else:
            pv_sc[g] += d
        if c == nc - 1:
            pm = jnp.maximum(jnp.max(pmx.astype(F32), axis=0, keepdims=True), 1.0)
            worst = pm if worst is None else jnp.maximum(worst, pm)
            m_sc[nxt, g] = m_ref + jnp.log2(pm)
            rp = 1.0 / pm
            acc_sc[nxt, g, 0:dv, :] = (acc_sc[cur, g, 0:dv, :] + pv_sc[g]) * rp
            acc_sc[nxt, g, dv:dv + 1, :] = (acc_sc[cur, g, dv:dv + 1, :] + jnp.sum(lsum, axis=0, keepdims=True)) * rp

    redo = jnp.logical_not(jnp.max(worst) <= P_LIMIT)

    @pl.when(redo)
    def _():
        def head(g, carry):
            q = q_ref[g]
            gki = g * gk // heads
            gvi = g * gv // heads
            m_old = m_sc[cur, g]

            def cmax(c, mx):
                s = _dot(k_ref[gki, pl.ds(pl.multiple_of(c * kc, kc), kc), :], q)
                return jnp.maximum(mx, jnp.max(s, axis=0, keepdims=True))

            m_new = lax.fori_loop(0, nc, cmax, m_old)

            def cacc(c, st):
                pv, l = st
                ks = pl.ds(pl.multiple_of(c * kc, kc), kc)
                p = jnp.exp2(_dot(k_ref[gki, ks, :], q) - m_new)
                return pv + _dot(v_ref[gvi, :, ks], p.astype(BF16)), l + jnp.sum(p, axis=0, keepdims=True)

            pv, l = lax.fori_loop(0, nc, cacc, (jnp.zeros((dv, tq), F32), jnp.zeros((1, tq), F32)))
            alpha = jnp.exp2(m_old - m_new)
            m_sc[nxt, g] = m_new
            acc_sc[nxt, g, 0:dv, :] = acc_sc[cur, g, 0:dv, :] * alpha + pv
            acc_sc[nxt, g, dv:dv + 1, :] = acc_sc[cur, g, dv:dv + 1, :] * alpha + l
            return carry

        lax.fori_loop(0, heads, head, 0)

    @pl.when(j == pl.num_programs(2) - 1)
    def _():
        for g in range(heads):
            o_ref[g] = (acc_sc[nxt, g, 0:dv, :] / acc_sc[nxt, g, dv:dv + 1, :]).astype(o_ref.dtype)


def _stream_attn(q_t, k, v_t, s, *, tq, tk, kc, ahead, name):
    b, hq = q_t.shape[0], q_t.shape[1]
    hk, n = k.shape[1], k.shape[2]
    hv, dv = v_t.shape[1], v_t.shape[2]
    return pl.pallas_call(
        functools.partial(_stream_kernel, heads=hq, kc=kc, ahead=ahead),
        grid=(b, s // tq, n // tk),
        in_specs=[
            pl.BlockSpec((None, hq, HEAD_PAD, tq), lambda i, qi, j: (i, 0, 0, qi)),
            pl.BlockSpec((None, hk, tk, HEAD_PAD), lambda i, qi, j: (i, 0, j, 0)),
            pl.BlockSpec((None, hv, dv, tk), lambda i, qi, j: (i, 0, 0, j)),
        ],
        out_specs=pl.BlockSpec((None, hq, dv, tq), lambda i, qi, j: (i, 0, 0, qi)),
        out_shape=jax.ShapeDtypeStruct((b, hq, dv, s), BF16),
        scratch_shapes=[
            pltpu.VMEM((2, hq, 1, tq), F32),
            pltpu.VMEM((2, hq, dv + STAT_ROWS, tq), F32),
            pltpu.VMEM((hq, dv, tq), F32),
        ],
        compiler_params=_cparams(("parallel", "parallel", "arbitrary")),
        name=name,
    )(q_t, k, v_t)


def _ctx_attn_kernel(q_ref, k_ref, v_ref, o_ref):
    heads, gk, gv = q_ref.shape[0], k_ref.shape[0], v_ref.shape[0]
    for g in range(heads):
        s = _dot(k_ref[g * gk // heads], q_ref[g])
        p = jnp.exp2(s - jnp.max(s, axis=0, keepdims=True))
        den = jnp.sum(p, axis=0, keepdims=True)
        o_ref[g] = (_dot(v_ref[g * gv // heads], p.astype(BF16)) / den).astype(o_ref.dtype)


def _ctx_attn(q_t, k, v_t, s, lc, name):
    b, hq = q_t.shape[0], q_t.shape[1]
    hk = k.shape[1]
    hv, dv = v_t.shape[1], v_t.shape[2]
    off = s // lc
    return pl.pallas_call(
        _ctx_attn_kernel,
        grid=(b,),
        in_specs=[
            pl.BlockSpec((None, hq, HEAD_PAD, lc), lambda i: (i, 0, 0, off)),
            pl.BlockSpec((None, hk, lc, HEAD_PAD), lambda i: (i, 0, off, 0)),
            pl.BlockSpec((None, hv, dv, lc), lambda i: (i, 0, 0, off)),
        ],
        out_specs=pl.BlockSpec((None, hq, dv, lc), lambda i: (i, 0, 0, 0)),
        out_shape=jax.ShapeDtypeStruct((b, hq, dv, lc), BF16),
        compiler_params=_cparams(("parallel",)),
        name=name,
    )(q_t, k, v_t)


def _out_ab_kernel(x_ref, oa_ref, ob_ref, sg_ref, gate_ref, w_ref, o_ref):
    m = jnp.concatenate([oa_ref[...], ob_ref[...]], axis=0) * sg_ref[...]
    y_t = _dot(w_ref[...], m)
    o_ref[...] = x_ref[...] + gate_ref[...] * y_t.T


def _out_ab(x, oa_t, ob_t, sg_t, sg_off, gate, w_out_t, ts):
    b, n, d = x.shape
    half = oa_t.shape[1]
    return pl.pallas_call(
        _out_ab_kernel,
        grid=(b, n // ts),
        in_specs=[
            pl.BlockSpec((None, ts, d), lambda i, t: (i, t, 0)),
            pl.BlockSpec((None, half, ts), lambda i, t: (i, 0, t)),
            pl.BlockSpec((None, d - half, ts), lambda i, t: (i, 0, t)),
            pl.BlockSpec((None, d, ts), lambda i, t: (i, 0, t + sg_off)),
            pl.BlockSpec((None, 1, d), lambda i, t: (i, 0, 0)),
            pl.BlockSpec((d, d), lambda i, t: (0, 0)),
        ],
        out_specs=pl.BlockSpec((None, ts, d), lambda i, t: (i, t, 0)),
        out_shape=jax.ShapeDtypeStruct((b, n, d), F32),
        compiler_params=_cparams(("parallel", "parallel")),
        name="out_ab",
    )(x, oa_t, ob_t, sg_t, gate, w_out_t)


def _proj_cd_kernel(x_ref, c_ref, mod_ref, w_in_ref, qg_ref, kg_ref, c64_ref, s64_ref,
                    q_ref, k_ref, v_ref, gb_ref, u_ref, sg_ref, *, nx):
    ts = x_ref.shape[0]
    h = _joint_rows(x_ref, c_ref, mod_ref, nx)
    pp = _dot_nt(w_in_ref[...], h)
    c64 = c64_ref[...]
    s64 = s64_ref[...]
    qg = qg_ref[...]
    kg = kg_ref[...]
    scale = WIN_DIM ** -0.5 * LOG2E
    z64 = jnp.zeros((WIN_DIM, ts), BF16)
    grp = WIN_HEADS // WIN_KV_HEADS
    for hd in range(WIN_HEADS):
        qh = _rms_cols(pp[hd * WIN_DIM:(hd + 1) * WIN_DIM], qg, WIN_DIM)
        qh = (_rope_cols(qh, c64, s64, WIN_DIM // 4) * scale).astype(BF16)
        for kvh in range(WIN_KV_HEADS):
            q_ref[hd, kvh * WIN_DIM:(kvh + 1) * WIN_DIM, :] = qh if kvh == hd // grp else z64
    k_heads = []
    for kvh in range(WIN_KV_HEADS):
        kh = _rms_cols(pp[512 + kvh * WIN_DIM:512 + (kvh + 1) * WIN_DIM], kg, WIN_DIM)
        k_heads.append(_rope_cols(kh, c64, s64, WIN_DIM // 4))
        v_ref[kvh] = pp[640 + kvh * WIN_DIM:640 + (kvh + 1) * WIN_DIM].astype(BF16)
    k_ref[...] = jnp.concatenate(k_heads, axis=0).T.astype(BF16)
    gb_ref[...] = pp[768:1280].astype(BF16)
    u_ref[...] = (pp[1280:1792] * pp[1792:2304]).astype(BF16)
    sg_ref[...] = _silu(pp[2304:3328]).astype(BF16)


def _proj_cd(x, ctx, mods, w_in_t, gains, tabs, ts):
    b, s, d = x.shape
    lc = ctx.shape[1]
    n = s + lc
    nx, nc_t = s // ts, lc // ts
    qg, kg = [_bcast_gain(g, ts) for g in gains]
    c64, s64 = tabs
    full = lambda a: pl.BlockSpec(a.shape, lambda i, t: (0,) * a.ndim)
    tab = lambda a: pl.BlockSpec((a.shape[0], ts), lambda i, t: (0, t))
    feat3 = lambda r: (jax.ShapeDtypeStruct((b, r, n), BF16), pl.BlockSpec((None, r, ts), lambda i, t: (i, 0, t)))
    outs = [
        (jax.ShapeDtypeStruct((b, WIN_HEADS, HEAD_PAD, n), BF16),
         pl.BlockSpec((None, WIN_HEADS, HEAD_PAD, ts), lambda i, t: (i, 0, 0, t))),
        (jax.ShapeDtypeStruct((b, n, HEAD_PAD), BF16),
         pl.BlockSpec((None, ts, HEAD_PAD), lambda i, t: (i, t, 0))),
        (jax.ShapeDtypeStruct((b, WIN_KV_HEADS, WIN_DIM, n), BF16),
         pl.BlockSpec((None, WIN_KV_HEADS, WIN_DIM, ts), lambda i, t: (i, 0, 0, t))),
        feat3(CONV_CH),
        feat3(CONV_CH),
        feat3(D_MODEL),
    ]
    return pl.pallas_call(
        functools.partial(_proj_cd_kernel, nx=nx),
        grid=(b, nx + nc_t),
        in_specs=_joint_specs(nx, nc_t, ts, d) + [full(w_in_t), full(qg), full(kg), tab(c64), tab(s64)],
        out_specs=[o[1] for o in outs],
        out_shape=[o[0] for o in outs],
        compiler_params=_cparams(("parallel", "parallel")),
        name="proj_cd",
    )(x, ctx, mods, w_in_t, qg, kg, c64, s64)


def _win_kernel(q_ref, kp_ref, kc_ref, kn_ref, vp_ref, vcur_ref, vn_ref, kctx_ref, vctx_ref, bias_ref, sink_ref,
                o_ref, *, heads, tq):
    hv = pl.program_id(1)
    qi = pl.program_id(2)
    edge_p = jnp.where(qi > 0, 0.0, NEG_INF)
    edge_n = jnp.where(qi < pl.num_programs(2) - 1, 0.0, NEG_INF)
    bias = jnp.concatenate([bias_ref[0:WINDOW, :] + edge_p, bias_ref[WINDOW:WINDOW + tq, :],
                            bias_ref[WINDOW + tq:, :] + edge_n], axis=0)
    k_loc = jnp.concatenate([kp_ref[...], kc_ref[...], kn_ref[...]], axis=0)
    v_loc = jnp.concatenate([vp_ref[...], vcur_ref[...], vn_ref[...]], axis=1)
    kctx = kctx_ref[...]
    vctx = vctx_ref[...]
    for g in range(heads):
        q = q_ref[g]
        s_l = _dot(k_loc, q) + bias
        s_x = _dot(kctx, q)
        sk = sink_ref[hv * heads + g] * LOG2E
        m = jnp.maximum(jnp.maximum(jnp.max(s_l, axis=0, keepdims=True), jnp.max(s_x, axis=0, keepdims=True)), sk)
        p_l = jnp.exp2(s_l - m)
        p_x = jnp.exp2(s_x - m)
        den = jnp.sum(p_l, axis=0, keepdims=True) + jnp.sum(p_x, axis=0, keepdims=True) + jnp.exp2(sk - m)
        acc = _dot(v_loc, p_l.astype(BF16)) + _dot(vctx, p_x.astype(BF16))
        o_ref[g] = (acc / den).astype(o_ref.dtype)


def _win_attn(q_t, k, v_t, sink, s, lc, *, tq):
    b, hq = q_t.shape[0], q_t.shape[1]
    hv, dv = v_t.shape[1], v_t.shape[2]
    heads = hq // hv
    r = tq // WINDOW
    nb = s // WINDOW
    prev = lambda qi: jnp.maximum(qi * r - 1, 0)
    nxt = lambda qi: jnp.minimum((qi + 1) * r, nb - 1)
    kpos = jnp.arange(tq + 2 * WINDOW, dtype=jnp.int32)[:, None] - WINDOW
    qpos = jnp.arange(tq, dtype=jnp.int32)[None, :]
    bias = jnp.where(jnp.abs(qpos - kpos) <= WINDOW, 0.0, NEG_INF).astype(F32)
    return pl.pallas_call(
        functools.partial(_win_kernel, heads=heads, tq=tq),
        grid=(b, hv, s // tq),
        in_specs=[
            pl.BlockSpec((None, heads, HEAD_PAD, tq), lambda i, h, qi: (i, h, 0, qi)),
            pl.BlockSpec((None, WINDOW, HEAD_PAD), lambda i, h, qi: (i, prev(qi), 0)),
            pl.BlockSpec((None, tq, HEAD_PAD), lambda i, h, qi: (i, qi, 0)),
            pl.BlockSpec((None, WINDOW, HEAD_PAD), lambda i, h, qi: (i, nxt(qi), 0)),
            pl.BlockSpec((None, None, dv, WINDOW), lambda i, h, qi: (i, h, 0, prev(qi))),
            pl.BlockSpec((None, None, dv, tq), lambda i, h, qi: (i, h, 0, qi)),
            pl.BlockSpec((None, None, dv, WINDOW), lambda i, h, qi: (i, h, 0, nxt(qi))),
            pl.BlockSpec((None, lc, HEAD_PAD), lambda i, h, qi: (i, s // lc, 0)),
            pl.BlockSpec((None, None, dv, lc), lambda i, h, qi: (i, h, 0, s // lc)),
            pl.BlockSpec(bias.shape, lambda i, h, qi: (0, 0)),
            pl.BlockSpec(memory_space=pltpu.SMEM),
        ],
        out_specs=pl.BlockSpec((None, heads, dv, tq), lambda i, h, qi: (i, h, 0, qi)),
        out_shape=jax.ShapeDtypeStruct((b, hq, dv, s), BF16),
        compiler_params=_cparams(("parallel", "parallel", "parallel")),
        name="win_attn",
    )(q_t, k, k, k, v_t, v_t, v_t, k, v_t, bias, sink)


def _out_cd_kernel(x_ref, oc_ref, gb_ref, up_ref, u_ref, un_ref, cw_ref, sg_ref, gate_ref, w_ref, o_ref, *, ts):
    t = pl.program_id(1)
    nt = pl.num_programs(1)
    u = u_ref[...].astype(F32)
    lane = lax.broadcasted_iota(jnp.int32, (CONV_CH, ts), 1)
    halo_l = pltpu.roll(up_ref[...].astype(F32), 1, axis=1)
    halo_l = jnp.where(t > 0, halo_l, 0.0)
    halo_r = pltpu.roll(un_ref[...].astype(F32), LANES - 1, axis=1)
    halo_r = jnp.where(t < nt - 1, halo_r, 0.0)
    reps = ts // LANES
    um = jnp.where(lane == 0, jnp.tile(halo_l, (1, reps)), pltpu.roll(u, 1, axis=1))
    up = jnp.where(lane == ts - 1, jnp.tile(halo_r, (1, reps)), pltpu.roll(u, ts - 1, axis=1))
    w0 = jnp.tile(cw_ref[0], (1, reps))
    w1 = jnp.tile(cw_ref[1], (1, reps))
    w2 = jnp.tile(cw_ref[2], (1, reps))
    od = gb_ref[...].astype(F32) * (w0 * um + w1 * u + w2 * up)
    sg = sg_ref[...]
    half = D_MODEL - CONV_CH
    m = jnp.concatenate([oc_ref[...] * sg[0:half], od.astype(BF16) * sg[half:]], axis=0)
    y_t = _dot(w_ref[...], m)
    o_ref[...] = x_ref[...] + gate_ref[...] * y_t.T


def _out_cd(x, oc_t, gb_t, u_t, conv_wb, sg_t, gate, w_out_t, ts):
    b, n, d = x.shape
    r = ts // LANES
    nb = n // LANES
    return pl.pallas_call(
        functools.partial(_out_cd_kernel, ts=ts),
        grid=(b, n // ts),
        in_specs=[
            pl.BlockSpec((None, ts, d), lambda i, t: (i, t, 0)),
            pl.BlockSpec((None, d - CONV_CH, ts), lambda i, t: (i, 0, t)),
            pl.BlockSpec((None, CONV_CH, ts), lambda i, t: (i, 0, t)),
            pl.BlockSpec((None, CONV_CH, LANES), lambda i, t: (i, 0, jnp.maximum(t * r - 1, 0))),
            pl.BlockSpec((None, CONV_CH, ts), lambda i, t: (i, 0, t)),
            pl.BlockSpec((None, CONV_CH, LANES), lambda i, t: (i, 0, jnp.minimum((t + 1) * r, nb - 1))),
            pl.BlockSpec((CONV_K, CONV_CH, LANES), lambda i, t: (0, 0, 0)),
            pl.BlockSpec((None, d, ts), lambda i, t: (i, 0, t)),
            pl.BlockSpec((None, 1, d), lambda i, t: (i, 0, 0)),
            pl.BlockSpec((d, d), lambda i, t: (0, 0)),
        ],
        out_specs=pl.BlockSpec((None, ts, d), lambda i, t: (i, t, 0)),
        out_shape=jax.ShapeDtypeStruct((b, n, d), F32),
        compiler_params=_cparams(("parallel", "parallel")),
        name="out_cd",
    )(x, oc_t, gb_t, u_t, u_t, u_t, conv_wb, sg_t, gate, w_out_t)


def _rope_tables(s, lc, dd):
    q = dd // 4
    rows = s // GRID_W
    row = jnp.repeat(jnp.arange(rows, dtype=F32), GRID_W)
    col = jnp.tile(jnp.arange(GRID_W, dtype=F32), rows)
    inv = ROPE_BASE ** (-jnp.arange(q, dtype=F32) / q)
    ar = inv[:, None] * row[None, :]
    ac = inv[:, None] * col[None, :]
    cos = jnp.concatenate([jnp.cos(ar), jnp.cos(ar), jnp.cos(ac), jnp.cos(ac)], axis=0)
    sin = jnp.concatenate([-jnp.sin(ar), jnp.sin(ar), -jnp.sin(ac), jnp.sin(ac)], axis=0)
    return (jnp.concatenate([cos, jnp.ones((dd, lc), F32)], axis=1),
            jnp.concatenate([sin, jnp.zeros((dd, lc), F32)], axis=1))


def _tile_rows(n, cap):
    t = cap
    while n % t:
        t //= 2
    return t


def _key_tile(n, cap, step):
    t = (cap // step) * step
    while n % t:
        t -= step
    return t


def kernel(x, c, ctx, c_ctx, mod_w, mod_b, ab_w_in, ab_w_out, mla_cq_gain, mla_ckv_gain, mla_w_uq, mla_w_ukv,
           mla_q_gain, mla_k_gain, gqa_q_gain, gqa_k_gain, cd_w_in, cd_w_out, win_q_gain, win_k_gain,
           win_sink, conv_w):
    b, s, d = x.shape
    lc = ctx.shape[1]
    ts_p = _tile_rows(lc, 256)
    ts_x = _tile_rows(s, 512)
    ts_c = _tile_rows(lc, 256)
    tq = _tile_rows(s, 512)
    tk = _key_tile(s + lc, 1280, LANES)
    kc = 256 if tk % 256 == 0 else tk
    ahead = 2
    tq_w = _tile_rows(s, 256)

    crows = jnp.zeros((8, d), F32).at[0:b].set(c).at[b].set(c_ctx)
    mods = _modulation(crows, mod_w, mod_b)

    def split_mod(layer):
        m = mods[layer]
        shift, scale, gate = m[:, 0:d], m[:, d:2 * d], m[:, 2 * d:3 * d]
        lat = jnp.stack([shift[0:b], scale[0:b]], axis=1)
        con = jnp.broadcast_to(jnp.stack([shift[b], scale[b]], axis=0)[None], (b, 2, d))
        gate_x = gate[0:b][:, None, :]
        gate_c = jnp.broadcast_to(gate[b][None, None, :], (b, 1, d))
        return jnp.stack([lat, con], axis=1), gate_x, gate_c

    tabs32 = _rope_tables(s, lc, MLA_ROPE)
    tabs64 = _rope_tables(s, lc, GQA_DIM)

    mod4, gate_x, gate_c = split_mod(0)
    w_in_t = ab_w_in[0].T.astype(BF16)
    w_uq_t = mla_w_uq[0].T.astype(BF16)
    w_ukv_t = mla_w_ukv[0].T.astype(BF16)
    w_out_t = ab_w_out[0].T.astype(BF16)
    gains = (mla_cq_gain[0], mla_ckv_gain[0], mla_q_gain[0], mla_k_gain[0], gqa_q_gain[0], gqa_k_gain[0])
    qa, ka, va, qb, kb, vb, sg = _proj_ab(x, ctx, mod4, w_in_t, w_uq_t, w_ukv_t, gains, tabs32 + tabs64, ts_p)
    kb = kb[:, None]
    oa = _stream_attn(qa, ka, va, s, tq=tq, tk=tk, kc=kc, ahead=ahead, name="flash_mla")
    ob = _stream_attn(qb, kb, vb, s, tq=tq, tk=tk, kc=kc, ahead=ahead, name="flash_gqa")
    x1 = _out_ab(x, oa.reshape(b, -1, s), ob.reshape(b, -1, s), sg, 0, gate_x, w_out_t, ts_x)
    oac = _ctx_attn(qa, ka, va, s, lc, "ctx_mla")
    obc = _ctx_attn(qb, kb, vb, s, lc, "ctx_gqa")
    xc1 = _out_ab(ctx, oac.reshape(b, -1, lc), obc.reshape(b, -1, lc), sg, s // ts_c, gate_c, w_out_t, ts_c)

    mod4, gate_x, gate_c = split_mod(1)
    w_in_t = cd_w_in[0].T.astype(BF16)
    w_out_t = cd_w_out[0].T.astype(BF16)
    q, k, v, gb, u, sg = _proj_cd(x1, xc1, mod4, w_in_t, (win_q_gain[0], win_k_gain[0]), tabs64, ts_p)
    oc = _win_attn(q, k, v, win_sink[0].astype(F32), s, lc, tq=tq_w)
    conv_wb = jnp.broadcast_to(conv_w[0].astype(F32)[:, :, None], (CONV_K, CONV_CH, LANES))
    return _out_cd(x1, oc.reshape(b, -1, s), gb, u, conv_wb, sg, gate_x, w_out_t, ts_x)
```

```python
import functools
import math

import jax
import jax.numpy as jnp
from jax import lax
from jax.experimental import pallas as pl
from jax.experimental.pallas import tpu as pltpu

D_MODEL = 1024
GRID_W = 64
ROPE_BASE = 10000.0
EPS = 1e-6
NEG_INF = -1e30
WINDOW = 128

MLA_HEADS = 8
MLA_NOPE = 64
MLA_ROPE = 32
MLA_V = 64
MLA_QK = MLA_NOPE + MLA_ROPE
MLA_Q_RANK = 256
MLA_KV_RANK = 256
GQA_HEADS = 8
GQA_KV_HEADS = 2
GQA_DIM = 64
WIN_HEADS = 8
WIN_KV_HEADS = 2
WIN_DIM = 64
CONV_CH = 512
CONV_K = 3

LANES = 128
HEAD_PAD = 128
STAT_ROWS = 8
LOG2E = math.log2(math.e)
VMEM_LIMIT_BYTES = 56 * 1024 * 1024

BF16 = jnp.bfloat16
F32 = jnp.float32


def _cparams(sem):
    return pltpu.CompilerParams(dimension_semantics=sem, vmem_limit_bytes=VMEM_LIMIT_BYTES)


def _dot(a, b):
    return jnp.dot(a, b, preferred_element_type=F32)


def _dot_nt(a, b):
    return lax.dot_general(a, b, (((1,), (1,)), ((), ())), preferred_element_type=F32)


def _silu(v):
    return v * (1.0 / (1.0 + jnp.exp(-v)))


def _mod_kernel(c_ref, w_ref, b_ref, o_ref):
    sc = _silu(c_ref[...])
    o_ref[...] = _dot(sc, w_ref[...]) + b_ref[...]


def _modulation(crows, mod_w, mod_b):
    depth, d, n = mod_w.shape
    tn = 1024
    return pl.pallas_call(
        _mod_kernel,
        grid=(depth, n // tn),
        in_specs=[
            pl.BlockSpec((8, d), lambda l, j: (0, 0)),
            pl.BlockSpec((None, d, tn), lambda l, j: (l, 0, j)),
            pl.BlockSpec((None, 1, tn), lambda l, j: (l, 0, j)),
        ],
        out_specs=pl.BlockSpec((None, 8, tn), lambda l, j: (l, 0, j)),
        out_shape=jax.ShapeDtypeStruct((depth, 8, n), F32),
        compiler_params=_cparams(("parallel", "parallel")),
        name="modulation",
    )(crows, mod_w, mod_b.reshape(depth, 1, n))


def _adaln_rows(x, shift, scale):
    ms = jnp.mean(x * x, axis=-1, keepdims=True)
    return (x * lax.rsqrt(ms + EPS)) * (1.0 + scale) + shift


def _rms_cols(v, gain_b, n):
    ss = jnp.sum(v * v, axis=0, keepdims=True)
    return v * lax.rsqrt(ss * (1.0 / n) + EPS) * gain_b


def _rope_cols(v, cos, sin, q):
    swap = jnp.concatenate([v[q:2 * q], v[0:q], v[3 * q:4 * q], v[2 * q:3 * q]], axis=0)
    return v * cos + swap * sin


def _joint_rows(x_ref, c_ref, mod_ref, nx):
    rows = jnp.where(pl.program_id(1) < nx, x_ref[...], c_ref[...])
    return _adaln_rows(rows, mod_ref[0:1, :], mod_ref[1:2, :]).astype(BF16)


def _joint_specs(nx, nc_t, ts, d):
    return [
        pl.BlockSpec((None, ts, d), lambda i, t: (i, jnp.minimum(t, nx - 1), 0)),
        pl.BlockSpec((None, ts, d), lambda i, t: (i, jnp.clip(t - nx, 0, nc_t - 1), 0)),
        pl.BlockSpec((None, None, 2, d), lambda i, t: (i, jnp.where(t < nx, 0, 1), 0, 0)),
    ]


def _proj_ab_kernel(x_ref, c_ref, mod_ref, w_in_ref, cqg_ref, ckvg_ref, w_uq_ref, w_ukv_ref,
                    qg_ref, kg_ref, gqg_ref, gkg_ref, c32_ref, s32_ref, c64_ref, s64_ref,
                    qa_ref, ka_ref, va_ref, qb_ref, kb_ref, vb_ref, sg_ref, *, nx):
    ts = x_ref.shape[0]
    h = _joint_rows(x_ref, c_ref, mod_ref, nx)
    pp = _dot_nt(w_in_ref[...], h)

    cq = pp[0:256]
    ckv = pp[256:512]
    kr = pp[512:544]
    cqn = _rms_cols(cq, cqg_ref[...], MLA_Q_RANK).astype(BF16)
    ckvn = _rms_cols(ckv, ckvg_ref[...], MLA_KV_RANK).astype(BF16)
    qa = _dot(w_uq_ref[...], cqn)
    kv = _dot(w_ukv_ref[...], ckvn)
    qg = qg_ref[...]
    kg = kg_ref[...]
    c32 = c32_ref[...]
    s32 = s32_ref[...]
    kr_ss = jnp.sum(kr * kr, axis=0, keepdims=True)
    zpad = jnp.zeros((HEAD_PAD - MLA_QK, ts), F32)
    q_scale = MLA_QK ** -0.5 * LOG2E
    for hd in range(MLA_HEADS):
        qh = _rms_cols(qa[hd * MLA_QK:(hd + 1) * MLA_QK], qg, MLA_QK)
        qr = _rope_cols(qh[MLA_NOPE:], c32, s32, MLA_ROPE // 4)
        qa_ref[hd, 0:MLA_NOPE, :] = (qh[0:MLA_NOPE] * q_scale).astype(BF16)
        qa_ref[hd, MLA_NOPE:MLA_QK, :] = (qr * q_scale).astype(BF16)
        qa_ref[hd, MLA_QK:HEAD_PAD, :] = zpad.astype(BF16)
        kn = kv[hd * 128:hd * 128 + MLA_NOPE]
        ss = jnp.sum(kn * kn, axis=0, keepdims=True) + kr_ss
        rs = lax.rsqrt(ss * (1.0 / MLA_QK) + EPS)
        krn = _rope_cols(kr * rs * kg[MLA_NOPE:], c32, s32, MLA_ROPE // 4)
        k_cols = jnp.concatenate([kn * rs * kg[0:MLA_NOPE], krn, zpad], axis=0)
        ka_ref[hd] = k_cols.T.astype(BF16)
        va_ref[hd] = kv[hd * 128 + MLA_NOPE:(hd + 1) * 128].astype(BF16)

    gq = pp[544:1056]
    gk = pp[1056:1184]
    gv = pp[1184:1312]
    c64 = c64_ref[...]
    s64 = s64_ref[...]
    gqg = gqg_ref[...]
    gkg = gkg_ref[...]
    g_scale = GQA_DIM ** -0.5 * LOG2E
    z64 = jnp.zeros((GQA_DIM, ts), BF16)
    grp = GQA_HEADS // GQA_KV_HEADS
    for hd in range(GQA_HEADS):
        qh = _rms_cols(gq[hd * GQA_DIM:(hd + 1) * GQA_DIM], gqg, GQA_DIM)
        qh = (_rope_cols(qh, c64, s64, GQA_DIM // 4) * g_scale).astype(BF16)
        for kvh in range(GQA_KV_HEADS):
            qb_ref[hd, kvh * GQA_DIM:(kvh + 1) * GQA_DIM, :] = qh if kvh == hd // grp else z64
    k_heads = []
    for kvh in range(GQA_KV_HEADS):
        kh = _rms_cols(gk[kvh * GQA_DIM:(kvh + 1) * GQA_DIM], gkg, GQA_DIM)
        k_heads.append(_rope_cols(kh, c64, s64, GQA_DIM // 4))
        vb_ref[kvh] = gv[kvh * GQA_DIM:(kvh + 1) * GQA_DIM].astype(BF16)
    kb_ref[...] = jnp.concatenate(k_heads, axis=0).T.astype(BF16)
    sg_ref[...] = _silu(pp[1312:2336]).astype(BF16)


def _bcast_gain(g, ts):
    return jnp.broadcast_to(g.astype(F32)[:, None], (g.shape[0], ts))


def _proj_ab(x, ctx, mods, w_in_t, w_uq_t, w_ukv_t, gains, tabs, ts):
    b, s, d = x.shape
    lc = ctx.shape[1]
    n = s + lc
    nx, nc_t = s // ts, lc // ts
    cqg, ckvg, qg, kg, gqg, gkg = [_bcast_gain(g, ts) for g in gains]
    c32, s32, c64, s64 = tabs
    full = lambda a: pl.BlockSpec(a.shape, lambda i, t: (0,) * a.ndim)
    tab = lambda a: pl.BlockSpec((a.shape[0], ts), lambda i, t: (0, t))
    feat4 = lambda h, r: (jax.ShapeDtypeStruct((b, h, r, n), BF16),
                          pl.BlockSpec((None, h, r, ts), lambda i, t: (i, 0, 0, t)))
    rows4 = lambda h: (jax.ShapeDtypeStruct((b, h, n, HEAD_PAD), BF16),
                       pl.BlockSpec((None, h, ts, HEAD_PAD), lambda i, t: (i, 0, t, 0)))
    outs = [
        feat4(MLA_HEADS, HEAD_PAD),
        rows4(MLA_HEADS),
        feat4(MLA_HEADS, MLA_V),
        feat4(GQA_HEADS, HEAD_PAD),
        (jax.ShapeDtypeStruct((b, n, HEAD_PAD), BF16),
         pl.BlockSpec((None, ts, HEAD_PAD), lambda i, t: (i, t, 0))),
        feat4(GQA_KV_HEADS, GQA_DIM),
        (jax.ShapeDtypeStruct((b, D_MODEL, n), BF16),
         pl.BlockSpec((None, D_MODEL, ts), lambda i, t: (i, 0, t))),
    ]
    return pl.pallas_call(
        functools.partial(_proj_ab_kernel, nx=nx),
        grid=(b, nx + nc_t),
        in_specs=_joint_specs(nx, nc_t, ts, d) + [
            full(w_in_t), full(cqg), full(ckvg), full(w_uq_t), full(w_ukv_t),
            full(qg), full(kg), full(gqg), full(gkg),
            tab(c32), tab(s32), tab(c64), tab(s64),
        ],
        out_specs=[o[1] for o in outs],
        out_shape=[o[0] for o in outs],
        compiler_params=_cparams(("parallel", "parallel")),
        name="proj_ab",
    )(x, ctx, mods, w_in_t, cqg, ckvg, w_uq_t, w_ukv_t, qg, kg, gqg, gkg, c32, s32, c64, s64)


P_LIMIT = 2.0 ** 100
FIRST_REF_KEYS = 128


def _stream_kernel(q_ref, k_ref, v_ref, o_ref, m_sc, acc_sc, pv_sc, *, heads, kc, ahead):
    j = pl.program_id(2)
    tk = k_ref.shape[1]
    tq = q_ref.shape[2]
    dv = o_ref.shape[1]
    nc = tk // kc
    gk = k_ref.shape[0]
    gv = v_ref.shape[0]
    cur = lax.rem(j, 2)
    nxt = 1 - cur

    @pl.when(j == 0)
    def _():
        for g in range(heads):
            s0 = _dot(k_ref[g * gk // heads, 0:FIRST_REF_KEYS, :], q_ref[g])
            m_sc[0, g] = jnp.max(s0, axis=0, keepdims=True)
        acc_sc[0] = jnp.zeros(acc_sc.shape[1:], F32)

    units = [(g, c) for g in range(heads) for c in range(nc)]
    pending = {}

    def emit_scores(u):
        g, c = u
        pending[u] = _dot(k_ref[g * gk // heads, c * kc:(c + 1) * kc, :], q_ref[g])

    worst = None
    issued = 0
    pmx = None
    lsum = None
    for i, (g, c) in enumerate(units):
        while issued <= min(i + ahead, len(units) - 1):
            emit_scores(units[issued])
            issued += 1
        s = pending.pop((g, c))
        m_ref = m_sc[cur, g]
        mb = jnp.broadcast_to(m_ref, (8, tq))
        p32 = jnp.exp2(s.reshape(kc // 8, 8, tq) - mb[None])
        lpart = jnp.sum(p32, axis=0)
        lsum = lpart if c == 0 else lsum + lpart
        p = p32.reshape(kc, tq).astype(BF16)
        part = jnp.max(p.reshape(kc // 16, 16, tq), axis=0)
        pmx = part if c == 0 else jnp.maximum(pmx, part)
        d = _dot(v_ref[g * gv // heads, :, c * kc:(c + 1) * kc], p)
        if c == 0:
            pv_sc[g] = d
        else:
            pv_sc[g] += d
        if c == nc - 1:
            pm = jnp.maximum(jnp.max(pmx.astype(F32), axis=0, keepdims=True), 1.0)
            worst = pm if worst is None else jnp.maximum(worst, pm)
            m_sc[nxt, g] = m_ref + jnp.log2(pm)
            rp = 1.0 / pm
            acc_sc[nxt, g, 0:dv, :] = (acc_sc[cur, g, 0:dv, :] + pv_sc[g]) * rp
            acc_sc[nxt, g, dv:dv + 1, :] = (acc_sc[cur, g, dv:dv + 1, :] + jnp.sum(lsum, axis=0, keepdims=True)) * rp

    redo = jnp.logical_not(jnp.max(worst) <= P_LIMIT)

    @pl.when(redo)
    def _():
        def head(g, carry):
            q = q_ref[g]
            gki = g * gk // heads
            gvi = g * gv // heads
            m_old = m_sc[cur, g]

            def cmax(c, mx):
                s = _dot(k_ref[gki, pl.ds(pl.multiple_of(c * kc, kc), kc), :], q)
                return jnp.maximum(mx, jnp.max(s, axis=0, keepdims=True))

            m_new = lax.fori_loop(0, nc, cmax, m_old)

            def cacc(c, st):
                pv, l = st
                ks = pl.ds(pl.multiple_of(c * kc, kc), kc)
                p = jnp.exp2(_dot(k_ref[gki, ks, :], q) - m_new)
                return pv + _dot(v_ref[gvi, :, ks], p.astype(BF16)), l + jnp.sum(p, axis=0, keepdims=True)

            pv, l = lax.fori_loop(0, nc, cacc, (jnp.zeros((dv, tq), F32), jnp.zeros((1, tq), F32)))
            alpha = jnp.exp2(m_old - m_new)
            m_sc[nxt, g] = m_new
            acc_sc[nxt, g, 0:dv, :] = acc_sc[cur, g, 0:dv, :] * alpha + pv
            acc_sc[nxt, g, dv:dv + 1, :] = acc_sc[cur, g, dv:dv + 1, :] * alpha + l
            return carry

        lax.fori_loop(0, heads, head, 0)

    @pl.when(j == pl.num_programs(2) - 1)
    def _():
        for g in range(heads):
            o_ref[g] = (acc_sc[nxt, g, 0:dv, :] / acc_sc[nxt, g, dv:dv + 1, :]).astype(o_ref.dtype)


def _stream_attn(q_t, k, v_t, s, *, tq, tk, kc, ahead, name):
    b, hq = q_t.shape[0], q_t.shape[1]
    hk, n = k.shape[1], k.shape[2]
    hv, dv = v_t.shape[1], v_t.shape[2]
    return pl.pallas_call(
        functools.partial(_stream_kernel, heads=hq, kc=kc, ahead=ahead),
        grid=(b, s // tq, n // tk),
        in_specs=[
            pl.BlockSpec((None, hq, HEAD_PAD, tq), lambda i, qi, j: (i, 0, 0, qi)),
            pl.BlockSpec((None, hk, tk, HEAD_PAD), lambda i, qi, j: (i, 0, j, 0)),
            pl.BlockSpec((None, hv, dv, tk), lambda i, qi, j: (i, 0, 0, j)),
        ],
        out_specs=pl.BlockSpec((None, hq, dv, tq), lambda i, qi, j: (i, 0, 0, qi)),
        out_shape=jax.ShapeDtypeStruct((b, hq, dv, s), BF16),
        scratch_shapes=[
            pltpu.VMEM((2, hq, 1, tq), F32),
            pltpu.VMEM((2, hq, dv + STAT_ROWS, tq), F32),
            pltpu.VMEM((hq, dv, tq), F32),
        ],
        compiler_params=_cparams(("parallel", "parallel", "arbitrary")),
        name=name,
    )(q_t, k, v_t)


def _ctx_attn_kernel(q_ref, k_ref, v_ref, o_ref):
    heads, gk, gv = q_ref.shape[0], k_ref.shape[0], v_ref.shape[0]
    for g in range(heads):
        s = _dot(k_ref[g * gk // heads], q_ref[g])
        p = jnp.exp2(s - jnp.max(s, axis=0, keepdims=True))
        den = jnp.sum(p, axis=0, keepdims=True)
        o_ref[g] = (_dot(v_ref[g * gv // heads], p.astype(BF16)) / den).astype(o_ref.dtype)


def _ctx_attn(q_t, k, v_t, s, lc, name):
    b, hq = q_t.shape[0], q_t.shape[1]
    hk = k.shape[1]
    hv, dv = v_t.shape[1], v_t.shape[2]
    off = s // lc
    return pl.pallas_call(
        _ctx_attn_kernel,
        grid=(b,),
        in_specs=[
            pl.BlockSpec((None, hq, HEAD_PAD, lc), lambda i: (i, 0, 0, off)),
            pl.BlockSpec((None, hk, lc, HEAD_PAD), lambda i: (i, 0, off, 0)),
            pl.BlockSpec((None, hv, dv, lc), lambda i: (i, 0, 0, off)),
        ],
        out_specs=pl.BlockSpec((None, hq, dv, lc), lambda i: (i, 0, 0, 0)),
        out_shape=jax.ShapeDtypeStruct((b, hq, dv, lc), BF16),
        compiler_params=_cparams(("parallel",)),
        name=name,
    )(q_t, k, v_t)


def _out_ab_kernel(x_ref, oa_ref, ob_ref, sg_ref, gate_ref, w_ref, o_ref):
    m = jnp.concatenate([oa_ref[...], ob_ref[...]], axis=0) * sg_ref[...]
    y_t = _dot(w_ref[...], m)
    o_ref[...] = x_ref[...] + gate_ref[...] * y_t.T


def _out_ab(x, oa_t, ob_t, sg_t, sg_off, gate, w_out_t, ts):
    b, n, d = x.shape
    half = oa_t.shape[1]
    return pl.pallas_call(
        _out_ab_kernel,
        grid=(b, n // ts),
        in_specs=[
            pl.BlockSpec((None, ts, d), lambda i, t: (i, t, 0)),
            pl.BlockSpec((None, half, ts), lambda i, t: (i, 0, t)),
            pl.BlockSpec((None, d - half, ts), lambda i, t: (i, 0, t)),
            pl.BlockSpec((None, d, ts), lambda i, t: (i, 0, t + sg_off)),
            pl.BlockSpec((None, 1, d), lambda i, t: (i, 0, 0)),
            pl.BlockSpec((d, d), lambda i, t: (0, 0)),
        ],
        out_specs=pl.BlockSpec((None, ts, d), lambda i, t: (i, t, 0)),
        out_shape=jax.ShapeDtypeStruct((b, n, d), F32),
        compiler_params=_cparams(("parallel", "parallel")),
        name="out_ab",
    )(x, oa_t, ob_t, sg_t, gate, w_out_t)


def _proj_cd_kernel(x_ref, c_ref, mod_ref, w_in_ref, qg_ref, kg_ref, c64_ref, s64_ref,
                    q_ref, k_ref, v_ref, gb_ref, u_ref, sg_ref, *, nx):
    ts = x_ref.shape[0]
    h = _joint_rows(x_ref, c_ref, mod_ref, nx)
    pp = _dot_nt(w_in_ref[...], h)
    c64 = c64_ref[...]
    s64 = s64_ref[...]
    qg = qg_ref[...]
    kg = kg_ref[...]
    scale = WIN_DIM ** -0.5 * LOG2E
    z64 = jnp.zeros((WIN_DIM, ts), BF16)
    grp = WIN_HEADS // WIN_KV_HEADS
    for hd in range(WIN_HEADS):
        qh = _rms_cols(pp[hd * WIN_DIM:(hd + 1) * WIN_DIM], qg, WIN_DIM)
        qh = (_rope_cols(qh, c64, s64, WIN_DIM // 4) * scale).astype(BF16)
        for kvh in range(WIN_KV_HEADS):
            q_ref[hd, kvh * WIN_DIM:(kvh + 1) * WIN_DIM, :] = qh if kvh == hd // grp else z64
    k_heads = []
    for kvh in range(WIN_KV_HEADS):
        kh = _rms_cols(pp[512 + kvh * WIN_DIM:512 + (kvh + 1) * WIN_DIM], kg, WIN_DIM)
        k_heads.append(_rope_cols(kh, c64, s64, WIN_DIM // 4))
        v_ref[kvh] = pp[640 + kvh * WIN_DIM:640 + (kvh + 1) * WIN_DIM].astype(BF16)
    k_ref[...] = jnp.concatenate(k_heads, axis=0).T.astype(BF16)
    gb_ref[...] = pp[768:1280].astype(BF16)
    u_ref[...] = (pp[1280:1792] * pp[1792:2304]).astype(BF16)
    sg_ref[...] = _silu(pp[2304:3328]).astype(BF16)


def _proj_cd(x, ctx, mods, w_in_t, gains, tabs, ts):
    b, s, d = x.shape
    lc = ctx.shape[1]
    n = s + lc
    nx, nc_t = s // ts, lc // ts
    qg, kg = [_bcast_gain(g, ts) for g in gains]
    c64, s64 = tabs
    full = lambda a: pl.BlockSpec(a.shape, lambda i, t: (0,) * a.ndim)
    tab = lambda a: pl.BlockSpec((a.shape[0], ts), lambda i, t: (0, t))
    feat3 = lambda r: (jax.ShapeDtypeStruct((b, r, n), BF16), pl.BlockSpec((None, r, ts), lambda i, t: (i, 0, t)))
    outs = [
        (jax.ShapeDtypeStruct((b, WIN_HEADS, HEAD_PAD, n), BF16),
         pl.BlockSpec((None, WIN_HEADS, HEAD_PAD, ts), lambda i, t: (i, 0, 0, t))),
        (jax.ShapeDtypeStruct((b, n, HEAD_PAD), BF16),
         pl.BlockSpec((None, ts, HEAD_PAD), lambda i, t: (i, t, 0))),
        (jax.ShapeDtypeStruct((b, WIN_KV_HEADS, WIN_DIM, n), BF16),
         pl.BlockSpec((None, WIN_KV_HEADS, WIN_DIM, ts), lambda i, t: (i, 0, 0, t))),
        feat3(CONV_CH),
        feat3(CONV_CH),
        feat3(D_MODEL),
    ]
    return pl.pallas_call(
        functools.partial(_proj_cd_kernel, nx=nx),
        grid=(b, nx + nc_t),
        in_specs=_joint_specs(nx, nc_t, ts, d) + [full(w_in_t), full(qg), full(kg), tab(c64), tab(s64)],
        out_specs=[o[1] for o in outs],
        out_shape=[o[0] for o in outs],
        compiler_params=_cparams(("parallel", "parallel")),
        name="proj_cd",
    )(x, ctx, mods, w_in_t, qg, kg, c64, s64)


def _win_kernel(q_ref, kp_ref, kc_ref, kn_ref, vp_ref, vcur_ref, vn_ref, kctx_ref, vctx_ref, bias_ref, sink_ref,
                o_ref, *, sub):
    heads, tq = q_ref.shape[0], q_ref.shape[2]
    gv = vcur_ref.shape[0]
    nsub = tq // sub
    qi = pl.program_id(1)
    edge_p = jnp.where(qi > 0, 0.0, NEG_INF)
    edge_n = jnp.where(qi < pl.num_programs(1) - 1, 0.0, NEG_INF)
    bias_mid = bias_ref[...]
    bias_first = jnp.concatenate([bias_ref[0:WINDOW, :] + edge_p, bias_ref[WINDOW:, :]], axis=0)
    bias_last = jnp.concatenate([bias_ref[0:sub + WINDOW, :], bias_ref[sub + WINDOW:, :] + edge_n], axis=0)
    if nsub == 1:
        bias_first = bias_last = jnp.concatenate(
            [bias_ref[0:WINDOW, :] + edge_p, bias_ref[WINDOW:sub + WINDOW, :], bias_ref[sub + WINDOW:, :] + edge_n],
            axis=0)
    k_loc = jnp.concatenate([kp_ref[...], kc_ref[...], kn_ref[...]], axis=0)
    v_loc = [jnp.concatenate([vp_ref[h], vcur_ref[h], vn_ref[h]], axis=1) for h in range(gv)]
    kctx = kctx_ref[...]
    units = [(g, t) for g in range(heads) for t in range(nsub)]
    pending = {}

    def emit_scores(u):
        g, t = u
        q = q_ref[g, :, t * sub:(t + 1) * sub]
        bias = bias_first if t == 0 else (bias_last if t == nsub - 1 else bias_mid)
        pending[u] = (_dot(k_loc[t * sub:(t + 1) * sub + 2 * WINDOW], q) + bias, _dot(kctx, q))

    emit_scores(units[0])
    for i, (g, t) in enumerate(units):
        if i + 1 < len(units):
            emit_scores(units[i + 1])
        s_l, s_x = pending.pop((g, t))
        sk = sink_ref[g] * LOG2E
        m = jnp.maximum(jnp.maximum(jnp.max(s_l, axis=0, keepdims=True), jnp.max(s_x, axis=0, keepdims=True)), sk)
        p_l = jnp.exp2(s_l - m)
        p_x = jnp.exp2(s_x - m)
        den = jnp.sum(p_l, axis=0, keepdims=True) + jnp.sum(p_x, axis=0, keepdims=True) + jnp.exp2(sk - m)
        h = g * gv // heads
        acc = (_dot(v_loc[h][:, t * sub:(t + 1) * sub + 2 * WINDOW], p_l.astype(BF16))
               + _dot(vctx_ref[h], p_x.astype(BF16)))
        o_ref[g, :, t * sub:(t + 1) * sub] = (acc / den).astype(o_ref.dtype)


def _win_attn(q_t, k, v_t, sink, s, lc, *, tq, sub):
    b, hq = q_t.shape[0], q_t.shape[1]
    hv, dv = v_t.shape[1], v_t.shape[2]
    r = tq // WINDOW
    nb = s // WINDOW
    prev = lambda qi: jnp.maximum(qi * r - 1, 0)
    nxt = lambda qi: jnp.minimum((qi + 1) * r, nb - 1)
    kpos = jnp.arange(sub + 2 * WINDOW, dtype=jnp.int32)[:, None] - WINDOW
    qpos = jnp.arange(sub, dtype=jnp.int32)[None, :]
    bias = jnp.where(jnp.abs(qpos - kpos) <= WINDOW, 0.0, NEG_INF).astype(F32)
    return pl.pallas_call(
        functools.partial(_win_kernel, sub=sub),
        grid=(b, s // tq),
        in_specs=[
            pl.BlockSpec((None, hq, HEAD_PAD, tq), lambda i, qi: (i, 0, 0, qi)),
            pl.BlockSpec((None, WINDOW, HEAD_PAD), lambda i, qi: (i, prev(qi), 0)),
            pl.BlockSpec((None, tq, HEAD_PAD), lambda i, qi: (i, qi, 0)),
            pl.BlockSpec((None, WINDOW, HEAD_PAD), lambda i, qi: (i, nxt(qi), 0)),
            pl.BlockSpec((None, hv, dv, WINDOW), lambda i, qi: (i, 0, 0, prev(qi))),
            pl.BlockSpec((None, hv, dv, tq), lambda i, qi: (i, 0, 0, qi)),
            pl.BlockSpec((None, hv, dv, WINDOW), lambda i, qi: (i, 0, 0, nxt(qi))),
            pl.BlockSpec((None, lc, HEAD_PAD), lambda i, qi: (i, s // lc, 0)),
            pl.BlockSpec((None, hv, dv, lc), lambda i, qi: (i, 0, 0, s // lc)),
            pl.BlockSpec(bias.shape, lambda i, qi: (0, 0)),
            pl.BlockSpec(memory_space=pltpu.SMEM),
        ],
        out_specs=pl.BlockSpec((None, hq, dv, tq), lambda i, qi: (i, 0, 0, qi)),
        out_shape=jax.ShapeDtypeStruct((b, hq, dv, s), BF16),
        compiler_params=_cparams(("parallel", "parallel")),
        name="win_attn",
    )(q_t, k, k, k, v_t, v_t, v_t, k, v_t, bias, sink)


def _out_cd_kernel(x_ref, oc_ref, gb_ref, up_ref, u_ref, un_ref, cw_ref, sg_ref, gate_ref, w_ref, o_ref, *, ts):
    t = pl.program_id(1)
    nt = pl.num_programs(1)
    u = u_ref[...].astype(F32)
    lane = lax.broadcasted_iota(jnp.int32, (CONV_CH, ts), 1)
    halo_l = pltpu.roll(up_ref[...].astype(F32), 1, axis=1)
    halo_l = jnp.where(t > 0, halo_l, 0.0)
    halo_r = pltpu.roll(un_ref[...].astype(F32), LANES - 1, axis=1)
    halo_r = jnp.where(t < nt - 1, halo_r, 0.0)
    reps = ts // LANES
    um = jnp.where(lane == 0, jnp.tile(halo_l, (1, reps)), pltpu.roll(u, 1, axis=1))
    up = jnp.where(lane == ts - 1, jnp.tile(halo_r, (1, reps)), pltpu.roll(u, ts - 1, axis=1))
    w0 = jnp.tile(cw_ref[0], (1, reps))
    w1 = jnp.tile(cw_ref[1], (1, reps))
    w2 = jnp.tile(cw_ref[2], (1, reps))
    od = gb_ref[...].astype(F32) * (w0 * um + w1 * u + w2 * up)
    sg = sg_ref[...]
    half = D_MODEL - CONV_CH
    m = jnp.concatenate([oc_ref[...] * sg[0:half], od.astype(BF16) * sg[half:]], axis=0)
    y_t = _dot(w_ref[...], m)
    o_ref[...] = x_ref[...] + gate_ref[...] * y_t.T


def _out_cd(x, oc_t, gb_t, u_t, conv_wb, sg_t, gate, w_out_t, ts):
    b, n, d = x.shape
    r = ts // LANES
    nb = n // LANES
    return pl.pallas_call(
        functools.partial(_out_cd_kernel, ts=ts),
        grid=(b, n // ts),
        in_specs=[
            pl.BlockSpec((None, ts, d), lambda i, t: (i, t, 0)),
            pl.BlockSpec((None, d - CONV_CH, ts), lambda i, t: (i, 0, t)),
            pl.BlockSpec((None, CONV_CH, ts), lambda i, t: (i, 0, t)),
            pl.BlockSpec((None, CONV_CH, LANES), lambda i, t: (i, 0, jnp.maximum(t * r - 1, 0))),
            pl.BlockSpec((None, CONV_CH, ts), lambda i, t: (i, 0, t)),
            pl.BlockSpec((None, CONV_CH, LANES), lambda i, t: (i, 0, jnp.minimum((t + 1) * r, nb - 1))),
            pl.BlockSpec((CONV_K, CONV_CH, LANES), lambda i, t: (0, 0, 0)),
            pl.BlockSpec((None, d, ts), lambda i, t: (i, 0, t)),
            pl.BlockSpec((None, 1, d), lambda i, t: (i, 0, 0)),
            pl.BlockSpec((d, d), lambda i, t: (0, 0)),
        ],
        out_specs=pl.BlockSpec((None, ts, d), lambda i, t: (i, t, 0)),
        out_shape=jax.ShapeDtypeStruct((b, n, d), F32),
        compiler_params=_cparams(("parallel", "parallel")),
        name="out_cd",
    )(x, oc_t, gb_t, u_t, u_t, u_t, conv_wb, sg_t, gate, w_out_t)


def _rope_tables(s, lc, dd):
    q = dd // 4
    rows = s // GRID_W
    row = jnp.repeat(jnp.arange(rows, dtype=F32), GRID_W)
    col = jnp.tile(jnp.arange(GRID_W, dtype=F32), rows)
    inv = ROPE_BASE ** (-jnp.arange(q, dtype=F32) / q)
    ar = inv[:, None] * row[None, :]
    ac = inv[:, None] * col[None, :]
    cos = jnp.concatenate([jnp.cos(ar), jnp.cos(ar), jnp.cos(ac), jnp.cos(ac)], axis=0)
    sin = jnp.concatenate([-jnp.sin(ar), jnp.sin(ar), -jnp.sin(ac), jnp.sin(ac)], axis=0)
    return (jnp.concatenate([cos, jnp.ones((dd, lc), F32)], axis=1),
            jnp.concatenate([sin, jnp.zeros((dd, lc), F32)], axis=1))


def _tile_rows(n, cap):
    t = cap
    while n % t:
        t //= 2
    return t


def _key_tile(n, cap, step):
    t = (cap // step) * step
    while n % t:
        t -= step
    return t


def kernel(x, c, ctx, c_ctx, mod_w, mod_b, ab_w_in, ab_w_out, mla_cq_gain, mla_ckv_gain, mla_w_uq, mla_w_ukv,
           mla_q_gain, mla_k_gain, gqa_q_gain, gqa_k_gain, cd_w_in, cd_w_out, win_q_gain, win_k_gain,
           win_sink, conv_w):
    b, s, d = x.shape
    lc = ctx.shape[1]
    ts_p = _tile_rows(lc, 256)
    ts_x = _tile_rows(s, 512)
    ts_c = _tile_rows(lc, 256)
    tq = _tile_rows(s, 512)
    tk = _key_tile(s + lc, 3328, LANES)
    kc = 256 if tk % 256 == 0 else tk
    ahead = 2
    tq_w = _tile_rows(s, 512)

    crows = jnp.zeros((8, d), F32).at[0:b].set(c).at[b].set(c_ctx)
    mods = _modulation(crows, mod_w, mod_b)

    def split_mod(layer):
        m = mods[layer]
        shift, scale, gate = m[:, 0:d], m[:, d:2 * d], m[:, 2 * d:3 * d]
        lat = jnp.stack([shift[0:b], scale[0:b]], axis=1)
        con = jnp.broadcast_to(jnp.stack([shift[b], scale[b]], axis=0)[None], (b, 2, d))
        gate_x = gate[0:b][:, None, :]
        gate_c = jnp.broadcast_to(gate[b][None, None, :], (b, 1, d))
        return jnp.stack([lat, con], axis=1), gate_x, gate_c

    tabs32 = _rope_tables(s, lc, MLA_ROPE)
    tabs64 = _rope_tables(s, lc, GQA_DIM)

    mod4, gate_x, gate_c = split_mod(0)
    w_in_t = ab_w_in[0].T.astype(BF16)
    w_uq_t = mla_w_uq[0].T.astype(BF16)
    w_ukv_t = mla_w_ukv[0].T.astype(BF16)
    w_out_t = ab_w_out[0].T.astype(BF16)
    gains = (mla_cq_gain[0], mla_ckv_gain[0], mla_q_gain[0], mla_k_gain[0], gqa_q_gain[0], gqa_k_gain[0])
    qa, ka, va, qb, kb, vb, sg = _proj_ab(x, ctx, mod4, w_in_t, w_uq_t, w_ukv_t, gains, tabs32 + tabs64, ts_p)
    kb = kb[:, None]
    oa = _stream_attn(qa, ka, va, s, tq=tq, tk=tk, kc=kc, ahead=ahead, name="flash_mla")
    ob = _stream_attn(qb, kb, vb, s, tq=tq, tk=tk, kc=kc, ahead=ahead, name="flash_gqa")
    x1 = _out_ab(x, oa.reshape(b, -1, s), ob.reshape(b, -1, s), sg, 0, gate_x, w_out_t, ts_x)
    oac = _ctx_attn(qa, ka, va, s, lc, "ctx_mla")
    obc = _ctx_attn(qb, kb, vb, s, lc, "ctx_gqa")
    xc1 = _out_ab(ctx, oac.reshape(b, -1, lc), obc.reshape(b, -1, lc), sg, s // ts_c, gate_c, w_out_t, ts_c)

    mod4, gate_x, gate_c = split_mod(1)
    w_in_t = cd_w_in[0].T.astype(BF16)
    w_out_t = cd_w_out[0].T.astype(BF16)
    q, k, v, gb, u, sg = _proj_cd(x1, xc1, mod4, w_in_t, (win_q_gain[0], win_k_gain[0]), tabs64, ts_p)
    oc = _win_attn(q, k, v, win_sink[0].astype(F32), s, lc, tq=tq_w, sub=min(tq_w, 256))
    conv_wb = jnp.broadcast_to(conv_w[0].astype(F32)[:, :, None], (CONV_K, CONV_CH, LANES))
    return _out_cd(x1, oc.reshape(b, -1, s), gb, u, conv_wb, sg, gate_x, w_out_t, ts_x)
```

```python
import functools
import math

import jax
import jax.numpy as jnp
from jax import lax
from jax.experimental import pallas as pl
from jax.experimental.pallas import tpu as pltpu

D_MODEL = 1024
GRID_W = 64
ROPE_BASE = 10000.0
EPS = 1e-6
NEG_INF = -1e30
WINDOW = 128

MLA_HEADS = 8
MLA_NOPE = 64
MLA_ROPE = 32
MLA_V = 64
MLA_QK = MLA_NOPE + MLA_ROPE
MLA_Q_RANK = 256
MLA_KV_RANK = 256
GQA_HEADS = 8
GQA_KV_HEADS = 2
GQA_DIM = 64
WIN_HEADS = 8
WIN_KV_HEADS = 2
WIN_DIM = 64
CONV_CH = 512
CONV_K = 3

LANES = 128
HEAD_PAD = 128
STAT_ROWS = 8
PROJ_PART = 256
LOG2E = math.log2(math.e)
VMEM_LIMIT_BYTES = 56 * 1024 * 1024

BF16 = jnp.bfloat16
F32 = jnp.float32


def _cparams(sem):
    return pltpu.CompilerParams(dimension_semantics=sem, vmem_limit_bytes=VMEM_LIMIT_BYTES)


def _dot(a, b):
    return jnp.dot(a, b, preferred_element_type=F32)


def _dot_nt(a, b):
    return lax.dot_general(a, b, (((1,), (1,)), ((), ())), preferred_element_type=F32)


def _silu(v):
    return v * (1.0 / (1.0 + jnp.exp(-v)))


def _mod_kernel(c_ref, w_ref, b_ref, o_ref):
    sc = _silu(c_ref[...])
    o_ref[...] = _dot(sc, w_ref[...]) + b_ref[...]


def _modulation(crows, mod_w, mod_b):
    depth, d, n = mod_w.shape
    tn = 1024
    return pl.pallas_call(
        _mod_kernel,
        grid=(depth, n // tn),
        in_specs=[
            pl.BlockSpec((8, d), lambda l, j: (0, 0)),
            pl.BlockSpec((None, d, tn), lambda l, j: (l, 0, j)),
            pl.BlockSpec((None, 1, tn), lambda l, j: (l, 0, j)),
        ],
        out_specs=pl.BlockSpec((None, 8, tn), lambda l, j: (l, 0, j)),
        out_shape=jax.ShapeDtypeStruct((depth, 8, n), F32),
        compiler_params=_cparams(("parallel", "parallel")),
        name="modulation",
    )(crows, mod_w, mod_b.reshape(depth, 1, n))


def _adaln_rows(x, shift, scale):
    ms = jnp.mean(x * x, axis=-1, keepdims=True)
    return (x * lax.rsqrt(ms + EPS)) * (1.0 + scale) + shift


def _rms_cols(v, gain_b, n):
    ss = jnp.sum(v * v, axis=0, keepdims=True)
    return v * lax.rsqrt(ss * (1.0 / n) + EPS) * gain_b


def _rope_cols(v, cos, sin, q):
    swap = jnp.concatenate([v[q:2 * q], v[0:q], v[3 * q:4 * q], v[2 * q:3 * q]], axis=0)
    return v * cos + swap * sin


def _projected_parts(x_ref, c_ref, mod_ref, w_in_ref, nx, tail):
    ts = x_ref.shape[0]
    ctx_rows = jnp.concatenate([c_ref[...]] * (ts // c_ref.shape[0]), axis=0)
    rows = jnp.where(pl.program_id(1) < nx, x_ref[...], ctx_rows)
    shift, scale = mod_ref[0:1, :], mod_ref[1:2, :]
    parts = ts // PROJ_PART

    def project(i):
        h = _adaln_rows(rows[i * PROJ_PART:(i + 1) * PROJ_PART], shift, scale).astype(BF16)
        return _dot_nt(w_in_ref[...], h)

    pp_next = project(0)
    for i in range(parts):
        pp = pp_next
        if i + 1 < parts:
            pp_next = project(i + 1)
        tail(pp, slice(i * PROJ_PART, (i + 1) * PROJ_PART))


def _joint_specs(nx, ts, lc, d):
    return [
        pl.BlockSpec((None, ts, d), lambda i, t: (i, jnp.minimum(t, nx - 1), 0)),
        pl.BlockSpec((None, lc, d), lambda i, t: (i, 0, 0)),
        pl.BlockSpec((None, None, 2, d), lambda i, t: (i, jnp.where(t < nx, 0, 1), 0, 0)),
    ]


def _bcast_gain(g):
    return jnp.broadcast_to(g.astype(F32)[:, None], (g.shape[0], PROJ_PART))


def _proj_ab_kernel(x_ref, c_ref, mod_ref, w_in_ref, cqg_ref, ckvg_ref, w_uq_ref, w_ukv_ref,
                    qg_ref, kg_ref, gqg_ref, gkg_ref, c32_ref, s32_ref, c64_ref, s64_ref,
                    qa_ref, ka_ref, va_ref, qb_ref, kb_ref, vb_ref, sg_ref, *, nx):
    zpad = jnp.zeros((HEAD_PAD - MLA_QK, PROJ_PART), F32)
    z64 = jnp.zeros((GQA_DIM, PROJ_PART), BF16)
    q_scale = MLA_QK ** -0.5 * LOG2E
    g_scale = GQA_DIM ** -0.5 * LOG2E
    grp = GQA_HEADS // GQA_KV_HEADS

    def tail(pp, cols):
        cq = pp[0:256]
        ckv = pp[256:512]
        kr = pp[512:544]
        cqn = _rms_cols(cq, cqg_ref[...], MLA_Q_RANK).astype(BF16)
        ckvn = _rms_cols(ckv, ckvg_ref[...], MLA_KV_RANK).astype(BF16)
        qa = _dot(w_uq_ref[...], cqn)
        kv = _dot(w_ukv_ref[...], ckvn)
        qg = qg_ref[...]
        kg = kg_ref[...]
        c32 = c32_ref[:, cols]
        s32 = s32_ref[:, cols]
        kr_ss = jnp.sum(kr * kr, axis=0, keepdims=True)
        for hd in range(MLA_HEADS):
            qh = _rms_cols(qa[hd * MLA_QK:(hd + 1) * MLA_QK], qg, MLA_QK)
            qr = _rope_cols(qh[MLA_NOPE:], c32, s32, MLA_ROPE // 4)
            qa_ref[hd, 0:MLA_NOPE, cols] = (qh[0:MLA_NOPE] * q_scale).astype(BF16)
            qa_ref[hd, MLA_NOPE:MLA_QK, cols] = (qr * q_scale).astype(BF16)
            qa_ref[hd, MLA_QK:HEAD_PAD, cols] = zpad.astype(BF16)
            kn = kv[hd * 128:hd * 128 + MLA_NOPE]
            ss = jnp.sum(kn * kn, axis=0, keepdims=True) + kr_ss
            rs = lax.rsqrt(ss * (1.0 / MLA_QK) + EPS)
            krn = _rope_cols(kr * rs * kg[MLA_NOPE:], c32, s32, MLA_ROPE // 4)
            k_cols = jnp.concatenate([kn * rs * kg[0:MLA_NOPE], krn, zpad], axis=0)
            ka_ref[hd, cols, :] = k_cols.T.astype(BF16)
            va_ref[hd, :, cols] = kv[hd * 128 + MLA_NOPE:(hd + 1) * 128].astype(BF16)

        gq = pp[544:1056]
        gk = pp[1056:1184]
        gv = pp[1184:1312]
        c64 = c64_ref[:, cols]
        s64 = s64_ref[:, cols]
        gqg = gqg_ref[...]
        gkg = gkg_ref[...]
        for hd in range(GQA_HEADS):
            qh = _rms_cols(gq[hd * GQA_DIM:(hd + 1) * GQA_DIM], gqg, GQA_DIM)
            qh = (_rope_cols(qh, c64, s64, GQA_DIM // 4) * g_scale).astype(BF16)
            for kvh in range(GQA_KV_HEADS):
                qb_ref[hd, kvh * GQA_DIM:(kvh + 1) * GQA_DIM, cols] = qh if kvh == hd // grp else z64
        k_heads = []
        for kvh in range(GQA_KV_HEADS):
            kh = _rms_cols(gk[kvh * GQA_DIM:(kvh + 1) * GQA_DIM], gkg, GQA_DIM)
            k_heads.append(_rope_cols(kh, c64, s64, GQA_DIM // 4))
            vb_ref[kvh, :, cols] = gv[kvh * GQA_DIM:(kvh + 1) * GQA_DIM].astype(BF16)
        kb_ref[cols, :] = jnp.concatenate(k_heads, axis=0).T.astype(BF16)
        sg_ref[:, cols] = _silu(pp[1312:2336]).astype(BF16)

    _projected_parts(x_ref, c_ref, mod_ref, w_in_ref, nx, tail)


def _proj_ab(x, ctx, mods, w_in_t, w_uq_t, w_ukv_t, gains, tabs, ts):
    b, s, d = x.shape
    lc = ctx.shape[1]
    nx = s // ts
    n = s + ts
    cqg, ckvg, qg, kg, gqg, gkg = [_bcast_gain(g) for g in gains]
    c32, s32, c64, s64 = tabs
    full = lambda a: pl.BlockSpec(a.shape, lambda i, t: (0,) * a.ndim)
    tab = lambda a: pl.BlockSpec((a.shape[0], ts), lambda i, t: (0, t))
    feat4 = lambda h, r: (jax.ShapeDtypeStruct((b, h, r, n), BF16),
                          pl.BlockSpec((None, h, r, ts), lambda i, t: (i, 0, 0, t)))
    rows4 = lambda h: (jax.ShapeDtypeStruct((b, h, n, HEAD_PAD), BF16),
                       pl.BlockSpec((None, h, ts, HEAD_PAD), lambda i, t: (i, 0, t, 0)))
    outs = [
        feat4(MLA_HEADS, HEAD_PAD),
        rows4(MLA_HEADS),
        feat4(MLA_HEADS, MLA_V),
        feat4(GQA_HEADS, HEAD_PAD),
        (jax.ShapeDtypeStruct((b, n, HEAD_PAD), BF16),
         pl.BlockSpec((None, ts, HEAD_PAD), lambda i, t: (i, t, 0))),
        feat4(GQA_KV_HEADS, GQA_DIM),
        (jax.ShapeDtypeStruct((b, D_MODEL, n), BF16),
         pl.BlockSpec((None, D_MODEL, ts), lambda i, t: (i, 0, t))),
    ]
    return pl.pallas_call(
        functools.partial(_proj_ab_kernel, nx=nx),
        grid=(b, nx + 1),
        in_specs=_joint_specs(nx, ts, lc, d) + [
            full(w_in_t), full(cqg), full(ckvg), full(w_uq_t), full(w_ukv_t),
            full(qg), full(kg), full(gqg), full(gkg),
            tab(c32), tab(s32), tab(c64), tab(s64),
        ],
        out_specs=[o[1] for o in outs],
        out_shape=[o[0] for o in outs],
        compiler_params=_cparams(("parallel", "parallel")),
        name="proj_ab",
    )(x, ctx, mods, w_in_t, cqg, ckvg, w_uq_t, w_ukv_t, qg, kg, gqg, gkg, c32, s32, c64, s64)


P_LIMIT = 2.0 ** 100
FIRST_REF_KEYS = 128


def _stream_kernel(q_ref, k_ref, v_ref, o_ref, m_sc, acc_sc, pv_sc, *, heads, kc, ahead):
    j = pl.program_id(2)
    tk = k_ref.shape[1]
    tq = q_ref.shape[2]
    dv = o_ref.shape[1]
    nc = tk // kc
    gk = k_ref.shape[0]
    gv = v_ref.shape[0]
    cur = lax.rem(j, 2)
    nxt = 1 - cur

    @pl.when(j == 0)
    def _():
        for g in range(heads):
            s0 = _dot(k_ref[g * gk // heads, 0:FIRST_REF_KEYS, :], q_ref[g])
            m_sc[0, g] = jnp.max(s0, axis=0, keepdims=True)
        acc_sc[0] = jnp.zeros(acc_sc.shape[1:], F32)

    units = [(g, c) for g in range(heads) for c in range(nc)]
    pending = {}

    def emit_scores(u):
        g, c = u
        pending[u] = _dot(k_ref[g * gk // heads, c * kc:(c + 1) * kc, :], q_ref[g])

    worst = None
    issued = 0
    pmx = None
    lsum = None
    for i, (g, c) in enumerate(units):
        while issued <= min(i + ahead, len(units) - 1):
            emit_scores(units[issued])
            issued += 1
        s = pending.pop((g, c))
        m_ref = m_sc[cur, g]
        mb = jnp.broadcast_to(m_ref, (8, tq))
        p32 = jnp.exp2(s.reshape(kc // 8, 8, tq) - mb[None])
        lpart = jnp.sum(p32, axis=0)
        lsum = lpart if c == 0 else lsum + lpart
        p = p32.reshape(kc, tq).astype(BF16)
        part = jnp.max(p.reshape(kc // 16, 16, tq), axis=0)
        pmx = part if c == 0 else jnp.maximum(pmx, part)
        d = _dot(v_ref[g * gv // heads, :, c * kc:(c + 1) * kc], p)
        if c == 0:
            pv_sc[g] = d
        else:
            pv_sc[g] += d
        if c == nc - 1:
            pm = jnp.maximum(jnp.max(pmx.astype(F32), axis=0, keepdims=True), 1.0)
            worst = pm if worst is None else jnp.maximum(worst, pm)
            m_sc[nxt, g] = m_ref + jnp.log2(pm)
            rp = 1.0 / pm
            acc_sc[nxt, g, 0:dv, :] = (acc_sc[cur, g, 0:dv, :] + pv_sc[g]) * rp
            acc_sc[nxt, g, dv:dv + 1, :] = (acc_sc[cur, g, dv:dv + 1, :] + jnp.sum(lsum, axis=0, keepdims=True)) * rp

    redo = jnp.logical_not(jnp.max(worst) <= P_LIMIT)

    @pl.when(redo)
    def _():
        def head(g, carry):
            q = q_ref[g]
            gki = g * gk // heads
            gvi = g * gv // heads
            m_old = m_sc[cur, g]

            def cmax(c, mx):
                s = _dot(k_ref[gki, pl.ds(pl.multiple_of(c * kc, kc), kc), :], q)
                return jnp.maximum(mx, jnp.max(s, axis=0, keepdims=True))

            m_new = lax.fori_loop(0, nc, cmax, m_old)

            def cacc(c, st):
                pv, l = st
                ks = pl.ds(pl.multiple_of(c * kc, kc), kc)
                p = jnp.exp2(_dot(k_ref[gki, ks, :], q) - m_new)
                return pv + _dot(v_ref[gvi, :, ks], p.astype(BF16)), l + jnp.sum(p, axis=0, keepdims=True)

            pv, l = lax.fori_loop(0, nc, cacc, (jnp.zeros((dv, tq), F32), jnp.zeros((1, tq), F32)))
            alpha = jnp.exp2(m_old - m_new)
            m_sc[nxt, g] = m_new
            acc_sc[nxt, g, 0:dv, :] = acc_sc[cur, g, 0:dv, :] * alpha + pv
            acc_sc[nxt, g, dv:dv + 1, :] = acc_sc[cur, g, dv:dv + 1, :] * alpha + l
            return carry

        lax.fori_loop(0, heads, head, 0)

    @pl.when(j == pl.num_programs(2) - 1)
    def _():
        for g in range(heads):
            o_ref[g] = (acc_sc[nxt, g, 0:dv, :] / acc_sc[nxt, g, dv:dv + 1, :]).astype(o_ref.dtype)


def _stream_attn(q_t, k, v_t, s, n, *, tq, tk, kc, ahead, name):
    b, hq = q_t.shape[0], q_t.shape[1]
    hk = k.shape[1]
    hv, dv = v_t.shape[1], v_t.shape[2]
    return pl.pallas_call(
        functools.partial(_stream_kernel, heads=hq, kc=kc, ahead=ahead),
        grid=(b, s // tq, n // tk),
        in_specs=[
            pl.BlockSpec((None, hq, HEAD_PAD, tq), lambda i, qi, j: (i, 0, 0, qi)),
            pl.BlockSpec((None, hk, tk, HEAD_PAD), lambda i, qi, j: (i, 0, j, 0)),
            pl.BlockSpec((None, hv, dv, tk), lambda i, qi, j: (i, 0, 0, j)),
        ],
        out_specs=pl.BlockSpec((None, hq, dv, tq), lambda i, qi, j: (i, 0, 0, qi)),
        out_shape=jax.ShapeDtypeStruct((b, hq, dv, s), BF16),
        scratch_shapes=[
            pltpu.VMEM((2, hq, 1, tq), F32),
            pltpu.VMEM((2, hq, dv + STAT_ROWS, tq), F32),
            pltpu.VMEM((hq, dv, tq), F32),
        ],
        compiler_params=_cparams(("parallel", "parallel", "arbitrary")),
        name=name,
    )(q_t, k, v_t)


def _ctx_attn_kernel(q_ref, k_ref, v_ref, o_ref):
    heads, gk, gv = q_ref.shape[0], k_ref.shape[0], v_ref.shape[0]
    for g in range(heads):
        s = _dot(k_ref[g * gk // heads], q_ref[g])
        p = jnp.exp2(s - jnp.max(s, axis=0, keepdims=True))
        den = jnp.sum(p, axis=0, keepdims=True)
        o_ref[g] = (_dot(v_ref[g * gv // heads], p.astype(BF16)) / den).astype(o_ref.dtype)


def _ctx_attn(q_t, k, v_t, s, lc, name):
    b, hq = q_t.shape[0], q_t.shape[1]
    hk = k.shape[1]
    hv, dv = v_t.shape[1], v_t.shape[2]
    off = s // lc
    return pl.pallas_call(
        _ctx_attn_kernel,
        grid=(b,),
        in_specs=[
            pl.BlockSpec((None, hq, HEAD_PAD, lc), lambda i: (i, 0, 0, off)),
            pl.BlockSpec((None, hk, lc, HEAD_PAD), lambda i: (i, 0, off, 0)),
            pl.BlockSpec((None, hv, dv, lc), lambda i: (i, 0, 0, off)),
        ],
        out_specs=pl.BlockSpec((None, hq, dv, lc), lambda i: (i, 0, 0, 0)),
        out_shape=jax.ShapeDtypeStruct((b, hq, dv, lc), BF16),
        compiler_params=_cparams(("parallel",)),
        name=name,
    )(q_t, k, v_t)


def _out_ab_kernel(x_ref, oa_ref, ob_ref, sg_ref, gate_ref, w_ref, o_ref):
    m = jnp.concatenate([oa_ref[...], ob_ref[...]], axis=0) * sg_ref[...]
    y_t = _dot(w_ref[...], m)
    o_ref[...] = x_ref[...] + gate_ref[...] * y_t.T


def _out_ab(x, oa_t, ob_t, sg_t, sg_off, gate, w_out_t, ts):
    b, n, d = x.shape
    half = oa_t.shape[1]
    return pl.pallas_call(
        _out_ab_kernel,
        grid=(b, n // ts),
        in_specs=[
            pl.BlockSpec((None, ts, d), lambda i, t: (i, t, 0)),
            pl.BlockSpec((None, half, ts), lambda i, t: (i, 0, t)),
            pl.BlockSpec((None, d - half, ts), lambda i, t: (i, 0, t)),
            pl.BlockSpec((None, d, ts), lambda i, t: (i, 0, t + sg_off)),
            pl.BlockSpec((None, 1, d), lambda i, t: (i, 0, 0)),
            pl.BlockSpec((d, d), lambda i, t: (0, 0)),
        ],
        out_specs=pl.BlockSpec((None, ts, d), lambda i, t: (i, t, 0)),
        out_shape=jax.ShapeDtypeStruct((b, n, d), F32),
        compiler_params=_cparams(("parallel", "parallel")),
        name="out_ab",
    )(x, oa_t, ob_t, sg_t, gate, w_out_t)


def _proj_cd_kernel(x_ref, c_ref, mod_ref, w_in_ref, qg_ref, kg_ref, c64_ref, s64_ref,
                    q_ref, k_ref, v_ref, gb_ref, u_ref, sg_ref, *, nx):
    scale = WIN_DIM ** -0.5 * LOG2E
    z64 = jnp.zeros((WIN_DIM, PROJ_PART), BF16)
    grp = WIN_HEADS // WIN_KV_HEADS

    def tail(pp, cols):
        c64 = c64_ref[:, cols]
        s64 = s64_ref[:, cols]
        qg = qg_ref[...]
        kg = kg_ref[...]
        for hd in range(WIN_HEADS):
            qh = _rms_cols(pp[hd * WIN_DIM:(hd + 1) * WIN_DIM], qg, WIN_DIM)
            qh = (_rope_cols(qh, c64, s64, WIN_DIM // 4) * scale).astype(BF16)
            for kvh in range(WIN_KV_HEADS):
                q_ref[hd, kvh * WIN_DIM:(kvh + 1) * WIN_DIM, cols] = qh if kvh == hd // grp else z64
        k_heads = []
        for kvh in range(WIN_KV_HEADS):
            kh = _rms_cols(pp[512 + kvh * WIN_DIM:512 + (kvh + 1) * WIN_DIM], kg, WIN_DIM)
            k_heads.append(_rope_cols(kh, c64, s64, WIN_DIM // 4))
            v_ref[kvh, :, cols] = pp[640 + kvh * WIN_DIM:640 + (kvh + 1) * WIN_DIM].astype(BF16)
        k_ref[cols, :] = jnp.concatenate(k_heads, axis=0).T.astype(BF16)
        gb_ref[:, cols] = pp[768:1280].astype(BF16)
        u_ref[:, cols] = (pp[1280:1792] * pp[1792:2304]).astype(BF16)
        sg_ref[:, cols] = _silu(pp[2304:3328]).astype(BF16)

    _projected_parts(x_ref, c_ref, mod_ref, w_in_ref, nx, tail)


def _proj_cd(x, ctx, mods, w_in_t, gains, tabs, ts):
    b, s, d = x.shape
    lc = ctx.shape[1]
    nx = s // ts
    n = s + ts
    qg, kg = [_bcast_gain(g) for g in gains]
    c64, s64 = tabs
    full = lambda a: pl.BlockSpec(a.shape, lambda i, t: (0,) * a.ndim)
    tab = lambda a: pl.BlockSpec((a.shape[0], ts), lambda i, t: (0, t))
    feat3 = lambda r: (jax.ShapeDtypeStruct((b, r, n), BF16), pl.BlockSpec((None, r, ts), lambda i, t: (i, 0, t)))
    outs = [
        (jax.ShapeDtypeStruct((b, WIN_HEADS, HEAD_PAD, n), BF16),
         pl.BlockSpec((None, WIN_HEADS, HEAD_PAD, ts), lambda i, t: (i, 0, 0, t))),
        (jax.ShapeDtypeStruct((b, n, HEAD_PAD), BF16),
         pl.BlockSpec((None, ts, HEAD_PAD), lambda i, t: (i, t, 0))),
        (jax.ShapeDtypeStruct((b, WIN_KV_HEADS, WIN_DIM, n), BF16),
         pl.BlockSpec((None, WIN_KV_HEADS, WIN_DIM, ts), lambda i, t: (i, 0, 0, t))),
        feat3(CONV_CH),
        feat3(CONV_CH),
        feat3(D_MODEL),
    ]
    return pl.pallas_call(
        functools.partial(_proj_cd_kernel, nx=nx),
        grid=(b, nx + 1),
        in_specs=_joint_specs(nx, ts, lc, d) + [full(w_in_t), full(qg), full(kg), tab(c64), tab(s64)],
        out_specs=[o[1] for o in outs],
        out_shape=[o[0] for o in outs],
        compiler_params=_cparams(("parallel", "parallel")),
        name="proj_cd",
    )(x, ctx, mods, w_in_t, qg, kg, c64, s64)


def _win_kernel(q_ref, kp_ref, kc_ref, kn_ref, vp_ref, vcur_ref, vn_ref, kctx_ref, vctx_ref, bias_ref, sink_ref,
                o_ref, *, sub):
    heads, tq = q_ref.shape[0], q_ref.shape[2]
    gv = vcur_ref.shape[0]
    nsub = tq // sub
    qi = pl.program_id(1)
    edge_p = jnp.where(qi > 0, 0.0, NEG_INF)
    edge_n = jnp.where(qi < pl.num_programs(1) - 1, 0.0, NEG_INF)
    bias_mid = bias_ref[...]
    bias_first = jnp.concatenate([bias_ref[0:WINDOW, :] + edge_p, bias_ref[WINDOW:, :]], axis=0)
    bias_last = jnp.concatenate([bias_ref[0:sub + WINDOW, :], bias_ref[sub + WINDOW:, :] + edge_n], axis=0)
    if nsub == 1:
        bias_first = bias_last = jnp.concatenate(
            [bias_ref[0:WINDOW, :] + edge_p, bias_ref[WINDOW:sub + WINDOW, :], bias_ref[sub + WINDOW:, :] + edge_n],
            axis=0)
    k_loc = jnp.concatenate([kp_ref[...], kc_ref[...], kn_ref[...]], axis=0)
    v_loc = [jnp.concatenate([vp_ref[h], vcur_ref[h], vn_ref[h]], axis=1) for h in range(gv)]
    kctx = kctx_ref[...]
    units = [(g, t) for g in range(heads) for t in range(nsub)]
    pending = {}

    def emit_scores(u):
        g, t = u
        q = q_ref[g, :, t * sub:(t + 1) * sub]
        bias = bias_first if t == 0 else (bias_last if t == nsub - 1 else bias_mid)
        pending[u] = (_dot(k_loc[t * sub:(t + 1) * sub + 2 * WINDOW], q) + bias, _dot(kctx, q))

    emit_scores(units[0])
    for i, (g, t) in enumerate(units):
        if i + 1 < len(units):
            emit_scores(units[i + 1])
        s_l, s_x = pending.pop((g, t))
        sk = sink_ref[g] * LOG2E
        m = jnp.maximum(jnp.maximum(jnp.max(s_l, axis=0, keepdims=True), jnp.max(s_x, axis=0, keepdims=True)), sk)
        p_l = jnp.exp2(s_l - m)
        p_x = jnp.exp2(s_x - m)
        den = jnp.sum(p_l, axis=0, keepdims=True) + jnp.sum(p_x, axis=0, keepdims=True) + jnp.exp2(sk - m)
        h = g * gv // heads
        acc = (_dot(v_loc[h][:, t * sub:(t + 1) * sub + 2 * WINDOW], p_l.astype(BF16))
               + _dot(vctx_ref[h], p_x.astype(BF16)))
        o_ref[g, :, t * sub:(t + 1) * sub] = (acc / den).astype(o_ref.dtype)


def _win_attn(q_t, k, v_t, sink, s, lc, *, tq, sub):
    b, hq = q_t.shape[0], q_t.shape[1]
    hv, dv = v_t.shape[1], v_t.shape[2]
    r = tq // WINDOW
    nb = s // WINDOW
    prev = lambda qi: jnp.maximum(qi * r - 1, 0)
    nxt = lambda qi: jnp.minimum((qi + 1) * r, nb - 1)
    kpos = jnp.arange(sub + 2 * WINDOW, dtype=jnp.int32)[:, None] - WINDOW
    qpos = jnp.arange(sub, dtype=jnp.int32)[None, :]
    bias = jnp.where(jnp.abs(qpos - kpos) <= WINDOW, 0.0, NEG_INF).astype(F32)
    return pl.pallas_call(
        functools.partial(_win_kernel, sub=sub),
        grid=(b, s // tq),
        in_specs=[
            pl.BlockSpec((None, hq, HEAD_PAD, tq), lambda i, qi: (i, 0, 0, qi)),
            pl.BlockSpec((None, WINDOW, HEAD_PAD), lambda i, qi: (i, prev(qi), 0)),
            pl.BlockSpec((None, tq, HEAD_PAD), lambda i, qi: (i, qi, 0)),
            pl.BlockSpec((None, WINDOW, HEAD_PAD), lambda i, qi: (i, nxt(qi), 0)),
            pl.BlockSpec((None, hv, dv, WINDOW), lambda i, qi: (i, 0, 0, prev(qi))),
            pl.BlockSpec((None, hv, dv, tq), lambda i, qi: (i, 0, 0, qi)),
            pl.BlockSpec((None, hv, dv, WINDOW), lambda i, qi: (i, 0, 0, nxt(qi))),
            pl.BlockSpec((None, lc, HEAD_PAD), lambda i, qi: (i, s // lc, 0)),
            pl.BlockSpec((None, hv, dv, lc), lambda i, qi: (i, 0, 0, s // lc)),
            pl.BlockSpec(bias.shape, lambda i, qi: (0, 0)),
            pl.BlockSpec(memory_space=pltpu.SMEM),
        ],
        out_specs=pl.BlockSpec((None, hq, dv, tq), lambda i, qi: (i, 0, 0, qi)),
        out_shape=jax.ShapeDtypeStruct((b, hq, dv, s), BF16),
        compiler_params=_cparams(("parallel", "parallel")),
        name="win_attn",
    )(q_t, k, k, k, v_t, v_t, v_t, k, v_t, bias, sink)


def _out_cd_kernel(x_ref, oc_ref, gb_ref, up_ref, u_ref, un_ref, cw_ref, sg_ref, gate_ref, w_ref, o_ref, *, ts):
    t = pl.program_id(1)
    nt = pl.num_programs(1)
    u = u_ref[...].astype(F32)
    lane = lax.broadcasted_iota(jnp.int32, (CONV_CH, ts), 1)
    halo_l = pltpu.roll(up_ref[...].astype(F32), 1, axis=1)
    halo_l = jnp.where(t > 0, halo_l, 0.0)
    halo_r = pltpu.roll(un_ref[...].astype(F32), LANES - 1, axis=1)
    halo_r = jnp.where(t < nt - 1, halo_r, 0.0)
    reps = ts // LANES
    um = jnp.where(lane == 0, jnp.tile(halo_l, (1, reps)), pltpu.roll(u, 1, axis=1))
    up = jnp.where(lane == ts - 1, jnp.tile(halo_r, (1, reps)), pltpu.roll(u, ts - 1, axis=1))
    w0 = jnp.tile(cw_ref[0], (1, reps))
    w1 = jnp.tile(cw_ref[1], (1, reps))
    w2 = jnp.tile(cw_ref[2], (1, reps))
    od = gb_ref[...].astype(F32) * (w0 * um + w1 * u + w2 * up)
    sg = sg_ref[...]
    half = D_MODEL - CONV_CH
    m = jnp.concatenate([oc_ref[...] * sg[0:half], od.astype(BF16) * sg[half:]], axis=0)
    y_t = _dot(w_ref[...], m)
    o_ref[...] = x_ref[...] + gate_ref[...] * y_t.T


def _out_cd(x, oc_t, gb_t, u_t, conv_wb, sg_t, gate, w_out_t, ts):
    b, n, d = x.shape
    r = ts // LANES
    nb = n // LANES
    return pl.pallas_call(
        functools.partial(_out_cd_kernel, ts=ts),
        grid=(b, n // ts),
        in_specs=[
            pl.BlockSpec((None, ts, d), lambda i, t: (i, t, 0)),
            pl.BlockSpec((None, d - CONV_CH, ts), lambda i, t: (i, 0, t)),
            pl.BlockSpec((None, CONV_CH, ts), lambda i, t: (i, 0, t)),
            pl.BlockSpec((None, CONV_CH, LANES), lambda i, t: (i, 0, jnp.maximum(t * r - 1, 0))),
            pl.BlockSpec((None, CONV_CH, ts), lambda i, t: (i, 0, t)),
            pl.BlockSpec((None, CONV_CH, LANES), lambda i, t: (i, 0, jnp.minimum((t + 1) * r, nb - 1))),
            pl.BlockSpec((CONV_K, CONV_CH, LANES), lambda i, t: (0, 0, 0)),
            pl.BlockSpec((None, d, ts), lambda i, t: (i, 0, t)),
            pl.BlockSpec((None, 1, d), lambda i, t: (i, 0, 0)),
            pl.BlockSpec((d, d), lambda i, t: (0, 0)),
        ],
        out_specs=pl.BlockSpec((None, ts, d), lambda i, t: (i, t, 0)),
        out_shape=jax.ShapeDtypeStruct((b, n, d), F32),
        compiler_params=_cparams(("parallel", "parallel")),
        name="out_cd",
    )(x, oc_t, gb_t, u_t, u_t, u_t, conv_wb, sg_t, gate, w_out_t)


def _rope_tables(s, tail, dd):
    q = dd // 4
    rows = s // GRID_W
    row = jnp.repeat(jnp.arange(rows, dtype=F32), GRID_W)
    col = jnp.tile(jnp.arange(GRID_W, dtype=F32), rows)
    inv = ROPE_BASE ** (-jnp.arange(q, dtype=F32) / q)
    ar = inv[:, None] * row[None, :]
    ac = inv[:, None] * col[None, :]
    cos = jnp.concatenate([jnp.cos(ar), jnp.cos(ar), jnp.cos(ac), jnp.cos(ac)], axis=0)
    sin = jnp.concatenate([-jnp.sin(ar), jnp.sin(ar), -jnp.sin(ac), jnp.sin(ac)], axis=0)
    return (jnp.concatenate([cos, jnp.ones((dd, tail), F32)], axis=1),
            jnp.concatenate([sin, jnp.zeros((dd, tail), F32)], axis=1))


def _tile_rows(n, cap):
    t = cap
    while n % t:
        t //= 2
    return t


def _key_tile(n, cap, step):
    t = (cap // step) * step
    while n % t:
        t -= step
    return t


def kernel(x, c, ctx, c_ctx, mod_w, mod_b, ab_w_in, ab_w_out, mla_cq_gain, mla_ckv_gain, mla_w_uq, mla_w_ukv,
           mla_q_gain, mla_k_gain, gqa_q_gain, gqa_k_gain, cd_w_in, cd_w_out, win_q_gain, win_k_gain,
           win_sink, conv_w):
    b, s, d = x.shape
    lc = ctx.shape[1]
    ts_p = _tile_rows(s, 2 * PROJ_PART)
    assert ts_p % lc == 0 and ts_p % PROJ_PART == 0 and s % lc == 0
    ts_x = _tile_rows(s, 512)
    ts_c = _tile_rows(lc, 256)
    tq = _tile_rows(s, 512)
    n = s + lc
    tk = _key_tile(n, 3328, LANES)
    kc = 256 if tk % 256 == 0 else tk
    ahead = 2
    tq_w = _tile_rows(s, 512)

    crows = jnp.zeros((8, d), F32).at[0:b].set(c).at[b].set(c_ctx)
    mods = _modulation(crows, mod_w, mod_b)

    def split_mod(layer):
        m = mods[layer]
        shift, scale, gate = m[:, 0:d], m[:, d:2 * d], m[:, 2 * d:3 * d]
        lat = jnp.stack([shift[0:b], scale[0:b]], axis=1)
        con = jnp.broadcast_to(jnp.stack([shift[b], scale[b]], axis=0)[None], (b, 2, d))
        gate_x = gate[0:b][:, None, :]
        gate_c = jnp.broadcast_to(gate[b][None, None, :], (b, 1, d))
        return jnp.stack([lat, con], axis=1), gate_x, gate_c

    tabs32 = _rope_tables(s, ts_p, MLA_ROPE)
    tabs64 = _rope_tables(s, ts_p, GQA_DIM)

    mod4, gate_x, gate_c = split_mod(0)
    w_in_t = ab_w_in[0].T.astype(BF16)
    w_uq_t = mla_w_uq[0].T.astype(BF16)
    w_ukv_t = mla_w_ukv[0].T.astype(BF16)
    w_out_t = ab_w_out[0].T.astype(BF16)
    gains = (mla_cq_gain[0], mla_ckv_gain[0], mla_q_gain[0], mla_k_gain[0], gqa_q_gain[0], gqa_k_gain[0])
    qa, ka, va, qb, kb, vb, sg = _proj_ab(x, ctx, mod4, w_in_t, w_uq_t, w_ukv_t, gains, tabs32 + tabs64, ts_p)
    kb = kb[:, None]
    oa = _stream_attn(qa, ka, va, s, n, tq=tq, tk=tk, kc=kc, ahead=ahead, name="flash_mla")
    ob = _stream_attn(qb, kb, vb, s, n, tq=tq, tk=tk, kc=kc, ahead=ahead, name="flash_gqa")
    x1 = _out_ab(x, oa.reshape(b, -1, s), ob.reshape(b, -1, s), sg, 0, gate_x, w_out_t, ts_x)
    oac = _ctx_attn(qa, ka, va, s, lc, "ctx_mla")
    obc = _ctx_attn(qb, kb, vb, s, lc, "ctx_gqa")
    xc1 = _out_ab(ctx, oac.reshape(b, -1, lc), obc.reshape(b, -1, lc), sg, s // ts_c, gate_c, w_out_t, ts_c)

    mod4, gate_x, gate_c = split_mod(1)
    w_in_t = cd_w_in[0].T.astype(BF16)
    w_out_t = cd_w_out[0].T.astype(BF16)
    q, k, v, gb, u, sg = _proj_cd(x1, xc1, mod4, w_in_t, (win_q_gain[0], win_k_gain[0]), tabs64, ts_p)
    oc = _win_attn(q, k, v, win_sink[0].astype(F32), s, lc, tq=tq_w, sub=min(tq_w, 256))
    conv_wb = jnp.broadcast_to(conv_w[0].astype(F32)[:, :, None], (CONV_K, CONV_CH, LANES))
    return _out_cd(x1, oc.reshape(b, -1, s), gb, u, conv_wb, sg, gate_x, w_out_t, ts_x)
```

```python
import functools
import math

import jax
import jax.numpy as jnp
from jax import lax
from jax.experimental import pallas as pl
from jax.experimental.pallas import tpu as pltpu

D_MODEL = 1024
GRID_W = 64
ROPE_BASE = 10000.0
EPS = 1e-6
NEG_INF = -1e30
WINDOW = 128

MLA_HEADS = 8
MLA_NOPE = 64
MLA_ROPE = 32
MLA_V = 64
MLA_QK = MLA_NOPE + MLA_ROPE
MLA_Q_RANK = 256
MLA_KV_RANK = 256
GQA_HEADS = 8
GQA_KV_HEADS = 2
GQA_DIM = 64
WIN_HEADS = 8
WIN_KV_HEADS = 2
WIN_DIM = 64
CONV_CH = 512
CONV_K = 3

LANES = 128
HEAD_PAD = 128
STAT_ROWS = 8
PROJ_PART = 256
GATE_ROWS = 256
WIN_AHEAD = 2
LOG2E = math.log2(math.e)
VMEM_LIMIT_BYTES = 56 * 1024 * 1024

BF16 = jnp.bfloat16
F32 = jnp.float32


def _cparams(sem):
    return pltpu.CompilerParams(dimension_semantics=sem, vmem_limit_bytes=VMEM_LIMIT_BYTES)


def _dot(a, b):
    return jnp.dot(a, b, preferred_element_type=F32)


def _dot_nt(a, b):
    return lax.dot_general(a, b, (((1,), (1,)), ((), ())), preferred_element_type=F32)


def _silu(v):
    return v * (1.0 / (1.0 + jnp.exp(-v)))


def _mod_kernel(c_ref, w_ref, b_ref, o_ref):
    sc = _silu(c_ref[...])
    o_ref[...] = _dot(sc, w_ref[...]) + b_ref[...]


def _modulation(crows, mod_w, mod_b):
    depth, d, n = mod_w.shape
    tn = 1024
    return pl.pallas_call(
        _mod_kernel,
        grid=(depth, n // tn),
        in_specs=[
            pl.BlockSpec((8, d), lambda l, j: (0, 0)),
            pl.BlockSpec((None, d, tn), lambda l, j: (l, 0, j)),
            pl.BlockSpec((None, 1, tn), lambda l, j: (l, 0, j)),
        ],
        out_specs=pl.BlockSpec((None, 8, tn), lambda l, j: (l, 0, j)),
        out_shape=jax.ShapeDtypeStruct((depth, 8, n), F32),
        compiler_params=_cparams(("parallel", "parallel")),
        name="modulation",
    )(crows, mod_w, mod_b.reshape(depth, 1, n))


def _adaln_rows(x, shift, scale):
    ms = jnp.mean(x * x, axis=-1, keepdims=True)
    return (x * lax.rsqrt(ms + EPS)) * (1.0 + scale) + shift


def _rms_cols(v, gain_b, n):
    ss = jnp.sum(v * v, axis=0, keepdims=True)
    return v * lax.rsqrt(ss * (1.0 / n) + EPS) * gain_b


def _rope_cols(v, cos, sin, q):
    swap = jnp.concatenate([v[q:2 * q], v[0:q], v[3 * q:4 * q], v[2 * q:3 * q]], axis=0)
    return v * cos + swap * sin


def _projected_parts(x_ref, c_ref, mod_ref, w_in_ref, nx, groups):
    ts = x_ref.shape[0]
    ctx_rows = jnp.concatenate([c_ref[...]] * (ts // c_ref.shape[0]), axis=0)
    rows = jnp.where(pl.program_id(1) < nx, x_ref[...], ctx_rows)
    shift, scale = mod_ref[0:1, :], mod_ref[1:2, :]
    parts = ts // PROJ_PART
    cols = lambda i: slice(i * PROJ_PART, (i + 1) * PROJ_PART)

    def normed(i):
        return _adaln_rows(rows[cols(i)], shift, scale).astype(BF16)

    def project(h, g):
        r0, r1, _ = groups[g]
        return _dot_nt(w_in_ref[r0:r1, :], h)

    h = normed(0)
    pend = [project(h, g) for g in range(len(groups))]
    for i in range(parts):
        nxt = []
        if i + 1 < parts:
            h = normed(i + 1)
        for g in range(len(groups)):
            if i + 1 < parts:
                nxt.append(project(h, g))
            groups[g][2](pend[g], cols(i))
        pend = nxt


def _joint_specs(nx, ts, lc, d):
    return [
        pl.BlockSpec((None, ts, d), lambda i, t: (i, jnp.minimum(t, nx - 1), 0)),
        pl.BlockSpec((None, lc, d), lambda i, t: (i, 0, 0)),
        pl.BlockSpec((None, None, 2, d), lambda i, t: (i, jnp.where(t < nx, 0, 1), 0, 0)),
    ]


def _bcast_gain(g):
    return jnp.broadcast_to(g.astype(F32)[:, None], (g.shape[0], PROJ_PART))


def _proj_ab_kernel(x_ref, c_ref, mod_ref, w_in_ref, cqg_ref, ckvg_ref, w_uq_ref, w_ukv_ref,
                    qg_ref, kg_ref, gqg_ref, gkg_ref, c32_ref, s32_ref, c64_ref, s64_ref,
                    qa_ref, ka_ref, va_ref, qb_ref, kb_ref, vb_ref, sg_ref, *, nx):
    zpad = jnp.zeros((HEAD_PAD - MLA_QK, PROJ_PART), F32)
    z64 = jnp.zeros((GQA_DIM, PROJ_PART), BF16)
    q_scale = MLA_QK ** -0.5 * LOG2E
    g_scale = GQA_DIM ** -0.5 * LOG2E
    grp = GQA_HEADS // GQA_KV_HEADS

    def tail_mla(pp, cols):
        cq = pp[0:256]
        ckv = pp[256:512]
        kr = pp[512:544]
        cqn = _rms_cols(cq, cqg_ref[...], MLA_Q_RANK).astype(BF16)
        ckvn = _rms_cols(ckv, ckvg_ref[...], MLA_KV_RANK).astype(BF16)
        qa = _dot(w_uq_ref[...], cqn)
        kv = _dot(w_ukv_ref[...], ckvn)
        qg = qg_ref[...]
        kg = kg_ref[...]
        c32 = c32_ref[:, cols]
        s32 = s32_ref[:, cols]
        kr_ss = jnp.sum(kr * kr, axis=0, keepdims=True)
        for hd in range(MLA_HEADS):
            qh = _rms_cols(qa[hd * MLA_QK:(hd + 1) * MLA_QK], qg, MLA_QK)
            qr = _rope_cols(qh[MLA_NOPE:], c32, s32, MLA_ROPE // 4)
            qa_ref[hd, 0:MLA_NOPE, cols] = (qh[0:MLA_NOPE] * q_scale).astype(BF16)
            qa_ref[hd, MLA_NOPE:MLA_QK, cols] = (qr * q_scale).astype(BF16)
            qa_ref[hd, MLA_QK:HEAD_PAD, cols] = zpad.astype(BF16)
            kn = kv[hd * 128:hd * 128 + MLA_NOPE]
            ss = jnp.sum(kn * kn, axis=0, keepdims=True) + kr_ss
            rs = lax.rsqrt(ss * (1.0 / MLA_QK) + EPS)
            krn = _rope_cols(kr * rs * kg[MLA_NOPE:], c32, s32, MLA_ROPE // 4)
            k_cols = jnp.concatenate([kn * rs * kg[0:MLA_NOPE], krn, zpad], axis=0)
            ka_ref[hd, cols, :] = k_cols.T.astype(BF16)
            va_ref[hd, :, cols] = kv[hd * 128 + MLA_NOPE:(hd + 1) * 128].astype(BF16)

    def tail_gqa(pp, cols):
        gq = pp[0:512]
        gk = pp[512:640]
        gv = pp[640:768]
        c64 = c64_ref[:, cols]
        s64 = s64_ref[:, cols]
        gqg = gqg_ref[...]
        gkg = gkg_ref[...]
        for hd in range(GQA_HEADS):
            qh = _rms_cols(gq[hd * GQA_DIM:(hd + 1) * GQA_DIM], gqg, GQA_DIM)
            qh = (_rope_cols(qh, c64, s64, GQA_DIM // 4) * g_scale).astype(BF16)
            for kvh in range(GQA_KV_HEADS):
                qb_ref[hd, kvh * GQA_DIM:(kvh + 1) * GQA_DIM, cols] = qh if kvh == hd // grp else z64
        k_heads = []
        for kvh in range(GQA_KV_HEADS):
            kh = _rms_cols(gk[kvh * GQA_DIM:(kvh + 1) * GQA_DIM], gkg, GQA_DIM)
            k_heads.append(_rope_cols(kh, c64, s64, GQA_DIM // 4))
            vb_ref[kvh, :, cols] = gv[kvh * GQA_DIM:(kvh + 1) * GQA_DIM].astype(BF16)
        kb_ref[cols, :] = jnp.concatenate(k_heads, axis=0).T.astype(BF16)

    def tail_gate(r0):
        def tail(pp, cols):
            sg_ref[r0:r0 + GATE_ROWS, cols] = _silu(pp).astype(BF16)
        return tail

    g0 = 544 + 768
    groups = [(0, 544, tail_mla), (544, g0, tail_gqa)]
    groups += [(g0 + r, g0 + r + GATE_ROWS, tail_gate(r)) for r in range(0, D_MODEL, GATE_ROWS)]
    _projected_parts(x_ref, c_ref, mod_ref, w_in_ref, nx, groups)


def _proj_ab(x, ctx, mods, w_in_t, w_uq_t, w_ukv_t, gains, tabs, ts):
    b, s, d = x.shape
    lc = ctx.shape[1]
    nx = s // ts
    n = s + ts
    cqg, ckvg, qg, kg, gqg, gkg = [_bcast_gain(g) for g in gains]
    c32, s32, c64, s64 = tabs
    full = lambda a: pl.BlockSpec(a.shape, lambda i, t: (0,) * a.ndim)
    tab = lambda a: pl.BlockSpec((a.shape[0], ts), lambda i, t: (0, t))
    feat4 = lambda h, r: (jax.ShapeDtypeStruct((b, h, r, n), BF16),
                          pl.BlockSpec((None, h, r, ts), lambda i, t: (i, 0, 0, t)))
    rows4 = lambda h: (jax.ShapeDtypeStruct((b, h, n, HEAD_PAD), BF16),
                       pl.BlockSpec((None, h, ts, HEAD_PAD), lambda i, t: (i, 0, t, 0)))
    outs = [
        feat4(MLA_HEADS, HEAD_PAD),
        rows4(MLA_HEADS),
        feat4(MLA_HEADS, MLA_V),
        feat4(GQA_HEADS, HEAD_PAD),
        (jax.ShapeDtypeStruct((b, n, HEAD_PAD), BF16),
         pl.BlockSpec((None, ts, HEAD_PAD), lambda i, t: (i, t, 0))),
        feat4(GQA_KV_HEADS, GQA_DIM),
        (jax.ShapeDtypeStruct((b, D_MODEL, n), BF16),
         pl.BlockSpec((None, D_MODEL, ts), lambda i, t: (i, 0, t))),
    ]
    return pl.pallas_call(
        functools.partial(_proj_ab_kernel, nx=nx),
        grid=(b, nx + 1),
        in_specs=_joint_specs(nx, ts, lc, d) + [
            full(w_in_t), full(cqg), full(ckvg), full(w_uq_t), full(w_ukv_t),
            full(qg), full(kg), full(gqg), full(gkg),
            tab(c32), tab(s32), tab(c64), tab(s64),
        ],
        out_specs=[o[1] for o in outs],
        out_shape=[o[0] for o in outs],
        compiler_params=_cparams(("parallel", "parallel")),
        name="proj_ab",
    )(x, ctx, mods, w_in_t, cqg, ckvg, w_uq_t, w_ukv_t, qg, kg, gqg, gkg, c32, s32, c64, s64)


P_LIMIT = 2.0 ** 100
FIRST_REF_KEYS = 128


def _stream_kernel(q_ref, k_ref, v_ref, o_ref, m_sc, acc_sc, pv_sc, *, heads, kc, ahead):
    j = pl.program_id(2)
    tk = k_ref.shape[1]
    tq = q_ref.shape[2]
    dv = o_ref.shape[1]
    nc = tk // kc
    gk = k_ref.shape[0]
    gv = v_ref.shape[0]
    cur = lax.rem(j, 2)
    nxt = 1 - cur

    @pl.when(j == 0)
    def _():
        for g in range(heads):
            s0 = _dot(k_ref[g * gk // heads, 0:FIRST_REF_KEYS, :], q_ref[g])
            m_sc[0, g] = jnp.max(s0, axis=0, keepdims=True)
        acc_sc[0] = jnp.zeros(acc_sc.shape[1:], F32)

    units = [(g, c) for g in range(heads) for c in range(nc)]
    pending = {}

    def emit_scores(u):
        g, c = u
        pending[u] = _dot(k_ref[g * gk // heads, c * kc:(c + 1) * kc, :], q_ref[g])

    worst = None
    issued = 0
    pmx = None
    lsum = None
    for i, (g, c) in enumerate(units):
        while issued <= min(i + ahead, len(units) - 1):
            emit_scores(units[issued])
            issued += 1
        s = pending.pop((g, c))
        m_ref = m_sc[cur, g]
        mb = jnp.broadcast_to(m_ref, (8, tq))
        p32 = jnp.exp2(s.reshape(kc // 8, 8, tq) - mb[None])
        lpart = jnp.sum(p32, axis=0)
        lsum = lpart if c == 0 else lsum + lpart
        p = p32.reshape(kc, tq).astype(BF16)
        part = jnp.max(p.reshape(kc // 16, 16, tq), axis=0)
        pmx = part if c == 0 else jnp.maximum(pmx, part)
        d = _dot(v_ref[g * gv // heads, :, c * kc:(c + 1) * kc], p)
        if c == 0:
            pv_sc[g] = d
        else:
            pv_sc[g] += d
        if c == nc - 1:
            pm = jnp.maximum(jnp.max(pmx.astype(F32), axis=0, keepdims=True), 1.0)
            worst = pm if worst is None else jnp.maximum(worst, pm)
            m_sc[nxt, g] = m_ref + jnp.log2(pm)
            rp = 1.0 / pm
            acc_sc[nxt, g, 0:dv, :] = (acc_sc[cur, g, 0:dv, :] + pv_sc[g]) * rp
            acc_sc[nxt, g, dv:dv + 1, :] = (acc_sc[cur, g, dv:dv + 1, :] + jnp.sum(lsum, axis=0, keepdims=True)) * rp

    redo = jnp.logical_not(jnp.max(worst) <= P_LIMIT)

    @pl.when(redo)
    def _():
        def head(g, carry):
            q = q_ref[g]
            gki = g * gk // heads
            gvi = g * gv // heads
            m_old = m_sc[cur, g]

            def cmax(c, mx):
                s = _dot(k_ref[gki, pl.ds(pl.multiple_of(c * kc, kc), kc), :], q)
                return jnp.maximum(mx, jnp.max(s, axis=0, keepdims=True))

            m_new = lax.fori_loop(0, nc, cmax, m_old)

            def cacc(c, st):
                pv, l = st
                ks = pl.ds(pl.multiple_of(c * kc, kc), kc)
                p = jnp.exp2(_dot(k_ref[gki, ks, :], q) - m_new)
                return pv + _dot(v_ref[gvi, :, ks], p.astype(BF16)), l + jnp.sum(p, axis=0, keepdims=True)

            pv, l = lax.fori_loop(0, nc, cacc, (jnp.zeros((dv, tq), F32), jnp.zeros((1, tq), F32)))
            alpha = jnp.exp2(m_old - m_new)
            m_sc[nxt, g] = m_new
            acc_sc[nxt, g, 0:dv, :] = acc_sc[cur, g, 0:dv, :] * alpha + pv
            acc_sc[nxt, g, dv:dv + 1, :] = acc_sc[cur, g, dv:dv + 1, :] * alpha + l
            return carry

        lax.fori_loop(0, heads, head, 0)

    @pl.when(j == pl.num_programs(2) - 1)
    def _():
        for g in range(heads):
            o_ref[g] = (acc_sc[nxt, g, 0:dv, :] / acc_sc[nxt, g, dv:dv + 1, :]).astype(o_ref.dtype)


def _stream_attn(q_t, k, v_t, s, n, *, tq, tk, kc, ahead, name):
    b, hq = q_t.shape[0], q_t.shape[1]
    hk = k.shape[1]
    hv, dv = v_t.shape[1], v_t.shape[2]
    return pl.pallas_call(
        functools.partial(_stream_kernel, heads=hq, kc=kc, ahead=ahead),
        grid=(b, s // tq, n // tk),
        in_specs=[
            pl.BlockSpec((None, hq, HEAD_PAD, tq), lambda i, qi, j: (i, 0, 0, qi)),
            pl.BlockSpec((None, hk, tk, HEAD_PAD), lambda i, qi, j: (i, 0, j, 0)),
            pl.BlockSpec((None, hv, dv, tk), lambda i, qi, j: (i, 0, 0, j)),
        ],
        out_specs=pl.BlockSpec((None, hq, dv, tq), lambda i, qi, j: (i, 0, 0, qi)),
        out_shape=jax.ShapeDtypeStruct((b, hq, dv, s), BF16),
        scratch_shapes=[
            pltpu.VMEM((2, hq, 1, tq), F32),
            pltpu.VMEM((2, hq, dv + STAT_ROWS, tq), F32),
            pltpu.VMEM((hq, dv, tq), F32),
        ],
        compiler_params=_cparams(("parallel", "parallel", "arbitrary")),
        name=name,
    )(q_t, k, v_t)


def _ctx_attn_kernel(q_ref, k_ref, v_ref, o_ref):
    heads, gk, gv = q_ref.shape[0], k_ref.shape[0], v_ref.shape[0]
    for g in range(heads):
        s = _dot(k_ref[g * gk // heads], q_ref[g])
        p = jnp.exp2(s - jnp.max(s, axis=0, keepdims=True))
        den = jnp.sum(p, axis=0, keepdims=True)
        o_ref[g] = (_dot(v_ref[g * gv // heads], p.astype(BF16)) / den).astype(o_ref.dtype)


def _ctx_attn(q_t, k, v_t, s, lc, name):
    b, hq = q_t.shape[0], q_t.shape[1]
    hk = k.shape[1]
    hv, dv = v_t.shape[1], v_t.shape[2]
    off = s // lc
    return pl.pallas_call(
        _ctx_attn_kernel,
        grid=(b,),
        in_specs=[
            pl.BlockSpec((None, hq, HEAD_PAD, lc), lambda i: (i, 0, 0, off)),
            pl.BlockSpec((None, hk, lc, HEAD_PAD), lambda i: (i, 0, off, 0)),
            pl.BlockSpec((None, hv, dv, lc), lambda i: (i, 0, 0, off)),
        ],
        out_specs=pl.BlockSpec((None, hq, dv, lc), lambda i: (i, 0, 0, 0)),
        out_shape=jax.ShapeDtypeStruct((b, hq, dv, lc), BF16),
        compiler_params=_cparams(("parallel",)),
        name=name,
    )(q_t, k, v_t)


def _out_ab_kernel(x_ref, oa_ref, ob_ref, sg_ref, gate_ref, w_ref, o_ref):
    m = jnp.concatenate([oa_ref[...], ob_ref[...]], axis=0) * sg_ref[...]
    y_t = _dot(w_ref[...], m)
    o_ref[...] = x_ref[...] + gate_ref[...] * y_t.T


def _out_ab(x, oa_t, ob_t, sg_t, sg_off, gate, w_out_t, ts):
    b, n, d = x.shape
    half = oa_t.shape[1]
    return pl.pallas_call(
        _out_ab_kernel,
        grid=(b, n // ts),
        in_specs=[
            pl.BlockSpec((None, ts, d), lambda i, t: (i, t, 0)),
            pl.BlockSpec((None, half, ts), lambda i, t: (i, 0, t)),
            pl.BlockSpec((None, d - half, ts), lambda i, t: (i, 0, t)),
            pl.BlockSpec((None, d, ts), lambda i, t: (i, 0, t + sg_off)),
            pl.BlockSpec((None, 1, d), lambda i, t: (i, 0, 0)),
            pl.BlockSpec((d, d), lambda i, t: (0, 0)),
        ],
        out_specs=pl.BlockSpec((None, ts, d), lambda i, t: (i, t, 0)),
        out_shape=jax.ShapeDtypeStruct((b, n, d), F32),
        compiler_params=_cparams(("parallel", "parallel")),
        name="out_ab",
    )(x, oa_t, ob_t, sg_t, gate, w_out_t)


def _proj_cd_kernel(x_ref, c_ref, mod_ref, w_in_ref, qg_ref, kg_ref, c64_ref, s64_ref,
                    q_ref, k_ref, v_ref, gb_ref, u_ref, sg_ref, *, nx):
    scale = WIN_DIM ** -0.5 * LOG2E
    z64 = jnp.zeros((WIN_DIM, PROJ_PART), BF16)
    grp = WIN_HEADS // WIN_KV_HEADS

    def tail_q(pp, cols):
        c64 = c64_ref[:, cols]
        s64 = s64_ref[:, cols]
        qg = qg_ref[...]
        for hd in range(WIN_HEADS):
            qh = _rms_cols(pp[hd * WIN_DIM:(hd + 1) * WIN_DIM], qg, WIN_DIM)
            qh = (_rope_cols(qh, c64, s64, WIN_DIM // 4) * scale).astype(BF16)
            for kvh in range(WIN_KV_HEADS):
                q_ref[hd, kvh * WIN_DIM:(kvh + 1) * WIN_DIM, cols] = qh if kvh == hd // grp else z64

    def tail_kv(pp, cols):
        c64 = c64_ref[:, cols]
        s64 = s64_ref[:, cols]
        kg = kg_ref[...]
        k_heads = []
        for kvh in range(WIN_KV_HEADS):
            kh = _rms_cols(pp[kvh * WIN_DIM:(kvh + 1) * WIN_DIM], kg, WIN_DIM)
            k_heads.append(_rope_cols(kh, c64, s64, WIN_DIM // 4))
            v_ref[kvh, :, cols] = pp[128 + kvh * WIN_DIM:128 + (kvh + 1) * WIN_DIM].astype(BF16)
        k_ref[cols, :] = jnp.concatenate(k_heads, axis=0).T.astype(BF16)
        gb_ref[:, cols] = pp[256:768].astype(BF16)

    def tail_conv(pp, cols):
        u_ref[:, cols] = (pp[0:CONV_CH] * pp[CONV_CH:2 * CONV_CH]).astype(BF16)

    def tail_gate(r0):
        def tail(pp, cols):
            sg_ref[r0:r0 + GATE_ROWS, cols] = _silu(pp).astype(BF16)
        return tail

    g0 = 2304
    groups = [(0, 512, tail_q), (512, 1280, tail_kv), (1280, g0, tail_conv)]
    groups += [(g0 + r, g0 + r + GATE_ROWS, tail_gate(r)) for r in range(0, D_MODEL, GATE_ROWS)]
    _projected_parts(x_ref, c_ref, mod_ref, w_in_ref, nx, groups)


def _proj_cd(x, ctx, mods, w_in_t, gains, tabs, ts):
    b, s, d = x.shape
    lc = ctx.shape[1]
    nx = s // ts
    n = s + ts
    qg, kg = [_bcast_gain(g) for g in gains]
    c64, s64 = tabs
    full = lambda a: pl.BlockSpec(a.shape, lambda i, t: (0,) * a.ndim)
    tab = lambda a: pl.BlockSpec((a.shape[0], ts), lambda i, t: (0, t))
    feat3 = lambda r: (jax.ShapeDtypeStruct((b, r, n), BF16), pl.BlockSpec((None, r, ts), lambda i, t: (i, 0, t)))
    outs = [
        (jax.ShapeDtypeStruct((b, WIN_HEADS, HEAD_PAD, n), BF16),
         pl.BlockSpec((None, WIN_HEADS, HEAD_PAD, ts), lambda i, t: (i, 0, 0, t))),
        (jax.ShapeDtypeStruct((b, n, HEAD_PAD), BF16),
         pl.BlockSpec((None, ts, HEAD_PAD), lambda i, t: (i, t, 0))),
        (jax.ShapeDtypeStruct((b, WIN_KV_HEADS, WIN_DIM, n), BF16),
         pl.BlockSpec((None, WIN_KV_HEADS, WIN_DIM, ts), lambda i, t: (i, 0, 0, t))),
        feat3(CONV_CH),
        feat3(CONV_CH),
        feat3(D_MODEL),
    ]
    return pl.pallas_call(
        functools.partial(_proj_cd_kernel, nx=nx),
        grid=(b, nx + 1),
        in_specs=_joint_specs(nx, ts, lc, d) + [full(w_in_t), full(qg), full(kg), tab(c64), tab(s64)],
        out_specs=[o[1] for o in outs],
        out_shape=[o[0] for o in outs],
        compiler_params=_cparams(("parallel", "parallel")),
        name="proj_cd",
    )(x, ctx, mods, w_in_t, qg, kg, c64, s64)


def _win_kernel(q_ref, kp_ref, kc_ref, kn_ref, vp_ref, vcur_ref, vn_ref, kctx_ref, vctx_ref, bias_ref, sink_ref,
                o_ref, *, sub):
    heads, tq = q_ref.shape[0], q_ref.shape[2]
    gv = vcur_ref.shape[0]
    nsub = tq // sub
    qi = pl.program_id(1)
    edge_p = jnp.where(qi > 0, 0.0, NEG_INF)
    edge_n = jnp.where(qi < pl.num_programs(1) - 1, 0.0, NEG_INF)
    bias_mid = bias_ref[...]
    bias_first = jnp.concatenate([bias_ref[0:WINDOW, :] + edge_p, bias_ref[WINDOW:, :]], axis=0)
    bias_last = jnp.concatenate([bias_ref[0:sub + WINDOW, :], bias_ref[sub + WINDOW:, :] + edge_n], axis=0)
    if nsub == 1:
        bias_first = bias_last = jnp.concatenate(
            [bias_ref[0:WINDOW, :] + edge_p, bias_ref[WINDOW:sub + WINDOW, :], bias_ref[sub + WINDOW:, :] + edge_n],
            axis=0)
    k_loc = jnp.concatenate([kp_ref[...], kc_ref[...], kn_ref[...]], axis=0)
    v_loc = [jnp.concatenate([vp_ref[h], vcur_ref[h], vn_ref[h]], axis=1) for h in range(gv)]
    kctx = kctx_ref[...]
    units = [(g, t) for g in range(heads) for t in range(nsub)]
    pending = {}

    def emit_scores(u):
        g, t = u
        q = q_ref[g, :, t * sub:(t + 1) * sub]
        bias = bias_first if t == 0 else (bias_last if t == nsub - 1 else bias_mid)
        pending[u] = (_dot(k_loc[t * sub:(t + 1) * sub + 2 * WINDOW], q) + bias, _dot(kctx, q))

    issued = 0
    for i, (g, t) in enumerate(units):
        while issued <= min(i + WIN_AHEAD, len(units) - 1):
            emit_scores(units[issued])
            issued += 1
        s_l, s_x = pending.pop((g, t))
        sk = sink_ref[g] * LOG2E
        m = jnp.maximum(jnp.maximum(jnp.max(s_l, axis=0, keepdims=True), jnp.max(s_x, axis=0, keepdims=True)), sk)
        p_l = jnp.exp2(s_l - m)
        p_x = jnp.exp2(s_x - m)
        den = jnp.sum(p_l, axis=0, keepdims=True) + jnp.sum(p_x, axis=0, keepdims=True) + jnp.exp2(sk - m)
        h = g * gv // heads
        acc = (_dot(v_loc[h][:, t * sub:(t + 1) * sub + 2 * WINDOW], p_l.astype(BF16))
               + _dot(vctx_ref[h], p_x.astype(BF16)))
        o_ref[g, :, t * sub:(t + 1) * sub] = (acc / den).astype(o_ref.dtype)


def _win_attn(q_t, k, v_t, sink, s, lc, *, tq, sub):
    b, hq = q_t.shape[0], q_t.shape[1]
    hv, dv = v_t.shape[1], v_t.shape[2]
    r = tq // WINDOW
    nb = s // WINDOW
    prev = lambda qi: jnp.maximum(qi * r - 1, 0)
    nxt = lambda qi: jnp.minimum((qi + 1) * r, nb - 1)
    kpos = jnp.arange(sub + 2 * WINDOW, dtype=jnp.int32)[:, None] - WINDOW
    qpos = jnp.arange(sub, dtype=jnp.int32)[None, :]
    bias = jnp.where(jnp.abs(qpos - kpos) <= WINDOW, 0.0, NEG_INF).astype(F32)
    return pl.pallas_call(
        functools.partial(_win_kernel, sub=sub),
        grid=(b, s // tq),
        in_specs=[
            pl.BlockSpec((None, hq, HEAD_PAD, tq), lambda i, qi: (i, 0, 0, qi)),
            pl.BlockSpec((None, WINDOW, HEAD_PAD), lambda i, qi: (i, prev(qi), 0)),
            pl.BlockSpec((None, tq, HEAD_PAD), lambda i, qi: (i, qi, 0)),
            pl.BlockSpec((None, WINDOW, HEAD_PAD), lambda i, qi: (i, nxt(qi), 0)),
            pl.BlockSpec((None, hv, dv, WINDOW), lambda i, qi: (i, 0, 0, prev(qi))),
            pl.BlockSpec((None, hv, dv, tq), lambda i, qi: (i, 0, 0, qi)),
            pl.BlockSpec((None, hv, dv, WINDOW), lambda i, qi: (i, 0, 0, nxt(qi))),
            pl.BlockSpec((None, lc, HEAD_PAD), lambda i, qi: (i, s // lc, 0)),
            pl.BlockSpec((None, hv, dv, lc), lambda i, qi: (i, 0, 0, s // lc)),
            pl.BlockSpec(bias.shape, lambda i, qi: (0, 0)),
            pl.BlockSpec(memory_space=pltpu.SMEM),
        ],
        out_specs=pl.BlockSpec((None, hq, dv, tq), lambda i, qi: (i, 0, 0, qi)),
        out_shape=jax.ShapeDtypeStruct((b, hq, dv, s), BF16),
        compiler_params=_cparams(("parallel", "parallel")),
        name="win_attn",
    )(q_t, k, k, k, v_t, v_t, v_t, k, v_t, bias, sink)


def _out_cd_kernel(x_ref, oc_ref, gb_ref, up_ref, u_ref, un_ref, cw_ref, sg_ref, gate_ref, w_ref, o_ref, *, ts):
    t = pl.program_id(1)
    nt = pl.num_programs(1)
    u = u_ref[...].astype(F32)
    lane = lax.broadcasted_iota(jnp.int32, (CONV_CH, ts), 1)
    halo_l = pltpu.roll(up_ref[...].astype(F32), 1, axis=1)
    halo_l = jnp.where(t > 0, halo_l, 0.0)
    halo_r = pltpu.roll(un_ref[...].astype(F32), LANES - 1, axis=1)
    halo_r = jnp.where(t < nt - 1, halo_r, 0.0)
    reps = ts // LANES
    um = jnp.where(lane == 0, jnp.tile(halo_l, (1, reps)), pltpu.roll(u, 1, axis=1))
    up = jnp.where(lane == ts - 1, jnp.tile(halo_r, (1, reps)), pltpu.roll(u, ts - 1, axis=1))
    w0 = jnp.tile(cw_ref[0], (1, reps))
    w1 = jnp.tile(cw_ref[1], (1, reps))
    w2 = jnp.tile(cw_ref[2], (1, reps))
    od = gb_ref[...].astype(F32) * (w0 * um + w1 * u + w2 * up)
    sg = sg_ref[...]
    half = D_MODEL - CONV_CH
    m = jnp.concatenate([oc_ref[...] * sg[0:half], od.astype(BF16) * sg[half:]], axis=0)
    y_t = _dot(w_ref[...], m)
    o_ref[...] = x_ref[...] + gate_ref[...] * y_t.T


def _out_cd(x, oc_t, gb_t, u_t, conv_wb, sg_t, gate, w_out_t, ts):
    b, n, d = x.shape
    r = ts // LANES
    nb = n // LANES
    return pl.pallas_call(
        functools.partial(_out_cd_kernel, ts=ts),
        grid=(b, n // ts),
        in_specs=[
            pl.BlockSpec((None, ts, d), lambda i, t: (i, t, 0)),
            pl.BlockSpec((None, d - CONV_CH, ts), lambda i, t: (i, 0, t)),
            pl.BlockSpec((None, CONV_CH, ts), lambda i, t: (i, 0, t)),
            pl.BlockSpec((None, CONV_CH, LANES), lambda i, t: (i, 0, jnp.maximum(t * r - 1, 0))),
            pl.BlockSpec((None, CONV_CH, ts), lambda i, t: (i, 0, t)),
            pl.BlockSpec((None, CONV_CH, LANES), lambda i, t: (i, 0, jnp.minimum((t + 1) * r, nb - 1))),
            pl.BlockSpec((CONV_K, CONV_CH, LANES), lambda i, t: (0, 0, 0)),
            pl.BlockSpec((None, d, ts), lambda i, t: (i, 0, t)),
            pl.BlockSpec((None, 1, d), lambda i, t: (i, 0, 0)),
            pl.BlockSpec((d, d), lambda i, t: (0, 0)),
        ],
        out_specs=pl.BlockSpec((None, ts, d), lambda i, t: (i, t, 0)),
        out_shape=jax.ShapeDtypeStruct((b, n, d), F32),
        compiler_params=_cparams(("parallel", "parallel")),
        name="out_cd",
    )(x, oc_t, gb_t, u_t, u_t, u_t, conv_wb, sg_t, gate, w_out_t)


def _rope_tables(s, tail, dd):
    q = dd // 4
    rows = s // GRID_W
    row = jnp.repeat(jnp.arange(rows, dtype=F32), GRID_W)
    col = jnp.tile(jnp.arange(GRID_W, dtype=F32), rows)
    inv = ROPE_BASE ** (-jnp.arange(q, dtype=F32) / q)
    ar = inv[:, None] * row[None, :]
    ac = inv[:, None] * col[None, :]
    cos = jnp.concatenate([jnp.cos(ar), jnp.cos(ar), jnp.cos(ac), jnp.cos(ac)], axis=0)
    sin = jnp.concatenate([-jnp.sin(ar), jnp.sin(ar), -jnp.sin(ac), jnp.sin(ac)], axis=0)
    return (jnp.concatenate([cos, jnp.ones((dd, tail), F32)], axis=1),
            jnp.concatenate([sin, jnp.zeros((dd, tail), F32)], axis=1))


def _tile_rows(n, cap):
    t = cap
    while n % t:
        t //= 2
    return t


def _key_tile(n, cap, step):
    t = (cap // step) * step
    while n % t:
        t -= step
    return t


def kernel(x, c, ctx, c_ctx, mod_w, mod_b, ab_w_in, ab_w_out, mla_cq_gain, mla_ckv_gain, mla_w_uq, mla_w_ukv,
           mla_q_gain, mla_k_gain, gqa_q_gain, gqa_k_gain, cd_w_in, cd_w_out, win_q_gain, win_k_gain,
           win_sink, conv_w):
    b, s, d = x.shape
    lc = ctx.shape[1]
    ts_p = _tile_rows(s, 4 * PROJ_PART)
    assert ts_p % lc == 0 and ts_p % PROJ_PART == 0 and s % lc == 0
    ts_x = _tile_rows(s, 512)
    ts_c = _tile_rows(lc, 256)
    tq = _tile_rows(s, 512)
    n = s + lc
    tk = _key_tile(n, 3328, LANES)
    kc = 256 if tk % 256 == 0 else tk
    ahead = 2
    tq_w = _tile_rows(s, 512)

    crows = jnp.zeros((8, d), F32).at[0:b].set(c).at[b].set(c_ctx)
    mods = _modulation(crows, mod_w, mod_b)

    def split_mod(layer):
        m = mods[layer]
        shift, scale, gate = m[:, 0:d], m[:, d:2 * d], m[:, 2 * d:3 * d]
        lat = jnp.stack([shift[0:b], scale[0:b]], axis=1)
        con = jnp.broadcast_to(jnp.stack([shift[b], scale[b]], axis=0)[None], (b, 2, d))
        gate_x = gate[0:b][:, None, :]
        gate_c = jnp.broadcast_to(gate[b][None, None, :], (b, 1, d))
        return jnp.stack([lat, con], axis=1), gate_x, gate_c

    tabs32 = _rope_tables(s, ts_p, MLA_ROPE)
    tabs64 = _rope_tables(s, ts_p, GQA_DIM)

    mod4, gate_x, gate_c = split_mod(0)
    w_in_t = ab_w_in[0].T.astype(BF16)
    w_uq_t = mla_w_uq[0].T.astype(BF16)
    w_ukv_t = mla_w_ukv[0].T.astype(BF16)
    w_out_t = ab_w_out[0].T.astype(BF16)
    gains = (mla_cq_gain[0], mla_ckv_gain[0], mla_q_gain[0], mla_k_gain[0], gqa_q_gain[0], gqa_k_gain[0])
    qa, ka, va, qb, kb, vb, sg = _proj_ab(x, ctx, mod4, w_in_t, w_uq_t, w_ukv_t, gains, tabs32 + tabs64, ts_p)
    kb = kb[:, None]
    oa = _stream_attn(qa, ka, va, s, n, tq=tq, tk=tk, kc=kc, ahead=ahead, name="flash_mla")
    ob = _stream_attn(qb, kb, vb, s, n, tq=tq, tk=tk, kc=kc, ahead=ahead, name="flash_gqa")
    x1 = _out_ab(x, oa.reshape(b, -1, s), ob.reshape(b, -1, s), sg, 0, gate_x, w_out_t, ts_x)
    oac = _ctx_attn(qa, ka, va, s, lc, "ctx_mla")
    obc = _ctx_attn(qb, kb, vb, s, lc, "ctx_gqa")
    xc1 = _out_ab(ctx, oac.reshape(b, -1, lc), obc.reshape(b, -1, lc), sg, s // ts_c, gate_c, w_out_t, ts_c)

    mod4, gate_x, gate_c = split_mod(1)
    w_in_t = cd_w_in[0].T.astype(BF16)
    w_out_t = cd_w_out[0].T.astype(BF16)
    q, k, v, gb, u, sg = _proj_cd(x1, xc1, mod4, w_in_t, (win_q_gain[0], win_k_gain[0]), tabs64, ts_p)
    oc = _win_attn(q, k, v, win_sink[0].astype(F32), s, lc, tq=tq_w, sub=min(tq_w, 256))
    conv_wb = jnp.broadcast_to(conv_w[0].astype(F32)[:, :, None], (CONV_K, CONV_CH, LANES))
    return _out_cd(x1, oc.reshape(b, -1, s), gb, u, conv_wb, sg, gate_x, w_out_t, ts_x)
```

```python
import functools
import math

import jax
import jax.numpy as jnp
from jax import lax
from jax.experimental import pallas as pl
from jax.experimental.pallas import tpu as pltpu

D_MODEL = 1024
GRID_W = 64
ROPE_BASE = 10000.0
EPS = 1e-6
NEG_INF = -1e30
WINDOW = 128

MLA_HEADS = 8
MLA_NOPE = 64
MLA_ROPE = 32
MLA_V = 64
MLA_QK = MLA_NOPE + MLA_ROPE
MLA_Q_RANK = 256
MLA_KV_RANK = 256
GQA_HEADS = 8
GQA_KV_HEADS = 2
GQA_DIM = 64
WIN_HEADS = 8
WIN_KV_HEADS = 2
WIN_DIM = 64
CONV_CH = 512
CONV_K = 3

LANES = 128
HEAD_PAD = 128
STAT_ROWS = 8
PROJ_PART = 256
GATE_ROWS = 256
WIN_AHEAD = 2
LOG2E = math.log2(math.e)
VMEM_LIMIT_BYTES = 56 * 1024 * 1024

BF16 = jnp.bfloat16
F32 = jnp.float32


def _cparams(sem):
    return pltpu.CompilerParams(dimension_semantics=sem, vmem_limit_bytes=VMEM_LIMIT_BYTES)


def _dot(a, b):
    return jnp.dot(a, b, preferred_element_type=F32)


def _dot_nt(a, b):
    return lax.dot_general(a, b, (((1,), (1,)), ((), ())), preferred_element_type=F32)


def _silu(v):
    return v * (1.0 / (1.0 + jnp.exp(-v)))


def _mod_kernel(c_ref, w_ref, b_ref, o_ref):
    sc = _silu(c_ref[...])
    o_ref[...] = _dot(sc, w_ref[...]) + b_ref[...]


def _modulation(crows, mod_w, mod_b):
    depth, d, n = mod_w.shape
    tn = 1024
    return pl.pallas_call(
        _mod_kernel,
        grid=(depth, n // tn),
        in_specs=[
            pl.BlockSpec((8, d), lambda l, j: (0, 0)),
            pl.BlockSpec((None, d, tn), lambda l, j: (l, 0, j)),
            pl.BlockSpec((None, 1, tn), lambda l, j: (l, 0, j)),
        ],
        out_specs=pl.BlockSpec((None, 8, tn), lambda l, j: (l, 0, j)),
        out_shape=jax.ShapeDtypeStruct((depth, 8, n), F32),
        compiler_params=_cparams(("parallel", "parallel")),
        name="modulation",
    )(crows, mod_w, mod_b.reshape(depth, 1, n))


def _adaln_rows(x, shift, scale):
    ms = jnp.mean(x * x, axis=-1, keepdims=True)
    return (x * lax.rsqrt(ms + EPS)) * (1.0 + scale) + shift


def _rms_cols(v, gain_b, n):
    ss = jnp.sum(v * v, axis=0, keepdims=True)
    return v * lax.rsqrt(ss * (1.0 / n) + EPS) * gain_b


def _rope_cols(v, cos, sin, q):
    swap = jnp.concatenate([v[q:2 * q], v[0:q], v[3 * q:4 * q], v[2 * q:3 * q]], axis=0)
    return v * cos + swap * sin


def _projected_parts(x_ref, c_ref, mod_ref, w_in_ref, nx, groups):
    ts = x_ref.shape[0]
    ctx_rows = jnp.concatenate([c_ref[...]] * (ts // c_ref.shape[0]), axis=0)
    rows = jnp.where(pl.program_id(1) < nx, x_ref[...], ctx_rows)
    shift, scale = mod_ref[0:1, :], mod_ref[1:2, :]
    parts = ts // PROJ_PART
    cols = lambda i: slice(i * PROJ_PART, (i + 1) * PROJ_PART)

    def normed(i):
        return _adaln_rows(rows[cols(i)], shift, scale).astype(BF16)

    def project(h, g):
        r0, r1, _ = groups[g]
        return _dot_nt(w_in_ref[r0:r1, :], h)

    h = normed(0)
    pend = [project(h, g) for g in range(len(groups))]
    for i in range(parts):
        nxt = []
        if i + 1 < parts:
            h = normed(i + 1)
        for g in range(len(groups)):
            if i + 1 < parts:
                nxt.append(project(h, g))
            groups[g][2](pend[g], cols(i))
        pend = nxt


def _joint_specs(nx, ts, lc, d):
    return [
        pl.BlockSpec((None, ts, d), lambda i, t: (i, jnp.minimum(t, nx - 1), 0)),
        pl.BlockSpec((None, lc, d), lambda i, t: (i, 0, 0)),
        pl.BlockSpec((None, None, 2, d), lambda i, t: (i, jnp.where(t < nx, 0, 1), 0, 0)),
    ]


def _bcast_gain(g):
    return jnp.broadcast_to(g.astype(F32)[:, None], (g.shape[0], PROJ_PART))


def _proj_ab_kernel(x_ref, c_ref, mod_ref, w_in_ref, cqg_ref, ckvg_ref, w_uq_ref, w_ukv_ref,
                    qg_ref, kg_ref, gqg_ref, gkg_ref, c32_ref, s32_ref, c64_ref, s64_ref,
                    qa_ref, ka_ref, va_ref, qb_ref, kb_ref, vb_ref, sg_ref, *, nx):
    zpad = jnp.zeros((HEAD_PAD - MLA_QK, PROJ_PART), F32)
    z64 = jnp.zeros((GQA_DIM, PROJ_PART), BF16)
    q_scale = MLA_QK ** -0.5 * LOG2E
    g_scale = GQA_DIM ** -0.5 * LOG2E
    grp = GQA_HEADS // GQA_KV_HEADS

    def tail_mla(pp, cols):
        cq = pp[0:256]
        ckv = pp[256:512]
        kr = pp[512:544]
        cqn = _rms_cols(cq, cqg_ref[...], MLA_Q_RANK).astype(BF16)
        ckvn = _rms_cols(ckv, ckvg_ref[...], MLA_KV_RANK).astype(BF16)
        qa = _dot(w_uq_ref[...], cqn)
        kv = _dot(w_ukv_ref[...], ckvn)
        qg = qg_ref[...]
        kg = kg_ref[...]
        c32 = c32_ref[:, cols]
        s32 = s32_ref[:, cols]
        kr_ss = jnp.sum(kr * kr, axis=0, keepdims=True)
        for hd in range(MLA_HEADS):
            qh = _rms_cols(qa[hd * MLA_QK:(hd + 1) * MLA_QK], qg, MLA_QK)
            qr = _rope_cols(qh[MLA_NOPE:], c32, s32, MLA_ROPE // 4)
            qa_ref[hd, 0:MLA_NOPE, cols] = (qh[0:MLA_NOPE] * q_scale).astype(BF16)
            qa_ref[hd, MLA_NOPE:MLA_QK, cols] = (qr * q_scale).astype(BF16)
            qa_ref[hd, MLA_QK:HEAD_PAD, cols] = zpad.astype(BF16)
            kn = kv[hd * 128:hd * 128 + MLA_NOPE]
            ss = jnp.sum(kn * kn, axis=0, keepdims=True) + kr_ss
            rs = lax.rsqrt(ss * (1.0 / MLA_QK) + EPS)
            krn = _rope_cols(kr * rs * kg[MLA_NOPE:], c32, s32, MLA_ROPE // 4)
            k_cols = jnp.concatenate([kn * rs * kg[0:MLA_NOPE], krn, zpad], axis=0)
            ka_ref[hd, cols, :] = k_cols.T.astype(BF16)
            va_ref[hd, :, cols] = kv[hd * 128 + MLA_NOPE:(hd + 1) * 128].astype(BF16)

    def tail_gqa(pp, cols):
        gq = pp[0:512]
        gk = pp[512:640]
        gv = pp[640:768]
        c64 = c64_ref[:, cols]
        s64 = s64_ref[:, cols]
        gqg = gqg_ref[...]
        gkg = gkg_ref[...]
        for hd in range(GQA_HEADS):
            qh = _rms_cols(gq[hd * GQA_DIM:(hd + 1) * GQA_DIM], gqg, GQA_DIM)
            qh = (_rope_cols(qh, c64, s64, GQA_DIM // 4) * g_scale).astype(BF16)
            for kvh in range(GQA_KV_HEADS):
                qb_ref[hd, kvh * GQA_DIM:(kvh + 1) * GQA_DIM, cols] = qh if kvh == hd // grp else z64
        k_heads = []
        for kvh in range(GQA_KV_HEADS):
            kh = _rms_cols(gk[kvh * GQA_DIM:(kvh + 1) * GQA_DIM], gkg, GQA_DIM)
            k_heads.append(_rope_cols(kh, c64, s64, GQA_DIM // 4))
            vb_ref[kvh, :, cols] = gv[kvh * GQA_DIM:(kvh + 1) * GQA_DIM].astype(BF16)
        kb_ref[cols, :] = jnp.concatenate(k_heads, axis=0).T.astype(BF16)

    def tail_gate(r0):
        def tail(pp, cols):
            sg_ref[r0:r0 + GATE_ROWS, cols] = _silu(pp).astype(BF16)
        return tail

    g0 = 544 + 768
    groups = [(0, 544, tail_mla), (544, g0, tail_gqa)]
    groups += [(g0 + r, g0 + r + GATE_ROWS, tail_gate(r)) for r in range(0, D_MODEL, GATE_ROWS)]
    _projected_parts(x_ref, c_ref, mod_ref, w_in_ref, nx, groups)


def _proj_ab(x, ctx, mods, w_in_t, w_uq_t, w_ukv_t, gains, tabs, ts):
    b, s, d = x.shape
    lc = ctx.shape[1]
    nx = s // ts
    n = s + ts
    cqg, ckvg, qg, kg, gqg, gkg = [_bcast_gain(g) for g in gains]
    c32, s32, c64, s64 = tabs
    full = lambda a: pl.BlockSpec(a.shape, lambda i, t: (0,) * a.ndim)
    tab = lambda a: pl.BlockSpec((a.shape[0], ts), lambda i, t: (0, t))
    feat4 = lambda h, r: (jax.ShapeDtypeStruct((b, h, r, n), BF16),
                          pl.BlockSpec((None, h, r, ts), lambda i, t: (i, 0, 0, t)))
    rows4 = lambda h: (jax.ShapeDtypeStruct((b, h, n, HEAD_PAD), BF16),
                       pl.BlockSpec((None, h, ts, HEAD_PAD), lambda i, t: (i, 0, t, 0)))
    outs = [
        feat4(MLA_HEADS, HEAD_PAD),
        rows4(MLA_HEADS),
        feat4(MLA_HEADS, MLA_V),
        feat4(GQA_HEADS, HEAD_PAD),
        (jax.ShapeDtypeStruct((b, n, HEAD_PAD), BF16),
         pl.BlockSpec((None, ts, HEAD_PAD), lambda i, t: (i, t, 0))),
        feat4(GQA_KV_HEADS, GQA_DIM),
        (jax.ShapeDtypeStruct((b, D_MODEL, n), BF16),
         pl.BlockSpec((None, D_MODEL, ts), lambda i, t: (i, 0, t))),
    ]
    return pl.pallas_call(
        functools.partial(_proj_ab_kernel, nx=nx),
        grid=(b, nx + 1),
        in_specs=_joint_specs(nx, ts, lc, d) + [
            full(w_in_t), full(cqg), full(ckvg), full(w_uq_t), full(w_ukv_t),
            full(qg), full(kg), full(gqg), full(gkg),
            tab(c32), tab(s32), tab(c64), tab(s64),
        ],
        out_specs=[o[1] for o in outs],
        out_shape=[o[0] for o in outs],
        compiler_params=_cparams(("parallel", "parallel")),
        name="proj_ab",
    )(x, ctx, mods, w_in_t, cqg, ckvg, w_uq_t, w_ukv_t, qg, kg, gqg, gkg, c32, s32, c64, s64)


P_LIMIT = 2.0 ** 100
FIRST_REF_KEYS = 128


def _stream_kernel(q_ref, k_ref, v_ref, o_ref, m_sc, acc_sc, pv_sc, *, heads, kc, ahead):
    j = pl.program_id(2)
    tk = k_ref.shape[1]
    tq = q_ref.shape[2]
    dv = o_ref.shape[1]
    nc = tk // kc
    gk = k_ref.shape[0]
    gv = v_ref.shape[0]
    cur = lax.rem(j, 2)
    nxt = 1 - cur

    @pl.when(j == 0)
    def _():
        for g in range(heads):
            s0 = _dot(k_ref[g * gk // heads, 0:FIRST_REF_KEYS, :], q_ref[g])
            m_sc[0, g] = jnp.max(s0, axis=0, keepdims=True)
        acc_sc[0] = jnp.zeros(acc_sc.shape[1:], F32)

    units = [(g, c) for g in range(heads) for c in range(nc)]
    pending = {}

    def emit_scores(u):
        g, c = u
        pending[u] = _dot(k_ref[g * gk // heads, c * kc:(c + 1) * kc, :], q_ref[g])

    worst = None
    issued = 0
    pmx = None
    lsum = None
    for i, (g, c) in enumerate(units):
        while issued <= min(i + ahead, len(units) - 1):
            emit_scores(units[issued])
            issued += 1
        s = pending.pop((g, c))
        m_ref = m_sc[cur, g]
        mb = jnp.broadcast_to(m_ref, (8, tq))
        p32 = jnp.exp2(s.reshape(kc // 8, 8, tq) - mb[None])
        lpart = jnp.sum(p32, axis=0)
        lsum = lpart if c == 0 else lsum + lpart
        p = p32.reshape(kc, tq).astype(BF16)
        part = jnp.max(p.reshape(kc // 16, 16, tq), axis=0)
        pmx = part if c == 0 else jnp.maximum(pmx, part)
        d = _dot(v_ref[g * gv // heads, :, c * kc:(c + 1) * kc], p)
        if c == 0:
            pv_sc[g] = d
        else:
            pv_sc[g] += d
        if c == nc - 1:
            pm = jnp.maximum(jnp.max(pmx.astype(F32), axis=0, keepdims=True), 1.0)
            worst = pm if worst is None else jnp.maximum(worst, pm)
            m_sc[nxt, g] = m_ref + jnp.log2(pm)
            rp = 1.0 / pm
            acc_sc[nxt, g, 0:dv, :] = (acc_sc[cur, g, 0:dv, :] + pv_sc[g]) * rp
            acc_sc[nxt, g, dv:dv + 1, :] = (acc_sc[cur, g, dv:dv + 1, :] + jnp.sum(lsum, axis=0, keepdims=True)) * rp

    redo = jnp.logical_not(jnp.max(worst) <= P_LIMIT)

    @pl.when(redo)
    def _():
        def head(g, carry):
            q = q_ref[g]
            gki = g * gk // heads
            gvi = g * gv // heads
            m_old = m_sc[cur, g]

            def cmax(c, mx):
                s = _dot(k_ref[gki, pl.ds(pl.multiple_of(c * kc, kc), kc), :], q)
                return jnp.maximum(mx, jnp.max(s, axis=0, keepdims=True))

            m_new = lax.fori_loop(0, nc, cmax, m_old)

            def cacc(c, st):
                pv, l = st
                ks = pl.ds(pl.multiple_of(c * kc, kc), kc)
                p = jnp.exp2(_dot(k_ref[gki, ks, :], q) - m_new)
                return pv + _dot(v_ref[gvi, :, ks], p.astype(BF16)), l + jnp.sum(p, axis=0, keepdims=True)

            pv, l = lax.fori_loop(0, nc, cacc, (jnp.zeros((dv, tq), F32), jnp.zeros((1, tq), F32)))
            alpha = jnp.exp2(m_old - m_new)
            m_sc[nxt, g] = m_new
            acc_sc[nxt, g, 0:dv, :] = acc_sc[cur, g, 0:dv, :] * alpha + pv
            acc_sc[nxt, g, dv:dv + 1, :] = acc_sc[cur, g, dv:dv + 1, :] * alpha + l
            return carry

        lax.fori_loop(0, heads, head, 0)

    @pl.when(j == pl.num_programs(2) - 1)
    def _():
        for g in range(heads):
            o_ref[g] = (acc_sc[nxt, g, 0:dv, :] / acc_sc[nxt, g, dv:dv + 1, :]).astype(o_ref.dtype)


def _stream_attn(q_t, k, v_t, s, n, *, tq, tk, kc, ahead, name):
    b, hq = q_t.shape[0], q_t.shape[1]
    hk = k.shape[1]
    hv, dv = v_t.shape[1], v_t.shape[2]
    return pl.pallas_call(
        functools.partial(_stream_kernel, heads=hq, kc=kc, ahead=ahead),
        grid=(b, s // tq, n // tk),
        in_specs=[
            pl.BlockSpec((None, hq, HEAD_PAD, tq), lambda i, qi, j: (i, 0, 0, qi)),
            pl.BlockSpec((None, hk, tk, HEAD_PAD), lambda i, qi, j: (i, 0, j, 0)),
            pl.BlockSpec((None, hv, dv, tk), lambda i, qi, j: (i, 0, 0, j)),
        ],
        out_specs=pl.BlockSpec((None, hq, dv, tq), lambda i, qi, j: (i, 0, 0, qi)),
        out_shape=jax.ShapeDtypeStruct((b, hq, dv, s), BF16),
        scratch_shapes=[
            pltpu.VMEM((2, hq, 1, tq), F32),
            pltpu.VMEM((2, hq, dv + STAT_ROWS, tq), F32),
            pltpu.VMEM((hq, dv, tq), F32),
        ],
        compiler_params=_cparams(("parallel", "parallel", "arbitrary")),
        name=name,
    )(q_t, k, v_t)


def _ctx_attn_kernel(q_ref, k_ref, v_ref, o_ref):
    heads, gk, gv = q_ref.shape[0], k_ref.shape[0], v_ref.shape[0]
    for g in range(heads):
        s = _dot(k_ref[g * gk // heads], q_ref[g])
        p = jnp.exp2(s - jnp.max(s, axis=0, keepdims=True))
        den = jnp.sum(p, axis=0, keepdims=True)
        o_ref[g] = (_dot(v_ref[g * gv // heads], p.astype(BF16)) / den).astype(o_ref.dtype)


def _ctx_attn(q_t, k, v_t, s, lc, name):
    b, hq = q_t.shape[0], q_t.shape[1]
    hk = k.shape[1]
    hv, dv = v_t.shape[1], v_t.shape[2]
    off = s // lc
    return pl.pallas_call(
        _ctx_attn_kernel,
        grid=(b,),
        in_specs=[
            pl.BlockSpec((None, hq, HEAD_PAD, lc), lambda i: (i, 0, 0, off)),
            pl.BlockSpec((None, hk, lc, HEAD_PAD), lambda i: (i, 0, off, 0)),
            pl.BlockSpec((None, hv, dv, lc), lambda i: (i, 0, 0, off)),
        ],
        out_specs=pl.BlockSpec((None, hq, dv, lc), lambda i: (i, 0, 0, 0)),
        out_shape=jax.ShapeDtypeStruct((b, hq, dv, lc), BF16),
        compiler_params=_cparams(("parallel",)),
        name=name,
    )(q_t, k, v_t)


def _out_ab_kernel(x_ref, oa_ref, ob_ref, sg_ref, gate_ref, w_ref, o_ref):
    m = jnp.concatenate([oa_ref[...], ob_ref[...]], axis=0) * sg_ref[...]
    y_t = _dot(w_ref[...], m)
    o_ref[...] = x_ref[...] + gate_ref[...] * y_t.T


def _out_ab(x, oa_t, ob_t, sg_t, sg_off, gate, w_out_t, ts):
    b, n, d = x.shape
    half = oa_t.shape[1]
    return pl.pallas_call(
        _out_ab_kernel,
        grid=(b, n // ts),
        in_specs=[
            pl.BlockSpec((None, ts, d), lambda i, t: (i, t, 0)),
            pl.BlockSpec((None, half, ts), lambda i, t: (i, 0, t)),
            pl.BlockSpec((None, d - half, ts), lambda i, t: (i, 0, t)),
            pl.BlockSpec((None, d, ts), lambda i, t: (i, 0, t + sg_off)),
            pl.BlockSpec((None, 1, d), lambda i, t: (i, 0, 0)),
            pl.BlockSpec((d, d), lambda i, t: (0, 0)),
        ],
        out_specs=pl.BlockSpec((None, ts, d), lambda i, t: (i, t, 0)),
        out_shape=jax.ShapeDtypeStruct((b, n, d), F32),
        compiler_params=_cparams(("parallel", "parallel")),
        name="out_ab",
    )(x, oa_t, ob_t, sg_t, gate, w_out_t)


def _proj_cd_kernel(x_ref, c_ref, mod_ref, w_in_ref, qg_ref, kg_ref, c64_ref, s64_ref,
                    q_ref, k_ref, v_ref, gb_ref, u_ref, sg_ref, *, nx):
    scale = WIN_DIM ** -0.5 * LOG2E
    z64 = jnp.zeros((WIN_DIM, PROJ_PART), BF16)
    grp = WIN_HEADS // WIN_KV_HEADS

    def tail_q(pp, cols):
        c64 = c64_ref[:, cols]
        s64 = s64_ref[:, cols]
        qg = qg_ref[...]
        for hd in range(WIN_HEADS):
            qh = _rms_cols(pp[hd * WIN_DIM:(hd + 1) * WIN_DIM], qg, WIN_DIM)
            qh = (_rope_cols(qh, c64, s64, WIN_DIM // 4) * scale).astype(BF16)
            for kvh in range(WIN_KV_HEADS):
                q_ref[hd, kvh * WIN_DIM:(kvh + 1) * WIN_DIM, cols] = qh if kvh == hd // grp else z64

    def tail_kv(pp, cols):
        c64 = c64_ref[:, cols]
        s64 = s64_ref[:, cols]
        kg = kg_ref[...]
        k_heads = []
        for kvh in range(WIN_KV_HEADS):
            kh = _rms_cols(pp[kvh * WIN_DIM:(kvh + 1) * WIN_DIM], kg, WIN_DIM)
            k_heads.append(_rope_cols(kh, c64, s64, WIN_DIM // 4))
            v_ref[kvh, :, cols] = pp[128 + kvh * WIN_DIM:128 + (kvh + 1) * WIN_DIM].astype(BF16)
        k_ref[cols, :] = jnp.concatenate(k_heads, axis=0).T.astype(BF16)
        gb_ref[:, cols] = pp[256:768].astype(BF16)

    def tail_conv(pp, cols):
        u_ref[:, cols] = (pp[0:CONV_CH] * pp[CONV_CH:2 * CONV_CH]).astype(BF16)

    def tail_gate(r0):
        def tail(pp, cols):
            sg_ref[r0:r0 + GATE_ROWS, cols] = _silu(pp).astype(BF16)
        return tail

    g0 = 2304
    groups = [(0, 512, tail_q), (512, 1280, tail_kv), (1280, g0, tail_conv)]
    groups += [(g0 + r, g0 + r + GATE_ROWS, tail_gate(r)) for r in range(0, D_MODEL, GATE_ROWS)]
    _projected_parts(x_ref, c_ref, mod_ref, w_in_ref, nx, groups)


def _proj_cd(x, ctx, mods, w_in_t, gains, tabs, ts):
    b, s, d = x.shape
    lc = ctx.shape[1]
    nx = s // ts
    n = s + ts
    qg, kg = [_bcast_gain(g) for g in gains]
    c64, s64 = tabs
    full = lambda a: pl.BlockSpec(a.shape, lambda i, t: (0,) * a.ndim)
    tab = lambda a: pl.BlockSpec((a.shape[0], ts), lambda i, t: (0, t))
    feat3 = lambda r: (jax.ShapeDtypeStruct((b, r, n), BF16), pl.BlockSpec((None, r, ts), lambda i, t: (i, 0, t)))
    outs = [
        (jax.ShapeDtypeStruct((b, WIN_HEADS, HEAD_PAD, n), BF16),
         pl.BlockSpec((None, WIN_HEADS, HEAD_PAD, ts), lambda i, t: (i, 0, 0, t))),
        (jax.ShapeDtypeStruct((b, n, HEAD_PAD), BF16),
         pl.BlockSpec((None, ts, HEAD_PAD), lambda i, t: (i, t, 0))),
        (jax.ShapeDtypeStruct((b, WIN_KV_HEADS, WIN_DIM, n), BF16),
         pl.BlockSpec((None, WIN_KV_HEADS, WIN_DIM, ts), lambda i, t: (i, 0, 0, t))),
        feat3(CONV_CH),
        feat3(CONV_CH),
        feat3(D_MODEL),
    ]
    return pl.pallas_call(
        functools.partial(_proj_cd_kernel, nx=nx),
        grid=(b, nx + 1),
        in_specs=_joint_specs(nx, ts, lc, d) + [full(w_in_t), full(qg), full(kg), tab(c64), tab(s64)],
        out_specs=[o[1] for o in outs],
        out_shape=[o[0] for o in outs],
        compiler_params=_cparams(("parallel", "parallel")),
        name="proj_cd",
    )(x, ctx, mods, w_in_t, qg, kg, c64, s64)


def _win_kernel(q_ref, kp_ref, kc_ref, kn_ref, vp_ref, vcur_ref, vn_ref, kctx_ref, vctx_ref, bias_ref, sink_ref,
                o_ref, *, sub):
    heads, tq = q_ref.shape[0], q_ref.shape[2]
    gv = vcur_ref.shape[0]
    nsub = tq // sub
    qi = pl.program_id(1)
    edge_p = jnp.where(qi > 0, 0.0, NEG_INF)
    edge_n = jnp.where(qi < pl.num_programs(1) - 1, 0.0, NEG_INF)
    bias_mid = bias_ref[...]
    bias_first = jnp.concatenate([bias_ref[0:WINDOW, :] + edge_p, bias_ref[WINDOW:, :]], axis=0)
    bias_last = jnp.concatenate([bias_ref[0:sub + WINDOW, :], bias_ref[sub + WINDOW:, :] + edge_n], axis=0)
    if nsub == 1:
        bias_first = bias_last = jnp.concatenate(
            [bias_ref[0:WINDOW, :] + edge_p, bias_ref[WINDOW:sub + WINDOW, :], bias_ref[sub + WINDOW:, :] + edge_n],
            axis=0)
    k_loc = jnp.concatenate([kp_ref[...], kc_ref[...], kn_ref[...]], axis=0)
    v_loc = [jnp.concatenate([vp_ref[h], vcur_ref[h], vn_ref[h]], axis=1) for h in range(gv)]
    kctx = kctx_ref[...]
    units = [(g, t) for g in range(heads) for t in range(nsub)]
    pending = {}

    def emit_scores(u):
        g, t = u
        q = q_ref[g, :, t * sub:(t + 1) * sub]
        bias = bias_first if t == 0 else (bias_last if t == nsub - 1 else bias_mid)
        pending[u] = (_dot(k_loc[t * sub:(t + 1) * sub + 2 * WINDOW], q) + bias, _dot(kctx, q))

    issued = 0
    for i, (g, t) in enumerate(units):
        while issued <= min(i + WIN_AHEAD, len(units) - 1):
            emit_scores(units[issued])
            issued += 1
        s_l, s_x = pending.pop((g, t))
        sk = sink_ref[g] * LOG2E
        m = jnp.maximum(jnp.maximum(jnp.max(s_l, axis=0, keepdims=True), jnp.max(s_x, axis=0, keepdims=True)), sk)
        p_l = jnp.exp2(s_l - m)
        p_x = jnp.exp2(s_x - m)
        den = jnp.sum(p_l, axis=0, keepdims=True) + jnp.sum(p_x, axis=0, keepdims=True) + jnp.exp2(sk - m)
        h = g * gv // heads
        acc = (_dot(v_loc[h][:, t * sub:(t + 1) * sub + 2 * WINDOW], p_l.astype(BF16))
               + _dot(vctx_ref[h], p_x.astype(BF16)))
        o_ref[g, :, t * sub:(t + 1) * sub] = (acc / den).astype(o_ref.dtype)


def _win_attn(q_t, k, v_t, sink, s, lc, *, tq, sub):
    b, hq = q_t.shape[0], q_t.shape[1]
    hv, dv = v_t.shape[1], v_t.shape[2]
    r = tq // WINDOW
    nb = s // WINDOW
    prev = lambda qi: jnp.maximum(qi * r - 1, 0)
    nxt = lambda qi: jnp.minimum((qi + 1) * r, nb - 1)
    kpos = jnp.arange(sub + 2 * WINDOW, dtype=jnp.int32)[:, None] - WINDOW
    qpos = jnp.arange(sub, dtype=jnp.int32)[None, :]
    bias = jnp.where(jnp.abs(qpos - kpos) <= WINDOW, 0.0, NEG_INF).astype(F32)
    return pl.pallas_call(
        functools.partial(_win_kernel, sub=sub),
        grid=(b, s // tq),
        in_specs=[
            pl.BlockSpec((None, hq, HEAD_PAD, tq), lambda i, qi: (i, 0, 0, qi)),
            pl.BlockSpec((None, WINDOW, HEAD_PAD), lambda i, qi: (i, prev(qi), 0)),
            pl.BlockSpec((None, tq, HEAD_PAD), lambda i, qi: (i, qi, 0)),
            pl.BlockSpec((None, WINDOW, HEAD_PAD), lambda i, qi: (i, nxt(qi), 0)),
            pl.BlockSpec((None, hv, dv, WINDOW), lambda i, qi: (i, 0, 0, prev(qi))),
            pl.BlockSpec((None, hv, dv, tq), lambda i, qi: (i, 0, 0, qi)),
            pl.BlockSpec((None, hv, dv, WINDOW), lambda i, qi: (i, 0, 0, nxt(qi))),
            pl.BlockSpec((None, lc, HEAD_PAD), lambda i, qi: (i, s // lc, 0)),
            pl.BlockSpec((None, hv, dv, lc), lambda i, qi: (i, 0, 0, s // lc)),
            pl.BlockSpec(bias.shape, lambda i, qi: (0, 0)),
            pl.BlockSpec(memory_space=pltpu.SMEM),
        ],
        out_specs=pl.BlockSpec((None, hq, dv, tq), lambda i, qi: (i, 0, 0, qi)),
        out_shape=jax.ShapeDtypeStruct((b, hq, dv, s), BF16),
        compiler_params=_cparams(("parallel", "parallel")),
        name="win_attn",
    )(q_t, k, k, k, v_t, v_t, v_t, k, v_t, bias, sink)


def _out_cd_kernel(x_ref, oc_ref, gb_ref, up_ref, u_ref, un_ref, cw_ref, sg_ref, gate_ref, w_ref, o_ref, *, ts):
    t = pl.program_id(1)
    nt = pl.num_programs(1)
    u = u_ref[...].astype(F32)
    lane = lax.broadcasted_iota(jnp.int32, (CONV_CH, ts), 1)
    halo_l = pltpu.roll(up_ref[...].astype(F32), 1, axis=1)
    halo_l = jnp.where(t > 0, halo_l, 0.0)
    halo_r = pltpu.roll(un_ref[...].astype(F32), LANES - 1, axis=1)
    halo_r = jnp.where(t < nt - 1, halo_r, 0.0)
    reps = ts // LANES
    um = jnp.where(lane == 0, jnp.tile(halo_l, (1, reps)), pltpu.roll(u, 1, axis=1))
    up = jnp.where(lane == ts - 1, jnp.tile(halo_r, (1, reps)), pltpu.roll(u, ts - 1, axis=1))
    w0 = jnp.tile(cw_ref[0], (1, reps))
    w1 = jnp.tile(cw_ref[1], (1, reps))
    w2 = jnp.tile(cw_ref[2], (1, reps))
    od = gb_ref[...].astype(F32) * (w0 * um + w1 * u + w2 * up)
    sg = sg_ref[...]
    half = D_MODEL - CONV_CH
    m = jnp.concatenate([oc_ref[...] * sg[0:half], od.astype(BF16) * sg[half:]], axis=0)
    y_t = _dot(w_ref[...], m)
    o_ref[...] = x_ref[...] + gate_ref[...] * y_t.T


def _out_cd(x, oc_t, gb_t, u_t, conv_wb, sg_t, gate, w_out_t, ts):
    b, n, d = x.shape
    r = ts // LANES
    nb = n // LANES
    return pl.pallas_call(
        functools.partial(_out_cd_kernel, ts=ts),
        grid=(b, n // ts),
        in_specs=[
            pl.BlockSpec((None, ts, d), lambda i, t: (i, t, 0)),
            pl.BlockSpec((None, d - CONV_CH, ts), lambda i, t: (i, 0, t)),
            pl.BlockSpec((None, CONV_CH, ts), lambda i, t: (i, 0, t)),
            pl.BlockSpec((None, CONV_CH, LANES), lambda i, t: (i, 0, jnp.maximum(t * r - 1, 0))),
            pl.BlockSpec((None, CONV_CH, ts), lambda i, t: (i, 0, t)),
            pl.BlockSpec((None, CONV_CH, LANES), lambda i, t: (i, 0, jnp.minimum((t + 1) * r, nb - 1))),
            pl.BlockSpec((CONV_K, CONV_CH, LANES), lambda i, t: (0, 0, 0)),
            pl.BlockSpec((None, d, ts), lambda i, t: (i, 0, t)),
            pl.BlockSpec((None, 1, d), lambda i, t: (i, 0, 0)),
            pl.BlockSpec((d, d), lambda i, t: (0, 0)),
        ],
        out_specs=pl.BlockSpec((None, ts, d), lambda i, t: (i, t, 0)),
        out_shape=jax.ShapeDtypeStruct((b, n, d), F32),
        compiler_params=_cparams(("parallel", "parallel")),
        name="out_cd",
    )(x, oc_t, gb_t, u_t, u_t, u_t, conv_wb, sg_t, gate, w_out_t)


def _rope_tables(s, tail, dd):
    q = dd // 4
    rows = s // GRID_W
    row = jnp.repeat(jnp.arange(rows, dtype=F32), GRID_W)
    col = jnp.tile(jnp.arange(GRID_W, dtype=F32), rows)
    inv = ROPE_BASE ** (-jnp.arange(q, dtype=F32) / q)
    ar = inv[:, None] * row[None, :]
    ac = inv[:, None] * col[None, :]
    cos = jnp.concatenate([jnp.cos(ar), jnp.cos(ar), jnp.cos(ac), jnp.cos(ac)], axis=0)
    sin = jnp.concatenate([-jnp.sin(ar), jnp.sin(ar), -jnp.sin(ac), jnp.sin(ac)], axis=0)
    return (jnp.concatenate([cos, jnp.ones((dd, tail), F32)], axis=1),
            jnp.concatenate([sin, jnp.zeros((dd, tail), F32)], axis=1))


def _tile_rows(n, cap):
    t = cap
    while n % t:
        t //= 2
    return t


def _key_tile(n, cap, step):
    t = (cap // step) * step
    while n % t:
        t -= step
    return t


def kernel(x, c, ctx, c_ctx, mod_w, mod_b, ab_w_in, ab_w_out, mla_cq_gain, mla_ckv_gain, mla_w_uq, mla_w_ukv,
           mla_q_gain, mla_k_gain, gqa_q_gain, gqa_k_gain, cd_w_in, cd_w_out, win_q_gain, win_k_gain,
           win_sink, conv_w):
    b, s, d = x.shape
    lc = ctx.shape[1]
    ts_p = _tile_rows(s, 4 * PROJ_PART)
    assert ts_p % lc == 0 and ts_p % PROJ_PART == 0 and s % lc == 0
    ts_x = _tile_rows(s, 1024)
    ts_c = _tile_rows(lc, 256)
    tq = _tile_rows(s, 512)
    n = s + lc
    tk = _key_tile(n, 3328, LANES)
    kc = 256 if tk % 256 == 0 else tk
    ahead = 2
    tq_w = _tile_rows(s, 512)

    crows = jnp.zeros((8, d), F32).at[0:b].set(c).at[b].set(c_ctx)
    mods = _modulation(crows, mod_w, mod_b)

    def split_mod(layer):
        m = mods[layer]
        shift, scale, gate = m[:, 0:d], m[:, d:2 * d], m[:, 2 * d:3 * d]
        lat = jnp.stack([shift[0:b], scale[0:b]], axis=1)
        con = jnp.broadcast_to(jnp.stack([shift[b], scale[b]], axis=0)[None], (b, 2, d))
        gate_x = gate[0:b][:, None, :]
        gate_c = jnp.broadcast_to(gate[b][None, None, :], (b, 1, d))
        return jnp.stack([lat, con], axis=1), gate_x, gate_c

    tabs32 = _rope_tables(s, ts_p, MLA_ROPE)
    tabs64 = _rope_tables(s, ts_p, GQA_DIM)

    mod4, gate_x, gate_c = split_mod(0)
    w_in_t = ab_w_in[0].T.astype(BF16)
    w_uq_t = mla_w_uq[0].T.astype(BF16)
    w_ukv_t = mla_w_ukv[0].T.astype(BF16)
    w_out_t = ab_w_out[0].T.astype(BF16)
    gains = (mla_cq_gain[0], mla_ckv_gain[0], mla_q_gain[0], mla_k_gain[0], gqa_q_gain[0], gqa_k_gain[0])
    qa, ka, va, qb, kb, vb, sg = _proj_ab(x, ctx, mod4, w_in_t, w_uq_t, w_ukv_t, gains, tabs32 + tabs64, ts_p)
    kb = kb[:, None]
    oa = _stream_attn(qa, ka, va, s, n, tq=tq, tk=tk, kc=kc, ahead=ahead, name="flash_mla")
    ob = _stream_attn(qb, kb, vb, s, n, tq=tq, tk=tk, kc=kc, ahead=ahead, name="flash_gqa")
    x1 = _out_ab(x, oa.reshape(b, -1, s), ob.reshape(b, -1, s), sg, 0, gate_x, w_out_t, ts_x)
    oac = _ctx_attn(qa, ka, va, s, lc, "ctx_mla")
    obc = _ctx_attn(qb, kb, vb, s, lc, "ctx_gqa")
    xc1 = _out_ab(ctx, oac.reshape(b, -1, lc), obc.reshape(b, -1, lc), sg, s // ts_c, gate_c, w_out_t, ts_c)

    mod4, gate_x, gate_c = split_mod(1)
    w_in_t = cd_w_in[0].T.astype(BF16)
    w_out_t = cd_w_out[0].T.astype(BF16)
    q, k, v, gb, u, sg = _proj_cd(x1, xc1, mod4, w_in_t, (win_q_gain[0], win_k_gain[0]), tabs64, ts_p)
    oc = _win_attn(q, k, v, win_sink[0].astype(F32), s, lc, tq=tq_w, sub=min(tq_w, 256))
    conv_wb = jnp.broadcast_to(conv_w[0].astype(F32)[:, :, None], (CONV_K, CONV_CH, LANES))
    return _out_cd(x1, oc.reshape(b, -1, s), gb, u, conv_wb, sg, gate_x, w_out_t, ts_x)
```

```python
import functools
import math

import jax
import jax.numpy as jnp
from jax import lax
from jax.experimental import pallas as pl
from jax.experimental.pallas import tpu as pltpu

D_MODEL = 1024
GRID_W = 64
ROPE_BASE = 10000.0
EPS = 1e-6
NEG_INF = -1e30
WINDOW = 128

MLA_HEADS = 8
MLA_NOPE = 64
MLA_ROPE = 32
MLA_V = 64
MLA_QK = MLA_NOPE + MLA_ROPE
MLA_Q_RANK = 256
MLA_KV_RANK = 256
GQA_HEADS = 8
GQA_KV_HEADS = 2
GQA_DIM = 64
WIN_HEADS = 8
WIN_KV_HEADS = 2
WIN_DIM = 64
CONV_CH = 512
CONV_K = 3

LANES = 128
HEAD_PAD = 128
STAT_ROWS = 8
PROJ_PART = 256
GATE_ROWS = 256
WIN_AHEAD = 2
LOG2E = math.log2(math.e)
VMEM_LIMIT_BYTES = 56 * 1024 * 1024

BF16 = jnp.bfloat16
F32 = jnp.float32


def _cparams(sem):
    return pltpu.CompilerParams(dimension_semantics=sem, vmem_limit_bytes=VMEM_LIMIT_BYTES)


def _dot(a, b):
    return jnp.dot(a, b, preferred_element_type=F32)


def _dot_nt(a, b):
    return lax.dot_general(a, b, (((1,), (1,)), ((), ())), preferred_element_type=F32)


def _silu(v):
    h = 0.5 * v
    return h + h * jnp.tanh(h)


def _mod_kernel(c_ref, w_ref, b_ref, o_ref):
    sc = _silu(c_ref[...])
    o_ref[...] = _dot(sc, w_ref[...]) + b_ref[...]


def _modulation(crows, mod_w, mod_b):
    depth, d, n = mod_w.shape
    tn = 1024
    return pl.pallas_call(
        _mod_kernel,
        grid=(depth, n // tn),
        in_specs=[
            pl.BlockSpec((8, d), lambda l, j: (0, 0)),
            pl.BlockSpec((None, d, tn), lambda l, j: (l, 0, j)),
            pl.BlockSpec((None, 1, tn), lambda l, j: (l, 0, j)),
        ],
        out_specs=pl.BlockSpec((None, 8, tn), lambda l, j: (l, 0, j)),
        out_shape=jax.ShapeDtypeStruct((depth, 8, n), F32),
        compiler_params=_cparams(("parallel", "parallel")),
        name="modulation",
    )(crows, mod_w, mod_b.reshape(depth, 1, n))


def _adaln_rows(x, shift, scale):
    ms = jnp.mean(x * x, axis=-1, keepdims=True)
    return (x * lax.rsqrt(ms + EPS)) * (1.0 + scale) + shift


def _rms_cols(v, gain_b, n):
    ss = jnp.sum(v * v, axis=0, keepdims=True)
    return v * lax.rsqrt(ss * (1.0 / n) + EPS) * gain_b


def _rope_cols(v, cos, sin, q):
    swap = jnp.concatenate([v[q:2 * q], v[0:q], v[3 * q:4 * q], v[2 * q:3 * q]], axis=0)
    return v * cos + swap * sin


def _projected_parts(x_ref, c_ref, mod_ref, w_in_ref, nx, groups):
    ts = x_ref.shape[0]
    ctx_rows = jnp.concatenate([c_ref[...]] * (ts // c_ref.shape[0]), axis=0)
    rows = jnp.where(pl.program_id(1) < nx, x_ref[...], ctx_rows)
    shift, scale = mod_ref[0:1, :], mod_ref[1:2, :]
    parts = ts // PROJ_PART
    cols = lambda i: slice(i * PROJ_PART, (i + 1) * PROJ_PART)

    def normed(i):
        return _adaln_rows(rows[cols(i)], shift, scale).astype(BF16)

    def project(h, g):
        r0, r1, _ = groups[g]
        return _dot_nt(w_in_ref[r0:r1, :], h)

    h = normed(0)
    pend = [project(h, g) for g in range(len(groups))]
    for i in range(parts):
        nxt = []
        if i + 1 < parts:
            h = normed(i + 1)
        for g in range(len(groups)):
            if i + 1 < parts:
                nxt.append(project(h, g))
            groups[g][2](pend[g], cols(i))
        pend = nxt


def _joint_specs(nx, ts, lc, d):
    return [
        pl.BlockSpec((None, ts, d), lambda i, t: (i, jnp.minimum(t, nx - 1), 0)),
        pl.BlockSpec((None, lc, d), lambda i, t: (i, 0, 0)),
        pl.BlockSpec((None, None, 2, d), lambda i, t: (i, jnp.where(t < nx, 0, 1), 0, 0)),
    ]


def _bcast_gain(g):
    return jnp.broadcast_to(g.astype(F32)[:, None], (g.shape[0], PROJ_PART))


def _proj_ab_kernel(x_ref, c_ref, mod_ref, w_in_ref, cqg_ref, ckvg_ref, w_uq_ref, w_ukv_ref,
                    qg_ref, kg_ref, gqg_ref, gkg_ref, c32_ref, s32_ref, c64_ref, s64_ref,
                    qa_ref, ka_ref, va_ref, qb_ref, kb_ref, vb_ref, sg_ref, *, nx):
    zpad = jnp.zeros((HEAD_PAD - MLA_QK, PROJ_PART), F32)
    z64 = jnp.zeros((GQA_DIM, PROJ_PART), BF16)
    q_scale = MLA_QK ** -0.5 * LOG2E
    g_scale = GQA_DIM ** -0.5 * LOG2E
    grp = GQA_HEADS // GQA_KV_HEADS

    def tail_mla(pp, cols):
        cq = pp[0:256]
        ckv = pp[256:512]
        kr = pp[512:544]
        cqn = _rms_cols(cq, cqg_ref[...], MLA_Q_RANK).astype(BF16)
        ckvn = _rms_cols(ckv, ckvg_ref[...], MLA_KV_RANK).astype(BF16)
        qa = _dot(w_uq_ref[...], cqn)
        kv = _dot(w_ukv_ref[...], ckvn)
        qg = qg_ref[...]
        kg = kg_ref[...]
        c32 = c32_ref[:, cols]
        s32 = s32_ref[:, cols]
        kr_ss = jnp.sum(kr * kr, axis=0, keepdims=True)
        for hd in range(MLA_HEADS):
            qh = _rms_cols(qa[hd * MLA_QK:(hd + 1) * MLA_QK], qg, MLA_QK)
            qr = _rope_cols(qh[MLA_NOPE:], c32, s32, MLA_ROPE // 4)
            qa_ref[hd, 0:MLA_NOPE, cols] = (qh[0:MLA_NOPE] * q_scale).astype(BF16)
            qa_ref[hd, MLA_NOPE:MLA_QK, cols] = (qr * q_scale).astype(BF16)
            qa_ref[hd, MLA_QK:HEAD_PAD, cols] = zpad.astype(BF16)
            kn = kv[hd * 128:hd * 128 + MLA_NOPE]
            ss = jnp.sum(kn * kn, axis=0, keepdims=True) + kr_ss
            rs = lax.rsqrt(ss * (1.0 / MLA_QK) + EPS)
            krn = _rope_cols(kr * rs * kg[MLA_NOPE:], c32, s32, MLA_ROPE // 4)
            k_cols = jnp.concatenate([kn * rs * kg[0:MLA_NOPE], krn, zpad], axis=0)
            ka_ref[hd, cols, :] = k_cols.T.astype(BF16)
            va_ref[hd, :, cols] = kv[hd * 128 + MLA_NOPE:(hd + 1) * 128].astype(BF16)

    def tail_gqa(pp, cols):
        gq = pp[0:512]
        gk = pp[512:640]
        gv = pp[640:768]
        c64 = c64_ref[:, cols]
        s64 = s64_ref[:, cols]
        gqg = gqg_ref[...]
        gkg = gkg_ref[...]
        for hd in range(GQA_HEADS):
            qh = _rms_cols(gq[hd * GQA_DIM:(hd + 1) * GQA_DIM], gqg, GQA_DIM)
            qh = (_rope_cols(qh, c64, s64, GQA_DIM // 4) * g_scale).astype(BF16)
            for kvh in range(GQA_KV_HEADS):
                qb_ref[hd, kvh * GQA_DIM:(kvh + 1) * GQA_DIM, cols] = qh if kvh == hd // grp else z64
        k_heads = []
        for kvh in range(GQA_KV_HEADS):
            kh = _rms_cols(gk[kvh * GQA_DIM:(kvh + 1) * GQA_DIM], gkg, GQA_DIM)
            k_heads.append(_rope_cols(kh, c64, s64, GQA_DIM // 4))
            vb_ref[kvh, :, cols] = gv[kvh * GQA_DIM:(kvh + 1) * GQA_DIM].astype(BF16)
        kb_ref[cols, :] = jnp.concatenate(k_heads, axis=0).T.astype(BF16)

    def tail_gate(r0):
        def tail(pp, cols):
            sg_ref[r0:r0 + GATE_ROWS, cols] = _silu(pp).astype(BF16)
        return tail

    g0 = 544 + 768
    groups = [(0, 544, tail_mla), (544, g0, tail_gqa)]
    groups += [(g0 + r, g0 + r + GATE_ROWS, tail_gate(r)) for r in range(0, D_MODEL, GATE_ROWS)]
    _projected_parts(x_ref, c_ref, mod_ref, w_in_ref, nx, groups)


def _proj_ab(x, ctx, mods, w_in_t, w_uq_t, w_ukv_t, gains, tabs, ts):
    b, s, d = x.shape
    lc = ctx.shape[1]
    nx = s // ts
    n = s + ts
    cqg, ckvg, qg, kg, gqg, gkg = [_bcast_gain(g) for g in gains]
    c32, s32, c64, s64 = tabs
    full = lambda a: pl.BlockSpec(a.shape, lambda i, t: (0,) * a.ndim)
    tab = lambda a: pl.BlockSpec((a.shape[0], ts), lambda i, t: (0, t))
    feat4 = lambda h, r: (jax.ShapeDtypeStruct((b, h, r, n), BF16),
                          pl.BlockSpec((None, h, r, ts), lambda i, t: (i, 0, 0, t)))
    rows4 = lambda h: (jax.ShapeDtypeStruct((b, h, n, HEAD_PAD), BF16),
                       pl.BlockSpec((None, h, ts, HEAD_PAD), lambda i, t: (i, 0, t, 0)))
    outs = [
        feat4(MLA_HEADS, HEAD_PAD),
        rows4(MLA_HEADS),
        feat4(MLA_HEADS, MLA_V),
        feat4(GQA_HEADS, HEAD_PAD),
        (jax.ShapeDtypeStruct((b, n, HEAD_PAD), BF16),
         pl.BlockSpec((None, ts, HEAD_PAD), lambda i, t: (i, t, 0))),
        feat4(GQA_KV_HEADS, GQA_DIM),
        (jax.ShapeDtypeStruct((b, D_MODEL, n), BF16),
         pl.BlockSpec((None, D_MODEL, ts), lambda i, t: (i, 0, t))),
    ]
    return pl.pallas_call(
        functools.partial(_proj_ab_kernel, nx=nx),
        grid=(b, nx + 1),
        in_specs=_joint_specs(nx, ts, lc, d) + [
            full(w_in_t), full(cqg), full(ckvg), full(w_uq_t), full(w_ukv_t),
            full(qg), full(kg), full(gqg), full(gkg),
            tab(c32), tab(s32), tab(c64), tab(s64),
        ],
        out_specs=[o[1] for o in outs],
        out_shape=[o[0] for o in outs],
        compiler_params=_cparams(("parallel", "parallel")),
        name="proj_ab",
    )(x, ctx, mods, w_in_t, cqg, ckvg, w_uq_t, w_ukv_t, qg, kg, gqg, gkg, c32, s32, c64, s64)


P_LIMIT = 2.0 ** 100
FIRST_REF_KEYS = 128


def _stream_kernel(q_ref, k_ref, v_ref, o_ref, m_sc, acc_sc, pv_sc, *, heads, kc, ahead):
    j = pl.program_id(2)
    tk = k_ref.shape[1]
    tq = q_ref.shape[2]
    dv = o_ref.shape[1]
    nc = tk // kc
    gk = k_ref.shape[0]
    gv = v_ref.shape[0]
    cur = lax.rem(j, 2)
    nxt = 1 - cur

    @pl.when(j == 0)
    def _():
        for g in range(heads):
            s0 = _dot(k_ref[g * gk // heads, 0:FIRST_REF_KEYS, :], q_ref[g])
            m_sc[0, g] = jnp.max(s0, axis=0, keepdims=True)
        acc_sc[0] = jnp.zeros(acc_sc.shape[1:], F32)

    units = [(g, c) for g in range(heads) for c in range(nc)]
    pending = {}

    def emit_scores(u):
        g, c = u
        pending[u] = _dot(k_ref[g * gk // heads, c * kc:(c + 1) * kc, :], q_ref[g])

    worst = None
    issued = 0
    pmx = None
    lsum = None
    for i, (g, c) in enumerate(units):
        while issued <= min(i + ahead, len(units) - 1):
            emit_scores(units[issued])
            issued += 1
        s = pending.pop((g, c))
        m_ref = m_sc[cur, g]
        mb = jnp.broadcast_to(m_ref, (8, tq))
        p32 = jnp.exp2(s.reshape(kc // 8, 8, tq) - mb[None])
        lpart = jnp.sum(p32, axis=0)
        lsum = lpart if c == 0 else lsum + lpart
        p = p32.reshape(kc, tq).astype(BF16)
        part = jnp.max(p.reshape(kc // 16, 16, tq), axis=0)
        pmx = part if c == 0 else jnp.maximum(pmx, part)
        d = _dot(v_ref[g * gv // heads, :, c * kc:(c + 1) * kc], p)
        if c == 0:
            pv_sc[g] = d
        else:
            pv_sc[g] += d
        if c == nc - 1:
            pm = jnp.maximum(jnp.max(pmx.astype(F32), axis=0, keepdims=True), 1.0)
            worst = pm if worst is None else jnp.maximum(worst, pm)
            m_sc[nxt, g] = m_ref + jnp.log2(pm)
            rp = 1.0 / pm
            acc_sc[nxt, g, 0:dv, :] = (acc_sc[cur, g, 0:dv, :] + pv_sc[g]) * rp
            acc_sc[nxt, g, dv:dv + 1, :] = (acc_sc[cur, g, dv:dv + 1, :] + jnp.sum(lsum, axis=0, keepdims=True)) * rp

    redo = jnp.logical_not(jnp.max(worst) <= P_LIMIT)

    @pl.when(redo)
    def _():
        def head(g, carry):
            q = q_ref[g]
            gki = g * gk // heads
            gvi = g * gv // heads
            m_old = m_sc[cur, g]

            def cmax(c, mx):
                s = _dot(k_ref[gki, pl.ds(pl.multiple_of(c * kc, kc), kc), :], q)
                return jnp.maximum(mx, jnp.max(s, axis=0, keepdims=True))

            m_new = lax.fori_loop(0, nc, cmax, m_old)

            def cacc(c, st):
                pv, l = st
                ks = pl.ds(pl.multiple_of(c * kc, kc), kc)
                p = jnp.exp2(_dot(k_ref[gki, ks, :], q) - m_new)
                return pv + _dot(v_ref[gvi, :, ks], p.astype(BF16)), l + jnp.sum(p, axis=0, keepdims=True)

            pv, l = lax.fori_loop(0, nc, cacc, (jnp.zeros((dv, tq), F32), jnp.zeros((1, tq), F32)))
            alpha = jnp.exp2(m_old - m_new)
            m_sc[nxt, g] = m_new
            acc_sc[nxt, g, 0:dv, :] = acc_sc[cur, g, 0:dv, :] * alpha + pv
            acc_sc[nxt, g, dv:dv + 1, :] = acc_sc[cur, g, dv:dv + 1, :] * alpha + l
            return carry

        lax.fori_loop(0, heads, head, 0)

    @pl.when(j == pl.num_programs(2) - 1)
    def _():
        for g in range(heads):
            o_ref[g] = (acc_sc[nxt, g, 0:dv, :] / acc_sc[nxt, g, dv:dv + 1, :]).astype(o_ref.dtype)


def _stream_attn(q_t, k, v_t, s, n, *, tq, tk, kc, ahead, name):
    b, hq = q_t.shape[0], q_t.shape[1]
    hk = k.shape[1]
    hv, dv = v_t.shape[1], v_t.shape[2]
    return pl.pallas_call(
        functools.partial(_stream_kernel, heads=hq, kc=kc, ahead=ahead),
        grid=(b, s // tq, n // tk),
        in_specs=[
            pl.BlockSpec((None, hq, HEAD_PAD, tq), lambda i, qi, j: (i, 0, 0, qi)),
            pl.BlockSpec((None, hk, tk, HEAD_PAD), lambda i, qi, j: (i, 0, j, 0)),
            pl.BlockSpec((None, hv, dv, tk), lambda i, qi, j: (i, 0, 0, j)),
        ],
        out_specs=pl.BlockSpec((None, hq, dv, tq), lambda i, qi, j: (i, 0, 0, qi)),
        out_shape=jax.ShapeDtypeStruct((b, hq, dv, s), BF16),
        scratch_shapes=[
            pltpu.VMEM((2, hq, 1, tq), F32),
            pltpu.VMEM((2, hq, dv + STAT_ROWS, tq), F32),
            pltpu.VMEM((hq, dv, tq), F32),
        ],
        compiler_params=_cparams(("parallel", "parallel", "arbitrary")),
        name=name,
    )(q_t, k, v_t)


def _ctx_attn_kernel(q_ref, k_ref, v_ref, o_ref):
    heads, gk, gv = q_ref.shape[0], k_ref.shape[0], v_ref.shape[0]
    for g in range(heads):
        s = _dot(k_ref[g * gk // heads], q_ref[g])
        p = jnp.exp2(s - jnp.max(s, axis=0, keepdims=True))
        den = jnp.sum(p, axis=0, keepdims=True)
        o_ref[g] = (_dot(v_ref[g * gv // heads], p.astype(BF16)) / den).astype(o_ref.dtype)


def _ctx_attn(q_t, k, v_t, s, lc, name):
    b, hq = q_t.shape[0], q_t.shape[1]
    hk = k.shape[1]
    hv, dv = v_t.shape[1], v_t.shape[2]
    off = s // lc
    return pl.pallas_call(
        _ctx_attn_kernel,
        grid=(b,),
        in_specs=[
            pl.BlockSpec((None, hq, HEAD_PAD, lc), lambda i: (i, 0, 0, off)),
            pl.BlockSpec((None, hk, lc, HEAD_PAD), lambda i: (i, 0, off, 0)),
            pl.BlockSpec((None, hv, dv, lc), lambda i: (i, 0, 0, off)),
        ],
        out_specs=pl.BlockSpec((None, hq, dv, lc), lambda i: (i, 0, 0, 0)),
        out_shape=jax.ShapeDtypeStruct((b, hq, dv, lc), BF16),
        compiler_params=_cparams(("parallel",)),
        name=name,
    )(q_t, k, v_t)


def _out_ab_kernel(x_ref, oa_ref, ob_ref, sg_ref, gate_ref, w_ref, o_ref):
    m = jnp.concatenate([oa_ref[...], ob_ref[...]], axis=0) * sg_ref[...]
    y_t = _dot(w_ref[...], m)
    o_ref[...] = x_ref[...] + gate_ref[...] * y_t.T


def _out_ab(x, oa_t, ob_t, sg_t, sg_off, gate, w_out_t, ts):
    b, n, d = x.shape
    half = oa_t.shape[1]
    return pl.pallas_call(
        _out_ab_kernel,
        grid=(b, n // ts),
        in_specs=[
            pl.BlockSpec((None, ts, d), lambda i, t: (i, t, 0)),
            pl.BlockSpec((None, half, ts), lambda i, t: (i, 0, t)),
            pl.BlockSpec((None, d - half, ts), lambda i, t: (i, 0, t)),
            pl.BlockSpec((None, d, ts), lambda i, t: (i, 0, t + sg_off)),
            pl.BlockSpec((None, 1, d), lambda i, t: (i, 0, 0)),
            pl.BlockSpec((d, d), lambda i, t: (0, 0)),
        ],
        out_specs=pl.BlockSpec((None, ts, d), lambda i, t: (i, t, 0)),
        out_shape=jax.ShapeDtypeStruct((b, n, d), F32),
        compiler_params=_cparams(("parallel", "parallel")),
        name="out_ab",
    )(x, oa_t, ob_t, sg_t, gate, w_out_t)


def _proj_cd_kernel(x_ref, c_ref, mod_ref, w_in_ref, qg_ref, kg_ref, c64_ref, s64_ref,
                    q_ref, k_ref, v_ref, gb_ref, u_ref, sg_ref, *, nx):
    scale = WIN_DIM ** -0.5 * LOG2E
    z64 = jnp.zeros((WIN_DIM, PROJ_PART), BF16)
    grp = WIN_HEADS // WIN_KV_HEADS

    def tail_q(pp, cols):
        c64 = c64_ref[:, cols]
        s64 = s64_ref[:, cols]
        qg = qg_ref[...]
        for hd in range(WIN_HEADS):
            qh = _rms_cols(pp[hd * WIN_DIM:(hd + 1) * WIN_DIM], qg, WIN_DIM)
            qh = (_rope_cols(qh, c64, s64, WIN_DIM // 4) * scale).astype(BF16)
            for kvh in range(WIN_KV_HEADS):
                q_ref[hd, kvh * WIN_DIM:(kvh + 1) * WIN_DIM, cols] = qh if kvh == hd // grp else z64

    def tail_kv(pp, cols):
        c64 = c64_ref[:, cols]
        s64 = s64_ref[:, cols]
        kg = kg_ref[...]
        k_heads = []
        for kvh in range(WIN_KV_HEADS):
            kh = _rms_cols(pp[kvh * WIN_DIM:(kvh + 1) * WIN_DIM], kg, WIN_DIM)
            k_heads.append(_rope_cols(kh, c64, s64, WIN_DIM // 4))
            v_ref[kvh, :, cols] = pp[128 + kvh * WIN_DIM:128 + (kvh + 1) * WIN_DIM].astype(BF16)
        k_ref[cols, :] = jnp.concatenate(k_heads, axis=0).T.astype(BF16)
        gb_ref[:, cols] = pp[256:768].astype(BF16)

    def tail_conv(pp, cols):
        u_ref[:, cols] = (pp[0:CONV_CH] * pp[CONV_CH:2 * CONV_CH]).astype(BF16)

    def tail_gate(r0):
        def tail(pp, cols):
            sg_ref[r0:r0 + GATE_ROWS, cols] = _silu(pp).astype(BF16)
        return tail

    g0 = 2304
    groups = [(0, 512, tail_q), (512, 1280, tail_kv), (1280, g0, tail_conv)]
    groups += [(g0 + r, g0 + r + GATE_ROWS, tail_gate(r)) for r in range(0, D_MODEL, GATE_ROWS)]
    _projected_parts(x_ref, c_ref, mod_ref, w_in_ref, nx, groups)


def _proj_cd(x, ctx, mods, w_in_t, gains, tabs, ts):
    b, s, d = x.shape
    lc = ctx.shape[1]
    nx = s // ts
    n = s + ts
    qg, kg = [_bcast_gain(g) for g in gains]
    c64, s64 = tabs
    full = lambda a: pl.BlockSpec(a.shape, lambda i, t: (0,) * a.ndim)
    tab = lambda a: pl.BlockSpec((a.shape[0], ts), lambda i, t: (0, t))
    feat3 = lambda r: (jax.ShapeDtypeStruct((b, r, n), BF16), pl.BlockSpec((None, r, ts), lambda i, t: (i, 0, t)))
    outs = [
        (jax.ShapeDtypeStruct((b, WIN_HEADS, HEAD_PAD, n), BF16),
         pl.BlockSpec((None, WIN_HEADS, HEAD_PAD, ts), lambda i, t: (i, 0, 0, t))),
        (jax.ShapeDtypeStruct((b, n, HEAD_PAD), BF16),
         pl.BlockSpec((None, ts, HEAD_PAD), lambda i, t: (i, t, 0))),
        (jax.ShapeDtypeStruct((b, WIN_KV_HEADS, WIN_DIM, n), BF16),
         pl.BlockSpec((None, WIN_KV_HEADS, WIN_DIM, ts), lambda i, t: (i, 0, 0, t))),
        feat3(CONV_CH),
        feat3(CONV_CH),
        feat3(D_MODEL),
    ]
    return pl.pallas_call(
        functools.partial(_proj_cd_kernel, nx=nx),
        grid=(b, nx + 1),
        in_specs=_joint_specs(nx, ts, lc, d) + [full(w_in_t), full(qg), full(kg), tab(c64), tab(s64)],
        out_specs=[o[1] for o in outs],
        out_shape=[o[0] for o in outs],
        compiler_params=_cparams(("parallel", "parallel")),
        name="proj_cd",
    )(x, ctx, mods, w_in_t, qg, kg, c64, s64)


def _win_kernel(q_ref, kp_ref, kc_ref, kn_ref, vp_ref, vcur_ref, vn_ref, kctx_ref, vctx_ref, bias_ref, sink_ref,
                o_ref, *, sub):
    heads, tq = q_ref.shape[0], q_ref.shape[2]
    gv = vcur_ref.shape[0]
    nsub = tq // sub
    qi = pl.program_id(1)
    edge_p = jnp.where(qi > 0, 0.0, NEG_INF)
    edge_n = jnp.where(qi < pl.num_programs(1) - 1, 0.0, NEG_INF)
    bias_mid = bias_ref[...]
    bias_first = jnp.concatenate([bias_ref[0:WINDOW, :] + edge_p, bias_ref[WINDOW:, :]], axis=0)
    bias_last = jnp.concatenate([bias_ref[0:sub + WINDOW, :], bias_ref[sub + WINDOW:, :] + edge_n], axis=0)
    if nsub == 1:
        bias_first = bias_last = jnp.concatenate(
            [bias_ref[0:WINDOW, :] + edge_p, bias_ref[WINDOW:sub + WINDOW, :], bias_ref[sub + WINDOW:, :] + edge_n],
            axis=0)
    k_loc = jnp.concatenate([kp_ref[...], kc_ref[...], kn_ref[...]], axis=0)
    v_loc = [jnp.concatenate([vp_ref[h], vcur_ref[h], vn_ref[h]], axis=1) for h in range(gv)]
    kctx = kctx_ref[...]
    units = [(g, t) for g in range(heads) for t in range(nsub)]
    pending = {}

    def emit_scores(u):
        g, t = u
        q = q_ref[g, :, t * sub:(t + 1) * sub]
        bias = bias_first if t == 0 else (bias_last if t == nsub - 1 else bias_mid)
        pending[u] = (_dot(k_loc[t * sub:(t + 1) * sub + 2 * WINDOW], q) + bias, _dot(kctx, q))

    issued = 0
    for i, (g, t) in enumerate(units):
        while issued <= min(i + WIN_AHEAD, len(units) - 1):
            emit_scores(units[issued])
            issued += 1
        s_l, s_x = pending.pop((g, t))
        sk = sink_ref[g] * LOG2E
        m = jnp.maximum(jnp.maximum(jnp.max(s_l, axis=0, keepdims=True), jnp.max(s_x, axis=0, keepdims=True)), sk)
        p_l = jnp.exp2(s_l - m)
        p_x = jnp.exp2(s_x - m)
        den = jnp.sum(p_l, axis=0, keepdims=True) + jnp.sum(p_x, axis=0, keepdims=True) + jnp.exp2(sk - m)
        h = g * gv // heads
        acc = (_dot(v_loc[h][:, t * sub:(t + 1) * sub + 2 * WINDOW], p_l.astype(BF16))
               + _dot(vctx_ref[h], p_x.astype(BF16)))
        o_ref[g, :, t * sub:(t + 1) * sub] = (acc / den).astype(o_ref.dtype)


def _win_attn(q_t, k, v_t, sink, s, lc, *, tq, sub):
    b, hq = q_t.shape[0], q_t.shape[1]
    hv, dv = v_t.shape[1], v_t.shape[2]
    r = tq // WINDOW
    nb = s // WINDOW
    prev = lambda qi: jnp.maximum(qi * r - 1, 0)
    nxt = lambda qi: jnp.minimum((qi + 1) * r, nb - 1)
    kpos = jnp.arange(sub + 2 * WINDOW, dtype=jnp.int32)[:, None] - WINDOW
    qpos = jnp.arange(sub, dtype=jnp.int32)[None, :]
    bias = jnp.where(jnp.abs(qpos - kpos) <= WINDOW, 0.0, NEG_INF).astype(F32)
    return pl.pallas_call(
        functools.partial(_win_kernel, sub=sub),
        grid=(b, s // tq),
        in_specs=[
            pl.BlockSpec((None, hq, HEAD_PAD, tq), lambda i, qi: (i, 0, 0, qi)),
            pl.BlockSpec((None, WINDOW, HEAD_PAD), lambda i, qi: (i, prev(qi), 0)),
            pl.BlockSpec((None, tq, HEAD_PAD), lambda i, qi: (i, qi, 0)),
            pl.BlockSpec((None, WINDOW, HEAD_PAD), lambda i, qi: (i, nxt(qi), 0)),
            pl.BlockSpec((None, hv, dv, WINDOW), lambda i, qi: (i, 0, 0, prev(qi))),
            pl.BlockSpec((None, hv, dv, tq), lambda i, qi: (i, 0, 0, qi)),
            pl.BlockSpec((None, hv, dv, WINDOW), lambda i, qi: (i, 0, 0, nxt(qi))),
            pl.BlockSpec((None, lc, HEAD_PAD), lambda i, qi: (i, s // lc, 0)),
            pl.BlockSpec((None, hv, dv, lc), lambda i, qi: (i, 0, 0, s // lc)),
            pl.BlockSpec(bias.shape, lambda i, qi: (0, 0)),
            pl.BlockSpec(memory_space=pltpu.SMEM),
        ],
        out_specs=pl.BlockSpec((None, hq, dv, tq), lambda i, qi: (i, 0, 0, qi)),
        out_shape=jax.ShapeDtypeStruct((b, hq, dv, s), BF16),
        compiler_params=_cparams(("parallel", "parallel")),
        name="win_attn",
    )(q_t, k, k, k, v_t, v_t, v_t, k, v_t, bias, sink)


def _out_cd_kernel(x_ref, oc_ref, gb_ref, up_ref, u_ref, un_ref, cw_ref, sg_ref, gate_ref, w_ref, o_ref, *, ts):
    t = pl.program_id(1)
    nt = pl.num_programs(1)
    u = u_ref[...].astype(F32)
    lane = lax.broadcasted_iota(jnp.int32, (CONV_CH, ts), 1)
    halo_l = pltpu.roll(up_ref[...].astype(F32), 1, axis=1)
    halo_l = jnp.where(t > 0, halo_l, 0.0)
    halo_r = pltpu.roll(un_ref[...].astype(F32), LANES - 1, axis=1)
    halo_r = jnp.where(t < nt - 1, halo_r, 0.0)
    reps = ts // LANES
    um = jnp.where(lane == 0, jnp.tile(halo_l, (1, reps)), pltpu.roll(u, 1, axis=1))
    up = jnp.where(lane == ts - 1, jnp.tile(halo_r, (1, reps)), pltpu.roll(u, ts - 1, axis=1))
    w0 = jnp.tile(cw_ref[0], (1, reps))
    w1 = jnp.tile(cw_ref[1], (1, reps))
    w2 = jnp.tile(cw_ref[2], (1, reps))
    od = gb_ref[...].astype(F32) * (w0 * um + w1 * u + w2 * up)
    sg = sg_ref[...]
    half = D_MODEL - CONV_CH
    m = jnp.concatenate([oc_ref[...] * sg[0:half], od.astype(BF16) * sg[half:]], axis=0)
    y_t = _dot(w_ref[...], m)
    o_ref[...] = x_ref[...] + gate_ref[...] * y_t.T


def _out_cd(x, oc_t, gb_t, u_t, conv_wb, sg_t, gate, w_out_t, ts):
    b, n, d = x.shape
    r = ts // LANES
    nb = n // LANES
    return pl.pallas_call(
        functools.partial(_out_cd_kernel, ts=ts),
        grid=(b, n // ts),
        in_specs=[
            pl.BlockSpec((None, ts, d), lambda i, t: (i, t, 0)),
            pl.BlockSpec((None, d - CONV_CH, ts), lambda i, t: (i, 0, t)),
            pl.BlockSpec((None, CONV_CH, ts), lambda i, t: (i, 0, t)),
            pl.BlockSpec((None, CONV_CH, LANES), lambda i, t: (i, 0, jnp.maximum(t * r - 1, 0))),
            pl.BlockSpec((None, CONV_CH, ts), lambda i, t: (i, 0, t)),
            pl.BlockSpec((None, CONV_CH, LANES), lambda i, t: (i, 0, jnp.minimum((t + 1) * r, nb - 1))),
            pl.BlockSpec((CONV_K, CONV_CH, LANES), lambda i, t: (0, 0, 0)),
            pl.BlockSpec((None, d, ts), lambda i, t: (i, 0, t)),
            pl.BlockSpec((None, 1, d), lambda i, t: (i, 0, 0)),
            pl.BlockSpec((d, d), lambda i, t: (0, 0)),
        ],
        out_specs=pl.BlockSpec((None, ts, d), lambda i, t: (i, t, 0)),
        out_shape=jax.ShapeDtypeStruct((b, n, d), F32),
        compiler_params=_cparams(("parallel", "parallel")),
        name="out_cd",
    )(x, oc_t, gb_t, u_t, u_t, u_t, conv_wb, sg_t, gate, w_out_t)


def _rope_tables(s, tail, dd):
    q = dd // 4
    rows = s // GRID_W
    row = jnp.repeat(jnp.arange(rows, dtype=F32), GRID_W)
    col = jnp.tile(jnp.arange(GRID_W, dtype=F32), rows)
    inv = ROPE_BASE ** (-jnp.arange(q, dtype=F32) / q)
    ar = inv[:, None] * row[None, :]
    ac = inv[:, None] * col[None, :]
    cos = jnp.concatenate([jnp.cos(ar), jnp.cos(ar), jnp.cos(ac), jnp.cos(ac)], axis=0)
    sin = jnp.concatenate([-jnp.sin(ar), jnp.sin(ar), -jnp.sin(ac), jnp.sin(ac)], axis=0)
    return (jnp.concatenate([cos, jnp.ones((dd, tail), F32)], axis=1),
            jnp.concatenate([sin, jnp.zeros((dd, tail), F32)], axis=1))


def _tile_rows(n, cap):
    t = cap
    while n % t:
        t //= 2
    return t


def _key_tile(n, cap, step):
    t = (cap // step) * step
    while n % t:
        t -= step
    return t


def kernel(x, c, ctx, c_ctx, mod_w, mod_b, ab_w_in, ab_w_out, mla_cq_gain, mla_ckv_gain, mla_w_uq, mla_w_ukv,
           mla_q_gain, mla_k_gain, gqa_q_gain, gqa_k_gain, cd_w_in, cd_w_out, win_q_gain, win_k_gain,
           win_sink, conv_w):
    b, s, d = x.shape
    lc = ctx.shape[1]
    ts_p = _tile_rows(s, 4 * PROJ_PART)
    assert ts_p % lc == 0 and ts_p % PROJ_PART == 0 and s % lc == 0
    ts_x = _tile_rows(s, 1024)
    ts_c = _tile_rows(lc, 256)
    tq = _tile_rows(s, 512)
    n = s + lc
    tk = _key_tile(n, 3328, LANES)
    kc = 256 if tk % 256 == 0 else tk
    ahead = 2
    tq_w = _tile_rows(s, 1024)

    crows = jnp.zeros((8, d), F32).at[0:b].set(c).at[b].set(c_ctx)
    mods = _modulation(crows, mod_w, mod_b)

    def split_mod(layer):
        m = mods[layer]
        shift, scale, gate = m[:, 0:d], m[:, d:2 * d], m[:, 2 * d:3 * d]
        lat = jnp.stack([shift[0:b], scale[0:b]], axis=1)
        con = jnp.broadcast_to(jnp.stack([shift[b], scale[b]], axis=0)[None], (b, 2, d))
        gate_x = gate[0:b][:, None, :]
        gate_c = jnp.broadcast_to(gate[b][None, None, :], (b, 1, d))
        return jnp.stack([lat, con], axis=1), gate_x, gate_c

    tabs32 = _rope_tables(s, ts_p, MLA_ROPE)
    tabs64 = _rope_tables(s, ts_p, GQA_DIM)

    mod4, gate_x, gate_c = split_mod(0)
    w_in_t = ab_w_in[0].T.astype(BF16)
    w_uq_t = mla_w_uq[0].T.astype(BF16)
    w_ukv_t = mla_w_ukv[0].T.astype(BF16)
    w_out_t = ab_w_out[0].T.astype(BF16)
    gains = (mla_cq_gain[0], mla_ckv_gain[0], mla_q_gain[0], mla_k_gain[0], gqa_q_gain[0], gqa_k_gain[0])
    qa, ka, va, qb, kb, vb, sg = _proj_ab(x, ctx, mod4, w_in_t, w_uq_t, w_ukv_t, gains, tabs32 + tabs64, ts_p)
    kb = kb[:, None]
    oa = _stream_attn(qa, ka, va, s, n, tq=tq, tk=tk, kc=kc, ahead=ahead, name="flash_mla")
    ob = _stream_attn(qb, kb, vb, s, n, tq=tq, tk=tk, kc=kc, ahead=ahead, name="flash_gqa")
    x1 = _out_ab(x, oa.reshape(b, -1, s), ob.reshape(b, -1, s), sg, 0, gate_x, w_out_t, ts_x)
    oac = _ctx_attn(qa, ka, va, s, lc, "ctx_mla")
    obc = _ctx_attn(qb, kb, vb, s, lc, "ctx_gqa")
    xc1 = _out_ab(ctx, oac.reshape(b, -1, lc), obc.reshape(b, -1, lc), sg, s // ts_c, gate_c, w_out_t, ts_c)

    mod4, gate_x, gate_c = split_mod(1)
    w_in_t = cd_w_in[0].T.astype(BF16)
    w_out_t = cd_w_out[0].T.astype(BF16)
    q, k, v, gb, u, sg = _proj_cd(x1, xc1, mod4, w_in_t, (win_q_gain[0], win_k_gain[0]), tabs64, ts_p)
    oc = _win_attn(q, k, v, win_sink[0].astype(F32), s, lc, tq=tq_w, sub=min(tq_w, 256))
    conv_wb = jnp.broadcast_to(conv_w[0].astype(F32)[:, :, None], (CONV_K, CONV_CH, LANES))
    return _out_cd(x1, oc.reshape(b, -1, s), gb, u, conv_wb, sg, gate_x, w_out_t, ts_x)
```

```python
import functools
import math

import jax
import jax.numpy as jnp
from jax import lax
from jax.experimental import pallas as pl
from jax.experimental.pallas import tpu as pltpu

D_MODEL = 1024
GRID_W = 64
ROPE_BASE = 10000.0
EPS = 1e-6
NEG_INF = -1e30
WINDOW = 128

MLA_HEADS = 8
MLA_NOPE = 64
MLA_ROPE = 32
MLA_V = 64
MLA_QK = MLA_NOPE + MLA_ROPE
MLA_Q_RANK = 256
MLA_KV_RANK = 256
GQA_HEADS = 8
GQA_KV_HEADS = 2
GQA_DIM = 64
WIN_HEADS = 8
WIN_KV_HEADS = 2
WIN_DIM = 64
CONV_CH = 512
CONV_K = 3

LANES = 128
HEAD_PAD = 128
STAT_ROWS = 8
PROJ_PART = 256
GATE_ROWS = 256
WIN_AHEAD = 2
LOG2E = math.log2(math.e)
VMEM_LIMIT_BYTES = 56 * 1024 * 1024

BF16 = jnp.bfloat16
F32 = jnp.float32


def _cparams(sem):
    return pltpu.CompilerParams(dimension_semantics=sem, vmem_limit_bytes=VMEM_LIMIT_BYTES)


def _dot(a, b):
    return jnp.dot(a, b, preferred_element_type=F32)


def _dot_nt(a, b):
    return lax.dot_general(a, b, (((1,), (1,)), ((), ())), preferred_element_type=F32)


def _silu(v):
    h = 0.5 * v
    return h + h * jnp.tanh(h)


def _mod_kernel(c_ref, w_ref, b_ref, o_ref):
    sc = _silu(c_ref[...])
    o_ref[...] = _dot(sc, w_ref[...]) + b_ref[...]


def _modulation(crows, mod_w, mod_b):
    depth, d, n = mod_w.shape
    tn = 1024
    return pl.pallas_call(
        _mod_kernel,
        grid=(depth, n // tn),
        in_specs=[
            pl.BlockSpec((8, d), lambda l, j: (0, 0)),
            pl.BlockSpec((None, d, tn), lambda l, j: (l, 0, j)),
            pl.BlockSpec((None, 1, tn), lambda l, j: (l, 0, j)),
        ],
        out_specs=pl.BlockSpec((None, 8, tn), lambda l, j: (l, 0, j)),
        out_shape=jax.ShapeDtypeStruct((depth, 8, n), F32),
        compiler_params=_cparams(("parallel", "parallel")),
        name="modulation",
    )(crows, mod_w, mod_b.reshape(depth, 1, n))


def _adaln_rows(x, shift, scale):
    ms = jnp.mean(x * x, axis=-1, keepdims=True)
    return (x * lax.rsqrt(ms + EPS)) * (1.0 + scale) + shift


def _rms_cols(v, gain_b, n):
    ss = jnp.sum(v * v, axis=0, keepdims=True)
    return v * lax.rsqrt(ss * (1.0 / n) + EPS) * gain_b


def _rope_cols(v, cos, sin, q):
    swap = jnp.concatenate([v[q:2 * q], v[0:q], v[3 * q:4 * q], v[2 * q:3 * q]], axis=0)
    return v * cos + swap * sin


def _projected_parts(x_ref, c_ref, mod_ref, w_in_ref, nx, groups):
    ts = x_ref.shape[0]
    ctx_rows = jnp.concatenate([c_ref[...]] * (ts // c_ref.shape[0]), axis=0)
    rows = jnp.where(pl.program_id(1) < nx, x_ref[...], ctx_rows)
    shift, scale = mod_ref[0:1, :], mod_ref[1:2, :]
    parts = ts // PROJ_PART
    cols = lambda i: slice(i * PROJ_PART, (i + 1) * PROJ_PART)

    def normed(i):
        return _adaln_rows(rows[cols(i)], shift, scale).astype(BF16)

    def project(h, g):
        r0, r1, _ = groups[g]
        return _dot_nt(w_in_ref[r0:r1, :], h)

    h = normed(0)
    pend = [project(h, g) for g in range(len(groups))]
    for i in range(parts):
        nxt = []
        if i + 1 < parts:
            h = normed(i + 1)
        for g in range(len(groups)):
            if i + 1 < parts:
                nxt.append(project(h, g))
            groups[g][2](pend[g], cols(i))
        pend = nxt


def _joint_specs(nx, ts, lc, d):
    return [
        pl.BlockSpec((None, ts, d), lambda i, t: (i, jnp.minimum(t, nx - 1), 0)),
        pl.BlockSpec((None, lc, d), lambda i, t: (i, 0, 0)),
        pl.BlockSpec((None, None, 2, d), lambda i, t: (i, jnp.where(t < nx, 0, 1), 0, 0)),
    ]


def _bcast_gain(g):
    return jnp.broadcast_to(g.astype(F32)[:, None], (g.shape[0], PROJ_PART))


def _proj_ab_kernel(x_ref, c_ref, mod_ref, w_in_ref, cqg_ref, ckvg_ref, w_uq_ref, w_ukv_ref,
                    qg_ref, kg_ref, gqg_ref, gkg_ref, c32_ref, s32_ref, c64_ref, s64_ref,
                    qa_ref, ka_ref, va_ref, qb_ref, kb_ref, vb_ref, sg_ref, *, nx):
    zpad = jnp.zeros((HEAD_PAD - MLA_QK, PROJ_PART), F32)
    z64 = jnp.zeros((GQA_DIM, PROJ_PART), BF16)
    q_scale = MLA_QK ** -0.5 * LOG2E
    g_scale = GQA_DIM ** -0.5 * LOG2E
    grp = GQA_HEADS // GQA_KV_HEADS

    def tail_mla(pp, cols):
        cq = pp[0:256]
        ckv = pp[256:512]
        kr = pp[512:544]
        cqn = _rms_cols(cq, cqg_ref[...], MLA_Q_RANK).astype(BF16)
        ckvn = _rms_cols(ckv, ckvg_ref[...], MLA_KV_RANK).astype(BF16)
        qa = _dot(w_uq_ref[...], cqn)
        kv = _dot(w_ukv_ref[...], ckvn)
        qg = qg_ref[...]
        kg = kg_ref[...]
        c32 = c32_ref[:, cols]
        s32 = s32_ref[:, cols]
        kr_ss = jnp.sum(kr * kr, axis=0, keepdims=True)
        for hd in range(MLA_HEADS):
            qh = _rms_cols(qa[hd * MLA_QK:(hd + 1) * MLA_QK], qg, MLA_QK)
            qr = _rope_cols(qh[MLA_NOPE:], c32, s32, MLA_ROPE // 4)
            qa_ref[hd, 0:MLA_NOPE, cols] = (qh[0:MLA_NOPE] * q_scale).astype(BF16)
            qa_ref[hd, MLA_NOPE:MLA_QK, cols] = (qr * q_scale).astype(BF16)
            qa_ref[hd, MLA_QK:HEAD_PAD, cols] = zpad.astype(BF16)
            kn = kv[hd * 128:hd * 128 + MLA_NOPE]
            ss = jnp.sum(kn * kn, axis=0, keepdims=True) + kr_ss
            rs = lax.rsqrt(ss * (1.0 / MLA_QK) + EPS)
            krn = _rope_cols(kr * rs * kg[MLA_NOPE:], c32, s32, MLA_ROPE // 4)
            k_cols = jnp.concatenate([kn * rs * kg[0:MLA_NOPE], krn, zpad], axis=0)
            ka_ref[hd, cols, :] = k_cols.T.astype(BF16)
            va_ref[hd, :, cols] = kv[hd * 128 + MLA_NOPE:(hd + 1) * 128].astype(BF16)

    def tail_gqa(pp, cols):
        gq = pp[0:512]
        gk = pp[512:640]
        gv = pp[640:768]
        c64 = c64_ref[:, cols]
        s64 = s64_ref[:, cols]
        gqg = gqg_ref[...]
        gkg = gkg_ref[...]
        for hd in range(GQA_HEADS):
            qh = _rms_cols(gq[hd * GQA_DIM:(hd + 1) * GQA_DIM], gqg, GQA_DIM)
            qh = (_rope_cols(qh, c64, s64, GQA_DIM // 4) * g_scale).astype(BF16)
            for kvh in range(GQA_KV_HEADS):
                qb_ref[hd, kvh * GQA_DIM:(kvh + 1) * GQA_DIM, cols] = qh if kvh == hd // grp else z64
        k_heads = []
        for kvh in range(GQA_KV_HEADS):
            kh = _rms_cols(gk[kvh * GQA_DIM:(kvh + 1) * GQA_DIM], gkg, GQA_DIM)
            k_heads.append(_rope_cols(kh, c64, s64, GQA_DIM // 4))
            vb_ref[kvh, :, cols] = gv[kvh * GQA_DIM:(kvh + 1) * GQA_DIM].astype(BF16)
        kb_ref[cols, :] = jnp.concatenate(k_heads, axis=0).T.astype(BF16)

    def tail_gate(r0):
        def tail(pp, cols):
            sg_ref[r0:r0 + GATE_ROWS, cols] = _silu(pp).astype(BF16)
        return tail

    g0 = 544 + 768
    groups = [(0, 544, tail_mla), (544, g0, tail_gqa)]
    groups += [(g0 + r, g0 + r + GATE_ROWS, tail_gate(r)) for r in range(0, D_MODEL, GATE_ROWS)]
    _projected_parts(x_ref, c_ref, mod_ref, w_in_ref, nx, groups)


def _proj_ab(x, ctx, mods, w_in_t, w_uq_t, w_ukv_t, gains, tabs, ts):
    b, s, d = x.shape
    lc = ctx.shape[1]
    nx = s // ts
    n = s + ts
    cqg, ckvg, qg, kg, gqg, gkg = [_bcast_gain(g) for g in gains]
    c32, s32, c64, s64 = tabs
    full = lambda a: pl.BlockSpec(a.shape, lambda i, t: (0,) * a.ndim)
    tab = lambda a: pl.BlockSpec((a.shape[0], ts), lambda i, t: (0, t))
    feat4 = lambda h, r: (jax.ShapeDtypeStruct((b, h, r, n), BF16),
                          pl.BlockSpec((None, h, r, ts), lambda i, t: (i, 0, 0, t)))
    rows4 = lambda h: (jax.ShapeDtypeStruct((b, h, n, HEAD_PAD), BF16),
                       pl.BlockSpec((None, h, ts, HEAD_PAD), lambda i, t: (i, 0, t, 0)))
    outs = [
        feat4(MLA_HEADS, HEAD_PAD),
        rows4(MLA_HEADS),
        feat4(MLA_HEADS, MLA_V),
        feat4(GQA_HEADS, HEAD_PAD),
        (jax.ShapeDtypeStruct((b, n, HEAD_PAD), BF16),
         pl.BlockSpec((None, ts, HEAD_PAD), lambda i, t: (i, t, 0))),
        feat4(GQA_KV_HEADS, GQA_DIM),
        (jax.ShapeDtypeStruct((b, D_MODEL, n), BF16),
         pl.BlockSpec((None, D_MODEL, ts), lambda i, t: (i, 0, t))),
    ]
    return pl.pallas_call(
        functools.partial(_proj_ab_kernel, nx=nx),
        grid=(b, nx + 1),
        in_specs=_joint_specs(nx, ts, lc, d) + [
            full(w_in_t), full(cqg), full(ckvg), full(w_uq_t), full(w_ukv_t),
            full(qg), full(kg), full(gqg), full(gkg),
            tab(c32), tab(s32), tab(c64), tab(s64),
        ],
        out_specs=[o[1] for o in outs],
        out_shape=[o[0] for o in outs],
        compiler_params=_cparams(("parallel", "parallel")),
        name="proj_ab",
    )(x, ctx, mods, w_in_t, cqg, ckvg, w_uq_t, w_ukv_t, qg, kg, gqg, gkg, c32, s32, c64, s64)


P_LIMIT = 2.0 ** 100
FIRST_REF_KEYS = 16


def _stream_kernel(q_ref, k_ref, v_ref, o_ref, m_sc, acc_sc, pv_sc, *, heads, kc, ahead):
    j = pl.program_id(2)
    tk = k_ref.shape[1]
    tq = q_ref.shape[2]
    dv = o_ref.shape[1]
    nc = tk // kc
    gk = k_ref.shape[0]
    gv = v_ref.shape[0]
    cur = lax.rem(j, 2)
    nxt = 1 - cur

    @pl.when(j == 0)
    def _():
        for g in range(heads):
            s0 = _dot(k_ref[g * gk // heads, 0:FIRST_REF_KEYS, :], q_ref[g])
            m_sc[0, g] = jnp.max(s0, axis=0, keepdims=True)
        acc_sc[0] = jnp.zeros(acc_sc.shape[1:], F32)

    units = [(g, c) for g in range(heads) for c in range(nc)]
    pending = {}

    def emit_scores(u):
        g, c = u
        pending[u] = _dot(k_ref[g * gk // heads, c * kc:(c + 1) * kc, :], q_ref[g])

    worst = None
    issued = 0
    pmx = None
    lsum = None
    for i, (g, c) in enumerate(units):
        while issued <= min(i + ahead, len(units) - 1):
            emit_scores(units[issued])
            issued += 1
        s = pending.pop((g, c))
        m_ref = m_sc[cur, g]
        mb = jnp.broadcast_to(m_ref, (8, tq))
        p32 = jnp.exp2(s.reshape(kc // 8, 8, tq) - mb[None])
        lpart = jnp.sum(p32, axis=0)
        lsum = lpart if c == 0 else lsum + lpart
        p = p32.reshape(kc, tq).astype(BF16)
        part = jnp.max(p.reshape(kc // 16, 16, tq), axis=0)
        pmx = part if c == 0 else jnp.maximum(pmx, part)
        d = _dot(v_ref[g * gv // heads, :, c * kc:(c + 1) * kc], p)
        if c == 0:
            pv_sc[g] = d
        else:
            pv_sc[g] += d
        if c == nc - 1:
            pm = jnp.maximum(jnp.max(pmx.astype(F32), axis=0, keepdims=True), 1.0)
            worst = pm if worst is None else jnp.maximum(worst, pm)
            m_sc[nxt, g] = m_ref + jnp.log2(pm)
            rp = 1.0 / pm
            acc_sc[nxt, g, 0:dv, :] = (acc_sc[cur, g, 0:dv, :] + pv_sc[g]) * rp
            acc_sc[nxt, g, dv:dv + 1, :] = (acc_sc[cur, g, dv:dv + 1, :] + jnp.sum(lsum, axis=0, keepdims=True)) * rp

    redo = jnp.logical_not(jnp.max(worst) <= P_LIMIT)

    @pl.when(redo)
    def _():
        def head(g, carry):
            q = q_ref[g]
            gki = g * gk // heads
            gvi = g * gv // heads
            m_old = m_sc[cur, g]

            def cmax(c, mx):
                s = _dot(k_ref[gki, pl.ds(pl.multiple_of(c * kc, kc), kc), :], q)
                return jnp.maximum(mx, jnp.max(s, axis=0, keepdims=True))

            m_new = lax.fori_loop(0, nc, cmax, m_old)

            def cacc(c, st):
                pv, l = st
                ks = pl.ds(pl.multiple_of(c * kc, kc), kc)
                p = jnp.exp2(_dot(k_ref[gki, ks, :], q) - m_new)
                return pv + _dot(v_ref[gvi, :, ks], p.astype(BF16)), l + jnp.sum(p, axis=0, keepdims=True)

            pv, l = lax.fori_loop(0, nc, cacc, (jnp.zeros((dv, tq), F32), jnp.zeros((1, tq), F32)))
            alpha = jnp.exp2(m_old - m_new)
            m_sc[nxt, g] = m_new
            acc_sc[nxt, g, 0:dv, :] = acc_sc[cur, g, 0:dv, :] * alpha + pv
            acc_sc[nxt, g, dv:dv + 1, :] = acc_sc[cur, g, dv:dv + 1, :] * alpha + l
            return carry

        lax.fori_loop(0, heads, head, 0)

    @pl.when(j == pl.num_programs(2) - 1)
    def _():
        for g in range(heads):
            o_ref[g] = (acc_sc[nxt, g, 0:dv, :] / acc_sc[nxt, g, dv:dv + 1, :]).astype(o_ref.dtype)


def _stream_attn(q_t, k, v_t, s, n, *, tq, tk, kc, ahead, name):
    b, hq = q_t.shape[0], q_t.shape[1]
    hk = k.shape[1]
    hv, dv = v_t.shape[1], v_t.shape[2]
    return pl.pallas_call(
        functools.partial(_stream_kernel, heads=hq, kc=kc, ahead=ahead),
        grid=(b, s // tq, n // tk),
        in_specs=[
            pl.BlockSpec((None, hq, HEAD_PAD, tq), lambda i, qi, j: (i, 0, 0, qi)),
            pl.BlockSpec((None, hk, tk, HEAD_PAD), lambda i, qi, j: (i, 0, j, 0)),
            pl.BlockSpec((None, hv, dv, tk), lambda i, qi, j: (i, 0, 0, j)),
        ],
        out_specs=pl.BlockSpec((None, hq, dv, tq), lambda i, qi, j: (i, 0, 0, qi)),
        out_shape=jax.ShapeDtypeStruct((b, hq, dv, s), BF16),
        scratch_shapes=[
            pltpu.VMEM((2, hq, 1, tq), F32),
            pltpu.VMEM((2, hq, dv + STAT_ROWS, tq), F32),
            pltpu.VMEM((hq, dv, tq), F32),
        ],
        compiler_params=_cparams(("parallel", "parallel", "arbitrary")),
        name=name,
    )(q_t, k, v_t)


def _ctx_attn_kernel(q_ref, k_ref, v_ref, o_ref):
    heads, gk, gv = q_ref.shape[0], k_ref.shape[0], v_ref.shape[0]
    for g in range(heads):
        s = _dot(k_ref[g * gk // heads], q_ref[g])
        p = jnp.exp2(s - jnp.max(s, axis=0, keepdims=True))
        den = jnp.sum(p, axis=0, keepdims=True)
        o_ref[g] = (_dot(v_ref[g * gv // heads], p.astype(BF16)) / den).astype(o_ref.dtype)


def _ctx_attn(q_t, k, v_t, s, lc, name):
    b, hq = q_t.shape[0], q_t.shape[1]
    hk = k.shape[1]
    hv, dv = v_t.shape[1], v_t.shape[2]
    off = s // lc
    return pl.pallas_call(
        _ctx_attn_kernel,
        grid=(b,),
        in_specs=[
            pl.BlockSpec((None, hq, HEAD_PAD, lc), lambda i: (i, 0, 0, off)),
            pl.BlockSpec((None, hk, lc, HEAD_PAD), lambda i: (i, 0, off, 0)),
            pl.BlockSpec((None, hv, dv, lc), lambda i: (i, 0, 0, off)),
        ],
        out_specs=pl.BlockSpec((None, hq, dv, lc), lambda i: (i, 0, 0, 0)),
        out_shape=jax.ShapeDtypeStruct((b, hq, dv, lc), BF16),
        compiler_params=_cparams(("parallel",)),
        name=name,
    )(q_t, k, v_t)


def _out_ab_kernel(x_ref, oa_ref, ob_ref, sg_ref, gate_ref, w_ref, o_ref):
    m = jnp.concatenate([oa_ref[...], ob_ref[...]], axis=0) * sg_ref[...]
    y_t = _dot(w_ref[...], m)
    o_ref[...] = x_ref[...] + gate_ref[...] * y_t.T


def _out_ab(x, oa_t, ob_t, sg_t, sg_off, gate, w_out_t, ts):
    b, n, d = x.shape
    half = oa_t.shape[1]
    return pl.pallas_call(
        _out_ab_kernel,
        grid=(b, n // ts),
        in_specs=[
            pl.BlockSpec((None, ts, d), lambda i, t: (i, t, 0)),
            pl.BlockSpec((None, half, ts), lambda i, t: (i, 0, t)),
            pl.BlockSpec((None, d - half, ts), lambda i, t: (i, 0, t)),
            pl.BlockSpec((None, d, ts), lambda i, t: (i, 0, t + sg_off)),
            pl.BlockSpec((None, 1, d), lambda i, t: (i, 0, 0)),
            pl.BlockSpec((d, d), lambda i, t: (0, 0)),
        ],
        out_specs=pl.BlockSpec((None, ts, d), lambda i, t: (i, t, 0)),
        out_shape=jax.ShapeDtypeStruct((b, n, d), F32),
        compiler_params=_cparams(("parallel", "parallel")),
        name="out_ab",
    )(x, oa_t, ob_t, sg_t, gate, w_out_t)


def _proj_cd_kernel(x_ref, c_ref, mod_ref, w_in_ref, qg_ref, kg_ref, c64_ref, s64_ref,
                    q_ref, k_ref, v_ref, gb_ref, u_ref, sg_ref, *, nx):
    scale = WIN_DIM ** -0.5 * LOG2E
    z64 = jnp.zeros((WIN_DIM, PROJ_PART), BF16)
    grp = WIN_HEADS // WIN_KV_HEADS

    def tail_q(pp, cols):
        c64 = c64_ref[:, cols]
        s64 = s64_ref[:, cols]
        qg = qg_ref[...]
        for hd in range(WIN_HEADS):
            qh = _rms_cols(pp[hd * WIN_DIM:(hd + 1) * WIN_DIM], qg, WIN_DIM)
            qh = (_rope_cols(qh, c64, s64, WIN_DIM // 4) * scale).astype(BF16)
            for kvh in range(WIN_KV_HEADS):
                q_ref[hd, kvh * WIN_DIM:(kvh + 1) * WIN_DIM, cols] = qh if kvh == hd // grp else z64

    def tail_kv(pp, cols):
        c64 = c64_ref[:, cols]
        s64 = s64_ref[:, cols]
        kg = kg_ref[...]
        k_heads = []
        for kvh in range(WIN_KV_HEADS):
            kh = _rms_cols(pp[kvh * WIN_DIM:(kvh + 1) * WIN_DIM], kg, WIN_DIM)
            k_heads.append(_rope_cols(kh, c64, s64, WIN_DIM // 4))
            v_ref[kvh, :, cols] = pp[128 + kvh * WIN_DIM:128 + (kvh + 1) * WIN_DIM].astype(BF16)
        k_ref[cols, :] = jnp.concatenate(k_heads, axis=0).T.astype(BF16)
        gb_ref[:, cols] = pp[256:768].astype(BF16)

    def tail_conv(pp, cols):
        u_ref[:, cols] = (pp[0:CONV_CH] * pp[CONV_CH:2 * CONV_CH]).astype(BF16)

    def tail_gate(r0):
        def tail(pp, cols):
            sg_ref[r0:r0 + GATE_ROWS, cols] = _silu(pp).astype(BF16)
        return tail

    g0 = 2304
    groups = [(0, 512, tail_q), (512, 1280, tail_kv), (1280, g0, tail_conv)]
    groups += [(g0 + r, g0 + r + GATE_ROWS, tail_gate(r)) for r in range(0, D_MODEL, GATE_ROWS)]
    _projected_parts(x_ref, c_ref, mod_ref, w_in_ref, nx, groups)


def _proj_cd(x, ctx, mods, w_in_t, gains, tabs, ts):
    b, s, d = x.shape
    lc = ctx.shape[1]
    nx = s // ts
    n = s + ts
    qg, kg = [_bcast_gain(g) for g in gains]
    c64, s64 = tabs
    full = lambda a: pl.BlockSpec(a.shape, lambda i, t: (0,) * a.ndim)
    tab = lambda a: pl.BlockSpec((a.shape[0], ts), lambda i, t: (0, t))
    feat3 = lambda r: (jax.ShapeDtypeStruct((b, r, n), BF16), pl.BlockSpec((None, r, ts), lambda i, t: (i, 0, t)))
    outs = [
        (jax.ShapeDtypeStruct((b, WIN_HEADS, HEAD_PAD, n), BF16),
         pl.BlockSpec((None, WIN_HEADS, HEAD_PAD, ts), lambda i, t: (i, 0, 0, t))),
        (jax.ShapeDtypeStruct((b, n, HEAD_PAD), BF16),
         pl.BlockSpec((None, ts, HEAD_PAD), lambda i, t: (i, t, 0))),
        (jax.ShapeDtypeStruct((b, WIN_KV_HEADS, WIN_DIM, n), BF16),
         pl.BlockSpec((None, WIN_KV_HEADS, WIN_DIM, ts), lambda i, t: (i, 0, 0, t))),
        feat3(CONV_CH),
        feat3(CONV_CH),
        feat3(D_MODEL),
    ]
    return pl.pallas_call(
        functools.partial(_proj_cd_kernel, nx=nx),
        grid=(b, nx + 1),
        in_specs=_joint_specs(nx, ts, lc, d) + [full(w_in_t), full(qg), full(kg), tab(c64), tab(s64)],
        out_specs=[o[1] for o in outs],
        out_shape=[o[0] for o in outs],
        compiler_params=_cparams(("parallel", "parallel")),
        name="proj_cd",
    )(x, ctx, mods, w_in_t, qg, kg, c64, s64)


def _win_kernel(q_ref, kp_ref, kc_ref, kn_ref, vp_ref, vcur_ref, vn_ref, kctx_ref, vctx_ref, bias_ref, sink_ref,
                o_ref, *, sub):
    heads, tq = q_ref.shape[0], q_ref.shape[2]
    gv = vcur_ref.shape[0]
    nsub = tq // sub
    qi = pl.program_id(1)
    edge_p = jnp.where(qi > 0, 0.0, NEG_INF)
    edge_n = jnp.where(qi < pl.num_programs(1) - 1, 0.0, NEG_INF)
    bias_mid = bias_ref[...]
    bias_first = jnp.concatenate([bias_ref[0:WINDOW, :] + edge_p, bias_ref[WINDOW:, :]], axis=0)
    bias_last = jnp.concatenate([bias_ref[0:sub + WINDOW, :], bias_ref[sub + WINDOW:, :] + edge_n], axis=0)
    if nsub == 1:
        bias_first = bias_last = jnp.concatenate(
            [bias_ref[0:WINDOW, :] + edge_p, bias_ref[WINDOW:sub + WINDOW, :], bias_ref[sub + WINDOW:, :] + edge_n],
            axis=0)
    k_loc = jnp.concatenate([kp_ref[...], kc_ref[...], kn_ref[...]], axis=0)
    v_loc = [jnp.concatenate([vp_ref[h], vcur_ref[h], vn_ref[h]], axis=1) for h in range(gv)]
    kctx = kctx_ref[...]
    units = [(g, t) for g in range(heads) for t in range(nsub)]
    pending = {}

    def emit_scores(u):
        g, t = u
        q = q_ref[g, :, t * sub:(t + 1) * sub]
        bias = bias_first if t == 0 else (bias_last if t == nsub - 1 else bias_mid)
        pending[u] = (_dot(k_loc[t * sub:(t + 1) * sub + 2 * WINDOW], q) + bias, _dot(kctx, q))

    issued = 0
    for i, (g, t) in enumerate(units):
        while issued <= min(i + WIN_AHEAD, len(units) - 1):
            emit_scores(units[issued])
            issued += 1
        s_l, s_x = pending.pop((g, t))
        sk = sink_ref[g] * LOG2E
        m = jnp.maximum(jnp.maximum(jnp.max(s_l, axis=0, keepdims=True), jnp.max(s_x, axis=0, keepdims=True)), sk)
        p_l = jnp.exp2(s_l - m)
        p_x = jnp.exp2(s_x - m)
        den = jnp.sum(p_l, axis=0, keepdims=True) + jnp.sum(p_x, axis=0, keepdims=True) + jnp.exp2(sk - m)
        h = g * gv // heads
        acc = (_dot(v_loc[h][:, t * sub:(t + 1) * sub + 2 * WINDOW], p_l.astype(BF16))
               + _dot(vctx_ref[h], p_x.astype(BF16)))
        o_ref[g, :, t * sub:(t + 1) * sub] = (acc / den).astype(o_ref.dtype)


def _win_attn(q_t, k, v_t, sink, s, lc, *, tq, sub):
    b, hq = q_t.shape[0], q_t.shape[1]
    hv, dv = v_t.shape[1], v_t.shape[2]
    r = tq // WINDOW
    nb = s // WINDOW
    prev = lambda qi: jnp.maximum(qi * r - 1, 0)
    nxt = lambda qi: jnp.minimum((qi + 1) * r, nb - 1)
    kpos = jnp.arange(sub + 2 * WINDOW, dtype=jnp.int32)[:, None] - WINDOW
    qpos = jnp.arange(sub, dtype=jnp.int32)[None, :]
    bias = jnp.where(jnp.abs(qpos - kpos) <= WINDOW, 0.0, NEG_INF).astype(F32)
    return pl.pallas_call(
        functools.partial(_win_kernel, sub=sub),
        grid=(b, s // tq),
        in_specs=[
            pl.BlockSpec((None, hq, HEAD_PAD, tq), lambda i, qi: (i, 0, 0, qi)),
            pl.BlockSpec((None, WINDOW, HEAD_PAD), lambda i, qi: (i, prev(qi), 0)),
            pl.BlockSpec((None, tq, HEAD_PAD), lambda i, qi: (i, qi, 0)),
            pl.BlockSpec((None, WINDOW, HEAD_PAD), lambda i, qi: (i, nxt(qi), 0)),
            pl.BlockSpec((None, hv, dv, WINDOW), lambda i, qi: (i, 0, 0, prev(qi))),
            pl.BlockSpec((None, hv, dv, tq), lambda i, qi: (i, 0, 0, qi)),
            pl.BlockSpec((None, hv, dv, WINDOW), lambda i, qi: (i, 0, 0, nxt(qi))),
            pl.BlockSpec((None, lc, HEAD_PAD), lambda i, qi: (i, s // lc, 0)),
            pl.BlockSpec((None, hv, dv, lc), lambda i, qi: (i, 0, 0, s // lc)),
            pl.BlockSpec(bias.shape, lambda i, qi: (0, 0)),
            pl.BlockSpec(memory_space=pltpu.SMEM),
        ],
        out_specs=pl.BlockSpec((None, hq, dv, tq), lambda i, qi: (i, 0, 0, qi)),
        out_shape=jax.ShapeDtypeStruct((b, hq, dv, s), BF16),
        compiler_params=_cparams(("parallel", "parallel")),
        name="win_attn",
    )(q_t, k, k, k, v_t, v_t, v_t, k, v_t, bias, sink)


def _out_cd_kernel(x_ref, oc_ref, gb_ref, up_ref, u_ref, un_ref, cw_ref, sg_ref, gate_ref, w_ref, o_ref, *, ts):
    t = pl.program_id(1)
    nt = pl.num_programs(1)
    u = u_ref[...].astype(F32)
    lane = lax.broadcasted_iota(jnp.int32, (CONV_CH, ts), 1)
    halo_l = pltpu.roll(up_ref[...].astype(F32), 1, axis=1)
    halo_l = jnp.where(t > 0, halo_l, 0.0)
    halo_r = pltpu.roll(un_ref[...].astype(F32), LANES - 1, axis=1)
    halo_r = jnp.where(t < nt - 1, halo_r, 0.0)
    reps = ts // LANES
    um = jnp.where(lane == 0, jnp.tile(halo_l, (1, reps)), pltpu.roll(u, 1, axis=1))
    up = jnp.where(lane == ts - 1, jnp.tile(halo_r, (1, reps)), pltpu.roll(u, ts - 1, axis=1))
    w0 = jnp.tile(cw_ref[0], (1, reps))
    w1 = jnp.tile(cw_ref[1], (1, reps))
    w2 = jnp.tile(cw_ref[2], (1, reps))
    od = gb_ref[...].astype(F32) * (w0 * um + w1 * u + w2 * up)
    sg = sg_ref[...]
    half = D_MODEL - CONV_CH
    m = jnp.concatenate([oc_ref[...] * sg[0:half], od.astype(BF16) * sg[half:]], axis=0)
    y_t = _dot(w_ref[...], m)
    o_ref[...] = x_ref[...] + gate_ref[...] * y_t.T


def _out_cd(x, oc_t, gb_t, u_t, conv_wb, sg_t, gate, w_out_t, ts):
    b, n, d = x.shape
    r = ts // LANES
    nb = n // LANES
    return pl.pallas_call(
        functools.partial(_out_cd_kernel, ts=ts),
        grid=(b, n // ts),
        in_specs=[
            pl.BlockSpec((None, ts, d), lambda i, t: (i, t, 0)),
            pl.BlockSpec((None, d - CONV_CH, ts), lambda i, t: (i, 0, t)),
            pl.BlockSpec((None, CONV_CH, ts), lambda i, t: (i, 0, t)),
            pl.BlockSpec((None, CONV_CH, LANES), lambda i, t: (i, 0, jnp.maximum(t * r - 1, 0))),
            pl.BlockSpec((None, CONV_CH, ts), lambda i, t: (i, 0, t)),
            pl.BlockSpec((None, CONV_CH, LANES), lambda i, t: (i, 0, jnp.minimum((t + 1) * r, nb - 1))),
            pl.BlockSpec((CONV_K, CONV_CH, LANES), lambda i, t: (0, 0, 0)),
            pl.BlockSpec((None, d, ts), lambda i, t: (i, 0, t)),
            pl.BlockSpec((None, 1, d), lambda i, t: (i, 0, 0)),
            pl.BlockSpec((d, d), lambda i, t: (0, 0)),
        ],
        out_specs=pl.BlockSpec((None, ts, d), lambda i, t: (i, t, 0)),
        out_shape=jax.ShapeDtypeStruct((b, n, d), F32),
        compiler_params=_cparams(("parallel", "parallel")),
        name="out_cd",
    )(x, oc_t, gb_t, u_t, u_t, u_t, conv_wb, sg_t, gate, w_out_t)


def _rope_tables(s, tail, dd):
    q = dd // 4
    rows = s // GRID_W
    row = jnp.repeat(jnp.arange(rows, dtype=F32), GRID_W)
    col = jnp.tile(jnp.arange(GRID_W, dtype=F32), rows)
    inv = ROPE_BASE ** (-jnp.arange(q, dtype=F32) / q)
    ar = inv[:, None] * row[None, :]
    ac = inv[:, None] * col[None, :]
    cos = jnp.concatenate([jnp.cos(ar), jnp.cos(ar), jnp.cos(ac), jnp.cos(ac)], axis=0)
    sin = jnp.concatenate([-jnp.sin(ar), jnp.sin(ar), -jnp.sin(ac), jnp.sin(ac)], axis=0)
    return (jnp.concatenate([cos, jnp.ones((dd, tail), F32)], axis=1),
            jnp.concatenate([sin, jnp.zeros((dd, tail), F32)], axis=1))


def _tile_rows(n, cap):
    t = cap
    while n % t:
        t //= 2
    return t


def _key_tile(n, cap, step):
    t = (cap // step) * step
    while n % t:
        t -= step
    return t


def kernel(x, c, ctx, c_ctx, mod_w, mod_b, ab_w_in, ab_w_out, mla_cq_gain, mla_ckv_gain, mla_w_uq, mla_w_ukv,
           mla_q_gain, mla_k_gain, gqa_q_gain, gqa_k_gain, cd_w_in, cd_w_out, win_q_gain, win_k_gain,
           win_sink, conv_w):
    b, s, d = x.shape
    lc = ctx.shape[1]
    ts_p = _tile_rows(s, 4 * PROJ_PART)
    assert ts_p % lc == 0 and ts_p % PROJ_PART == 0 and s % lc == 0
    ts_x = _tile_rows(s, 1024)
    ts_c = _tile_rows(lc, 256)
    tq = _tile_rows(s, 512)
    n = s + lc
    tk = _key_tile(n, 3328, LANES)
    kc = 256 if tk % 256 == 0 else tk
    ahead = 2
    tq_w = _tile_rows(s, 1024)

    crows = jnp.zeros((8, d), F32).at[0:b].set(c).at[b].set(c_ctx)
    mods = _modulation(crows, mod_w, mod_b)

    def split_mod(layer):
        m = mods[layer]
        shift, scale, gate = m[:, 0:d], m[:, d:2 * d], m[:, 2 * d:3 * d]
        lat = jnp.stack([shift[0:b], scale[0:b]], axis=1)
        con = jnp.broadcast_to(jnp.stack([shift[b], scale[b]], axis=0)[None], (b, 2, d))
        gate_x = gate[0:b][:, None, :]
        gate_c = jnp.broadcast_to(gate[b][None, None, :], (b, 1, d))
        return jnp.stack([lat, con], axis=1), gate_x, gate_c

    tabs32 = _rope_tables(s, ts_p, MLA_ROPE)
    tabs64 = _rope_tables(s, ts_p, GQA_DIM)

    mod4, gate_x, gate_c = split_mod(0)
    w_in_t = ab_w_in[0].T.astype(BF16)
    w_uq_t = mla_w_uq[0].T.astype(BF16)
    w_ukv_t = mla_w_ukv[0].T.astype(BF16)
    w_out_t = ab_w_out[0].T.astype(BF16)
    gains = (mla_cq_gain[0], mla_ckv_gain[0], mla_q_gain[0], mla_k_gain[0], gqa_q_gain[0], gqa_k_gain[0])
    qa, ka, va, qb, kb, vb, sg = _proj_ab(x, ctx, mod4, w_in_t, w_uq_t, w_ukv_t, gains, tabs32 + tabs64, ts_p)
    kb = kb[:, None]
    oa = _stream_attn(qa, ka, va, s, n, tq=tq, tk=tk, kc=kc, ahead=ahead, name="flash_mla")
    ob = _stream_attn(qb, kb, vb, s, n, tq=tq, tk=tk, kc=kc, ahead=ahead, name="flash_gqa")
    x1 = _out_ab(x, oa.reshape(b, -1, s), ob.reshape(b, -1, s), sg, 0, gate_x, w_out_t, ts_x)
    oac = _ctx_attn(qa, ka, va, s, lc, "ctx_mla")
    obc = _ctx_attn(qb, kb, vb, s, lc, "ctx_gqa")
    xc1 = _out_ab(ctx, oac.reshape(b, -1, lc), obc.reshape(b, -1, lc), sg, s // ts_c, gate_c, w_out_t, ts_c)

    mod4, gate_x, gate_c = split_mod(1)
    w_in_t = cd_w_in[0].T.astype(BF16)
    w_out_t = cd_w_out[0].T.astype(BF16)
    q, k, v, gb, u, sg = _proj_cd(x1, xc1, mod4, w_in_t, (win_q_gain[0], win_k_gain[0]), tabs64, ts_p)
    oc = _win_attn(q, k, v, win_sink[0].astype(F32), s, lc, tq=tq_w, sub=min(tq_w, 256))
    conv_wb = jnp.broadcast_to(conv_w[0].astype(F32)[:, :, None], (CONV_K, CONV_CH, LANES))
    return _out_cd(x1, oc.reshape(b, -1, s), gb, u, conv_wb, sg, gate_x, w_out_t, ts_x)
```

```python
import functools
import math

import jax
import jax.numpy as jnp
from jax import lax
from jax.experimental import pallas as pl
from jax.experimental.pallas import tpu as pltpu

D_MODEL = 1024
GRID_W = 64
ROPE_BASE = 10000.0
EPS = 1e-6
NEG_INF = -1e30
WINDOW = 128

MLA_HEADS = 8
MLA_NOPE = 64
MLA_ROPE = 32
MLA_V = 64
MLA_QK = MLA_NOPE + MLA_ROPE
MLA_Q_RANK = 256
MLA_KV_RANK = 256
GQA_HEADS = 8
GQA_KV_HEADS = 2
GQA_DIM = 64
WIN_HEADS = 8
WIN_KV_HEADS = 2
WIN_DIM = 64
CONV_CH = 512
CONV_K = 3

LANES = 128
HEAD_PAD = 128
STAT_ROWS = 8
PROJ_PART = 256
GATE_ROWS = 256
WIN_AHEAD = 2
LOG2E = math.log2(math.e)
VMEM_LIMIT_BYTES = 56 * 1024 * 1024

BF16 = jnp.bfloat16
F32 = jnp.float32


def _cparams(sem):
    return pltpu.CompilerParams(dimension_semantics=sem, vmem_limit_bytes=VMEM_LIMIT_BYTES)


def _dot(a, b):
    return jnp.dot(a, b, preferred_element_type=F32)


def _dot_nt(a, b):
    return lax.dot_general(a, b, (((1,), (1,)), ((), ())), preferred_element_type=F32)


def _silu(v):
    h = 0.5 * v
    return h + h * jnp.tanh(h)


def _mod_kernel(c_ref, w_ref, b_ref, o_ref):
    sc = _silu(c_ref[...])
    o_ref[...] = _dot(sc, w_ref[...]) + b_ref[...]


def _modulation(crows, mod_w, mod_b):
    depth, d, n = mod_w.shape
    tn = 1024
    return pl.pallas_call(
        _mod_kernel,
        grid=(depth, n // tn),
        in_specs=[
            pl.BlockSpec((8, d), lambda l, j: (0, 0)),
            pl.BlockSpec((None, d, tn), lambda l, j: (l, 0, j)),
            pl.BlockSpec((None, 1, tn), lambda l, j: (l, 0, j)),
        ],
        out_specs=pl.BlockSpec((None, 8, tn), lambda l, j: (l, 0, j)),
        out_shape=jax.ShapeDtypeStruct((depth, 8, n), F32),
        compiler_params=_cparams(("parallel", "parallel")),
        name="modulation",
    )(crows, mod_w, mod_b.reshape(depth, 1, n))


def _adaln_rows(x, shift, scale):
    ms = jnp.mean(x * x, axis=-1, keepdims=True)
    return (x * lax.rsqrt(ms + EPS)) * (1.0 + scale) + shift


def _rms_cols(v, gain_b, n):
    ss = jnp.sum(v * v, axis=0, keepdims=True)
    return v * lax.rsqrt(ss * (1.0 / n) + EPS) * gain_b


def _rope_cols(v, cos, sin, q):
    swap = jnp.concatenate([v[q:2 * q], v[0:q], v[3 * q:4 * q], v[2 * q:3 * q]], axis=0)
    return v * cos + swap * sin


def _projected_parts(x_ref, c_ref, mod_ref, w_in_ref, nx, groups):
    ts = x_ref.shape[0]
    ctx_rows = jnp.concatenate([c_ref[...]] * (ts // c_ref.shape[0]), axis=0)
    rows = jnp.where(pl.program_id(1) < nx, x_ref[...], ctx_rows)
    shift, scale = mod_ref[0:1, :], mod_ref[1:2, :]
    parts = ts // PROJ_PART
    cols = lambda i: slice(i * PROJ_PART, (i + 1) * PROJ_PART)

    def normed(i):
        return _adaln_rows(rows[cols(i)], shift, scale).astype(BF16)

    def project(h, g):
        r0, r1, _ = groups[g]
        return _dot_nt(w_in_ref[r0:r1, :], h)

    h = normed(0)
    pend = [project(h, g) for g in range(len(groups))]
    for i in range(parts):
        nxt = []
        if i + 1 < parts:
            h = normed(i + 1)
        for g in range(len(groups)):
            if i + 1 < parts:
                nxt.append(project(h, g))
            groups[g][2](pend[g], cols(i))
        pend = nxt


def _joint_specs(nx, ts, lc, d):
    return [
        pl.BlockSpec((None, ts, d), lambda i, t: (i, jnp.minimum(t, nx - 1), 0)),
        pl.BlockSpec((None, lc, d), lambda i, t: (i, 0, 0)),
        pl.BlockSpec((None, None, 2, d), lambda i, t: (i, jnp.where(t < nx, 0, 1), 0, 0)),
    ]


def _bcast_gain(g):
    return jnp.broadcast_to(g.astype(F32)[:, None], (g.shape[0], PROJ_PART))


def _proj_ab_kernel(x_ref, c_ref, mod_ref, w_in_ref, cqg_ref, ckvg_ref, w_uq_ref, w_ukv_ref,
                    qg_ref, kg_ref, gqg_ref, gkg_ref, c32_ref, s32_ref, c64_ref, s64_ref,
                    qa_ref, ka_ref, va_ref, qb_ref, kb_ref, vb_ref, sg_ref, *, nx):
    zpad = jnp.zeros((HEAD_PAD - MLA_QK, PROJ_PART), F32)
    z64 = jnp.zeros((GQA_DIM, PROJ_PART), BF16)
    q_scale = MLA_QK ** -0.5 * LOG2E
    g_scale = GQA_DIM ** -0.5 * LOG2E
    grp = GQA_HEADS // GQA_KV_HEADS

    def tail_mla(pp, cols):
        cq = pp[0:256]
        ckv = pp[256:512]
        kr = pp[512:544]
        cqn = _rms_cols(cq, cqg_ref[...], MLA_Q_RANK).astype(BF16)
        ckvn = _rms_cols(ckv, ckvg_ref[...], MLA_KV_RANK).astype(BF16)
        qa = _dot(w_uq_ref[...], cqn)
        kv = _dot(w_ukv_ref[...], ckvn)
        qg = qg_ref[...]
        kg = kg_ref[...]
        c32 = c32_ref[:, cols]
        s32 = s32_ref[:, cols]
        kr_ss = jnp.sum(kr * kr, axis=0, keepdims=True)
        for hd in range(MLA_HEADS):
            qh = _rms_cols(qa[hd * MLA_QK:(hd + 1) * MLA_QK], qg, MLA_QK)
            qr = _rope_cols(qh[MLA_NOPE:], c32, s32, MLA_ROPE // 4)
            qa_ref[hd, 0:MLA_NOPE, cols] = (qh[0:MLA_NOPE] * q_scale).astype(BF16)
            qa_ref[hd, MLA_NOPE:MLA_QK, cols] = (qr * q_scale).astype(BF16)
            qa_ref[hd, MLA_QK:HEAD_PAD, cols] = zpad.astype(BF16)
            kn = kv[hd * 128:hd * 128 + MLA_NOPE]
            ss = jnp.sum(kn * kn, axis=0, keepdims=True) + kr_ss
            rs = lax.rsqrt(ss * (1.0 / MLA_QK) + EPS)
            krn = _rope_cols(kr * rs * kg[MLA_NOPE:], c32, s32, MLA_ROPE // 4)
            k_cols = jnp.concatenate([kn * rs * kg[0:MLA_NOPE], krn, zpad], axis=0)
            ka_ref[hd, cols, :] = k_cols.T.astype(BF16)
            va_ref[hd, :, cols] = kv[hd * 128 + MLA_NOPE:(hd + 1) * 128].astype(BF16)

    def tail_gqa(pp, cols):
        gq = pp[0:512]
        gk = pp[512:640]
        gv = pp[640:768]
        c64 = c64_ref[:, cols]
        s64 = s64_ref[:, cols]
        gqg = gqg_ref[...]
        gkg = gkg_ref[...]
        for hd in range(GQA_HEADS):
            qh = _rms_cols(gq[hd * GQA_DIM:(hd + 1) * GQA_DIM], gqg, GQA_DIM)
            qh = (_rope_cols(qh, c64, s64, GQA_DIM // 4) * g_scale).astype(BF16)
            for kvh in range(GQA_KV_HEADS):
                qb_ref[hd, kvh * GQA_DIM:(kvh + 1) * GQA_DIM, cols] = qh if kvh == hd // grp else z64
        k_heads = []
        for kvh in range(GQA_KV_HEADS):
            kh = _rms_cols(gk[kvh * GQA_DIM:(kvh + 1) * GQA_DIM], gkg, GQA_DIM)
            k_heads.append(_rope_cols(kh, c64, s64, GQA_DIM // 4))
            vb_ref[kvh, :, cols] = gv[kvh * GQA_DIM:(kvh + 1) * GQA_DIM].astype(BF16)
        kb_ref[cols, :] = jnp.concatenate(k_heads, axis=0).T.astype(BF16)

    def tail_gate(r0):
        def tail(pp, cols):
            sg_ref[r0:r0 + GATE_ROWS, cols] = _silu(pp).astype(BF16)
        return tail

    g0 = 544 + 768
    groups = [(0, 544, tail_mla), (544, g0, tail_gqa)]
    groups += [(g0 + r, g0 + r + GATE_ROWS, tail_gate(r)) for r in range(0, D_MODEL, GATE_ROWS)]
    _projected_parts(x_ref, c_ref, mod_ref, w_in_ref, nx, groups)


def _proj_ab(x, ctx, mods, w_in_t, w_uq_t, w_ukv_t, gains, tabs, ts):
    b, s, d = x.shape
    lc = ctx.shape[1]
    nx = s // ts
    n = s + ts
    cqg, ckvg, qg, kg, gqg, gkg = [_bcast_gain(g) for g in gains]
    c32, s32, c64, s64 = tabs
    full = lambda a: pl.BlockSpec(a.shape, lambda i, t: (0,) * a.ndim)
    tab = lambda a: pl.BlockSpec((a.shape[0], ts), lambda i, t: (0, t))
    feat4 = lambda h, r: (jax.ShapeDtypeStruct((b, h, r, n), BF16),
                          pl.BlockSpec((None, h, r, ts), lambda i, t: (i, 0, 0, t)))
    rows4 = lambda h: (jax.ShapeDtypeStruct((b, h, n, HEAD_PAD), BF16),
                       pl.BlockSpec((None, h, ts, HEAD_PAD), lambda i, t: (i, 0, t, 0)))
    outs = [
        feat4(MLA_HEADS, HEAD_PAD),
        rows4(MLA_HEADS),
        feat4(MLA_HEADS, MLA_V),
        feat4(GQA_HEADS, HEAD_PAD),
        (jax.ShapeDtypeStruct((b, n, HEAD_PAD), BF16),
         pl.BlockSpec((None, ts, HEAD_PAD), lambda i, t: (i, t, 0))),
        feat4(GQA_KV_HEADS, GQA_DIM),
        (jax.ShapeDtypeStruct((b, D_MODEL, n), BF16),
         pl.BlockSpec((None, D_MODEL, ts), lambda i, t: (i, 0, t))),
    ]
    return pl.pallas_call(
        functools.partial(_proj_ab_kernel, nx=nx),
        grid=(b, nx + 1),
        in_specs=_joint_specs(nx, ts, lc, d) + [
            full(w_in_t), full(cqg), full(ckvg), full(w_uq_t), full(w_ukv_t),
            full(qg), full(kg), full(gqg), full(gkg),
            tab(c32), tab(s32), tab(c64), tab(s64),
        ],
        out_specs=[o[1] for o in outs],
        out_shape=[o[0] for o in outs],
        compiler_params=_cparams(("parallel", "parallel")),
        name="proj_ab",
    )(x, ctx, mods, w_in_t, cqg, ckvg, w_uq_t, w_ukv_t, qg, kg, gqg, gkg, c32, s32, c64, s64)


P_LIMIT = 2.0 ** 100
HEAD_GROUPS = 2
FIRST_REF_KEYS = 16


def _stream_kernel(q_ref, k_ref, v_ref, o_ref, m_sc, acc_sc, pv_sc, *, heads, kc, ahead):
    j = pl.program_id(3)
    tk = k_ref.shape[1]
    tq = q_ref.shape[2]
    dv = o_ref.shape[1]
    nc = tk // kc
    gk = k_ref.shape[0]
    gv = v_ref.shape[0]
    cur = lax.rem(j, 2)
    nxt = 1 - cur

    @pl.when(j == 0)
    def _():
        for g in range(heads):
            s0 = _dot(k_ref[g * gk // heads, 0:FIRST_REF_KEYS, :], q_ref[g])
            m_sc[0, g] = jnp.max(s0, axis=0, keepdims=True)
        acc_sc[0] = jnp.zeros(acc_sc.shape[1:], F32)

    units = [(g, c) for g in range(heads) for c in range(nc)]
    pending = {}

    def emit_scores(u):
        g, c = u
        pending[u] = _dot(k_ref[g * gk // heads, c * kc:(c + 1) * kc, :], q_ref[g])

    worst = None
    issued = 0
    pmx = None
    lsum = None
    for i, (g, c) in enumerate(units):
        while issued <= min(i + ahead, len(units) - 1):
            emit_scores(units[issued])
            issued += 1
        s = pending.pop((g, c))
        m_ref = m_sc[cur, g]
        mb = jnp.broadcast_to(m_ref, (8, tq))
        p32 = jnp.exp2(s.reshape(kc // 8, 8, tq) - mb[None])
        lpart = jnp.sum(p32, axis=0)
        lsum = lpart if c == 0 else lsum + lpart
        p = p32.reshape(kc, tq).astype(BF16)
        part = jnp.max(p.reshape(kc // 16, 16, tq), axis=0)
        pmx = part if c == 0 else jnp.maximum(pmx, part)
        d = _dot(v_ref[g * gv // heads, :, c * kc:(c + 1) * kc], p)
        if c == 0:
            pv_sc[g] = d
        else:
            pv_sc[g] += d
        if c == nc - 1:
            pm = jnp.maximum(jnp.max(pmx.astype(F32), axis=0, keepdims=True), 1.0)
            worst = pm if worst is None else jnp.maximum(worst, pm)
            m_sc[nxt, g] = m_ref + jnp.log2(pm)
            rp = 1.0 / pm
            acc_sc[nxt, g, 0:dv, :] = (acc_sc[cur, g, 0:dv, :] + pv_sc[g]) * rp
            acc_sc[nxt, g, dv:dv + 1, :] = (acc_sc[cur, g, dv:dv + 1, :] + jnp.sum(lsum, axis=0, keepdims=True)) * rp

    redo = jnp.logical_not(jnp.max(worst) <= P_LIMIT)

    @pl.when(redo)
    def _():
        def head(g, carry):
            q = q_ref[g]
            gki = g * gk // heads
            gvi = g * gv // heads
            m_old = m_sc[cur, g]

            def cmax(c, mx):
                s = _dot(k_ref[gki, pl.ds(pl.multiple_of(c * kc, kc), kc), :], q)
                return jnp.maximum(mx, jnp.max(s, axis=0, keepdims=True))

            m_new = lax.fori_loop(0, nc, cmax, m_old)

            def cacc(c, st):
                pv, l = st
                ks = pl.ds(pl.multiple_of(c * kc, kc), kc)
                p = jnp.exp2(_dot(k_ref[gki, ks, :], q) - m_new)
                return pv + _dot(v_ref[gvi, :, ks], p.astype(BF16)), l + jnp.sum(p, axis=0, keepdims=True)

            pv, l = lax.fori_loop(0, nc, cacc, (jnp.zeros((dv, tq), F32), jnp.zeros((1, tq), F32)))
            alpha = jnp.exp2(m_old - m_new)
            m_sc[nxt, g] = m_new
            acc_sc[nxt, g, 0:dv, :] = acc_sc[cur, g, 0:dv, :] * alpha + pv
            acc_sc[nxt, g, dv:dv + 1, :] = acc_sc[cur, g, dv:dv + 1, :] * alpha + l
            return carry

        lax.fori_loop(0, heads, head, 0)

    @pl.when(j == pl.num_programs(3) - 1)
    def _():
        for g in range(heads):
            o_ref[g] = (acc_sc[nxt, g, 0:dv, :] / acc_sc[nxt, g, dv:dv + 1, :]).astype(o_ref.dtype)


def _stream_attn(q_t, k, v_t, s, n, *, tq, tk, kc, ahead, name):
    b, hq = q_t.shape[0], q_t.shape[1]
    hk = k.shape[1]
    hv, dv = v_t.shape[1], v_t.shape[2]
    hg = hq // HEAD_GROUPS
    kg = max(hk // HEAD_GROUPS, 1)
    vg = hv // HEAD_GROUPS
    kgrp = (lambda g: g) if hk >= HEAD_GROUPS else (lambda g: 0)
    return pl.pallas_call(
        functools.partial(_stream_kernel, heads=hg, kc=kc, ahead=ahead),
        grid=(b, HEAD_GROUPS, s // tq, n // tk),
        in_specs=[
            pl.BlockSpec((None, hg, HEAD_PAD, tq), lambda i, g, qi, j: (i, g, 0, qi)),
            pl.BlockSpec((None, kg, tk, HEAD_PAD), lambda i, g, qi, j: (i, kgrp(g), j, 0)),
            pl.BlockSpec((None, vg, dv, tk), lambda i, g, qi, j: (i, g, 0, j)),
        ],
        out_specs=pl.BlockSpec((None, hg, dv, tq), lambda i, g, qi, j: (i, g, 0, qi)),
        out_shape=jax.ShapeDtypeStruct((b, hq, dv, s), BF16),
        scratch_shapes=[
            pltpu.VMEM((2, hg, 1, tq), F32),
            pltpu.VMEM((2, hg, dv + STAT_ROWS, tq), F32),
            pltpu.VMEM((hg, dv, tq), F32),
        ],
        compiler_params=_cparams(("parallel", "parallel", "parallel", "arbitrary")),
        name=name,
    )(q_t, k, v_t)


def _ctx_attn_kernel(q_ref, k_ref, v_ref, o_ref):
    heads, gk, gv = q_ref.shape[0], k_ref.shape[0], v_ref.shape[0]
    for g in range(heads):
        s = _dot(k_ref[g * gk // heads], q_ref[g])
        p = jnp.exp2(s - jnp.max(s, axis=0, keepdims=True))
        den = jnp.sum(p, axis=0, keepdims=True)
        o_ref[g] = (_dot(v_ref[g * gv // heads], p.astype(BF16)) / den).astype(o_ref.dtype)


def _ctx_attn(q_t, k, v_t, s, lc, name):
    b, hq = q_t.shape[0], q_t.shape[1]
    hk = k.shape[1]
    hv, dv = v_t.shape[1], v_t.shape[2]
    off = s // lc
    return pl.pallas_call(
        _ctx_attn_kernel,
        grid=(b,),
        in_specs=[
            pl.BlockSpec((None, hq, HEAD_PAD, lc), lambda i: (i, 0, 0, off)),
            pl.BlockSpec((None, hk, lc, HEAD_PAD), lambda i: (i, 0, off, 0)),
            pl.BlockSpec((None, hv, dv, lc), lambda i: (i, 0, 0, off)),
        ],
        out_specs=pl.BlockSpec((None, hq, dv, lc), lambda i: (i, 0, 0, 0)),
        out_shape=jax.ShapeDtypeStruct((b, hq, dv, lc), BF16),
        compiler_params=_cparams(("parallel",)),
        name=name,
    )(q_t, k, v_t)


def _out_ab_kernel(x_ref, oa_ref, ob_ref, sg_ref, gate_ref, w_ref, o_ref):
    m = jnp.concatenate([oa_ref[...], ob_ref[...]], axis=0) * sg_ref[...]
    y_t = _dot(w_ref[...], m)
    o_ref[...] = x_ref[...] + gate_ref[...] * y_t.T


def _out_ab(x, oa_t, ob_t, sg_t, sg_off, gate, w_out_t, ts):
    b, n, d = x.shape
    half = oa_t.shape[1]
    return pl.pallas_call(
        _out_ab_kernel,
        grid=(b, n // ts),
        in_specs=[
            pl.BlockSpec((None, ts, d), lambda i, t: (i, t, 0)),
            pl.BlockSpec((None, half, ts), lambda i, t: (i, 0, t)),
            pl.BlockSpec((None, d - half, ts), lambda i, t: (i, 0, t)),
            pl.BlockSpec((None, d, ts), lambda i, t: (i, 0, t + sg_off)),
            pl.BlockSpec((None, 1, d), lambda i, t: (i, 0, 0)),
            pl.BlockSpec((d, d), lambda i, t: (0, 0)),
        ],
        out_specs=pl.BlockSpec((None, ts, d), lambda i, t: (i, t, 0)),
        out_shape=jax.ShapeDtypeStruct((b, n, d), F32),
        compiler_params=_cparams(("parallel", "parallel")),
        name="out_ab",
    )(x, oa_t, ob_t, sg_t, gate, w_out_t)


def _proj_cd_kernel(x_ref, c_ref, mod_ref, w_in_ref, qg_ref, kg_ref, c64_ref, s64_ref,
                    q_ref, k_ref, v_ref, gb_ref, u_ref, sg_ref, *, nx):
    scale = WIN_DIM ** -0.5 * LOG2E
    z64 = jnp.zeros((WIN_DIM, PROJ_PART), BF16)
    grp = WIN_HEADS // WIN_KV_HEADS

    def tail_q(pp, cols):
        c64 = c64_ref[:, cols]
        s64 = s64_ref[:, cols]
        qg = qg_ref[...]
        for hd in range(WIN_HEADS):
            qh = _rms_cols(pp[hd * WIN_DIM:(hd + 1) * WIN_DIM], qg, WIN_DIM)
            qh = (_rope_cols(qh, c64, s64, WIN_DIM // 4) * scale).astype(BF16)
            for kvh in range(WIN_KV_HEADS):
                q_ref[hd, kvh * WIN_DIM:(kvh + 1) * WIN_DIM, cols] = qh if kvh == hd // grp else z64

    def tail_kv(pp, cols):
        c64 = c64_ref[:, cols]
        s64 = s64_ref[:, cols]
        kg = kg_ref[...]
        k_heads = []
        for kvh in range(WIN_KV_HEADS):
            kh = _rms_cols(pp[kvh * WIN_DIM:(kvh + 1) * WIN_DIM], kg, WIN_DIM)
            k_heads.append(_rope_cols(kh, c64, s64, WIN_DIM // 4))
            v_ref[kvh, :, cols] = pp[128 + kvh * WIN_DIM:128 + (kvh + 1) * WIN_DIM].astype(BF16)
        k_ref[cols, :] = jnp.concatenate(k_heads, axis=0).T.astype(BF16)
        gb_ref[:, cols] = pp[256:768].astype(BF16)

    def tail_conv(pp, cols):
        u_ref[:, cols] = (pp[0:CONV_CH] * pp[CONV_CH:2 * CONV_CH]).astype(BF16)

    def tail_gate(r0):
        def tail(pp, cols):
            sg_ref[r0:r0 + GATE_ROWS, cols] = _silu(pp).astype(BF16)
        return tail

    g0 = 2304
    groups = [(0, 512, tail_q), (512, 1280, tail_kv), (1280, g0, tail_conv)]
    groups += [(g0 + r, g0 + r + GATE_ROWS, tail_gate(r)) for r in range(0, D_MODEL, GATE_ROWS)]
    _projected_parts(x_ref, c_ref, mod_ref, w_in_ref, nx, groups)


def _proj_cd(x, ctx, mods, w_in_t, gains, tabs, ts):
    b, s, d = x.shape
    lc = ctx.shape[1]
    nx = s // ts
    n = s + ts
    qg, kg = [_bcast_gain(g) for g in gains]
    c64, s64 = tabs
    full = lambda a: pl.BlockSpec(a.shape, lambda i, t: (0,) * a.ndim)
    tab = lambda a: pl.BlockSpec((a.shape[0], ts), lambda i, t: (0, t))
    feat3 = lambda r: (jax.ShapeDtypeStruct((b, r, n), BF16), pl.BlockSpec((None, r, ts), lambda i, t: (i, 0, t)))
    outs = [
        (jax.ShapeDtypeStruct((b, WIN_HEADS, HEAD_PAD, n), BF16),
         pl.BlockSpec((None, WIN_HEADS, HEAD_PAD, ts), lambda i, t: (i, 0, 0, t))),
        (jax.ShapeDtypeStruct((b, n, HEAD_PAD), BF16),
         pl.BlockSpec((None, ts, HEAD_PAD), lambda i, t: (i, t, 0))),
        (jax.ShapeDtypeStruct((b, WIN_KV_HEADS, WIN_DIM, n), BF16),
         pl.BlockSpec((None, WIN_KV_HEADS, WIN_DIM, ts), lambda i, t: (i, 0, 0, t))),
        feat3(CONV_CH),
        feat3(CONV_CH),
        feat3(D_MODEL),
    ]
    return pl.pallas_call(
        functools.partial(_proj_cd_kernel, nx=nx),
        grid=(b, nx + 1),
        in_specs=_joint_specs(nx, ts, lc, d) + [full(w_in_t), full(qg), full(kg), tab(c64), tab(s64)],
        out_specs=[o[1] for o in outs],
        out_shape=[o[0] for o in outs],
        compiler_params=_cparams(("parallel", "parallel")),
        name="proj_cd",
    )(x, ctx, mods, w_in_t, qg, kg, c64, s64)


def _win_kernel(q_ref, kp_ref, kc_ref, kn_ref, vp_ref, vcur_ref, vn_ref, kctx_ref, vctx_ref, bias_ref, sink_ref,
                o_ref, *, sub):
    heads, tq = q_ref.shape[0], q_ref.shape[2]
    gv = vcur_ref.shape[0]
    nsub = tq // sub
    qi = pl.program_id(1)
    edge_p = jnp.where(qi > 0, 0.0, NEG_INF)
    edge_n = jnp.where(qi < pl.num_programs(1) - 1, 0.0, NEG_INF)
    bias_mid = bias_ref[...]
    bias_first = jnp.concatenate([bias_ref[0:WINDOW, :] + edge_p, bias_ref[WINDOW:, :]], axis=0)
    bias_last = jnp.concatenate([bias_ref[0:sub + WINDOW, :], bias_ref[sub + WINDOW:, :] + edge_n], axis=0)
    if nsub == 1:
        bias_first = bias_last = jnp.concatenate(
            [bias_ref[0:WINDOW, :] + edge_p, bias_ref[WINDOW:sub + WINDOW, :], bias_ref[sub + WINDOW:, :] + edge_n],
            axis=0)
    k_loc = jnp.concatenate([kp_ref[...], kc_ref[...], kn_ref[...]], axis=0)
    v_loc = [jnp.concatenate([vp_ref[h], vcur_ref[h], vn_ref[h]], axis=1) for h in range(gv)]
    kctx = kctx_ref[...]
    units = [(g, t) for g in range(heads) for t in range(nsub)]
    pending = {}

    def emit_scores(u):
        g, t = u
        q = q_ref[g, :, t * sub:(t + 1) * sub]
        bias = bias_first if t == 0 else (bias_last if t == nsub - 1 else bias_mid)
        pending[u] = (_dot(k_loc[t * sub:(t + 1) * sub + 2 * WINDOW], q) + bias, _dot(kctx, q))

    issued = 0
    for i, (g, t) in enumerate(units):
        while issued <= min(i + WIN_AHEAD, len(units) - 1):
            emit_scores(units[issued])
            issued += 1
        s_l, s_x = pending.pop((g, t))
        sk = sink_ref[g] * LOG2E
        m = jnp.maximum(jnp.maximum(jnp.max(s_l, axis=0, keepdims=True), jnp.max(s_x, axis=0, keepdims=True)), sk)
        p_l = jnp.exp2(s_l - m)
        p_x = jnp.exp2(s_x - m)
        den = jnp.sum(p_l, axis=0, keepdims=True) + jnp.sum(p_x, axis=0, keepdims=True) + jnp.exp2(sk - m)
        h = g * gv // heads
        acc = (_dot(v_loc[h][:, t * sub:(t + 1) * sub + 2 * WINDOW], p_l.astype(BF16))
               + _dot(vctx_ref[h], p_x.astype(BF16)))
        o_ref[g, :, t * sub:(t + 1) * sub] = (acc / den).astype(o_ref.dtype)


def _win_attn(q_t, k, v_t, sink, s, lc, *, tq, sub):
    b, hq = q_t.shape[0], q_t.shape[1]
    hv, dv = v_t.shape[1], v_t.shape[2]
    r = tq // WINDOW
    nb = s // WINDOW
    prev = lambda qi: jnp.maximum(qi * r - 1, 0)
    nxt = lambda qi: jnp.minimum((qi + 1) * r, nb - 1)
    kpos = jnp.arange(sub + 2 * WINDOW, dtype=jnp.int32)[:, None] - WINDOW
    qpos = jnp.arange(sub, dtype=jnp.int32)[None, :]
    bias = jnp.where(jnp.abs(qpos - kpos) <= WINDOW, 0.0, NEG_INF).astype(F32)
    return pl.pallas_call(
        functools.partial(_win_kernel, sub=sub),
        grid=(b, s // tq),
        in_specs=[
            pl.BlockSpec((None, hq, HEAD_PAD, tq), lambda i, qi: (i, 0, 0, qi)),
            pl.BlockSpec((None, WINDOW, HEAD_PAD), lambda i, qi: (i, prev(qi), 0)),
            pl.BlockSpec((None, tq, HEAD_PAD), lambda i, qi: (i, qi, 0)),
            pl.BlockSpec((None, WINDOW, HEAD_PAD), lambda i, qi: (i, nxt(qi), 0)),
            pl.BlockSpec((None, hv, dv, WINDOW), lambda i, qi: (i, 0, 0, prev(qi))),
            pl.BlockSpec((None, hv, dv, tq), lambda i, qi: (i, 0, 0, qi)),
            pl.BlockSpec((None, hv, dv, WINDOW), lambda i, qi: (i, 0, 0, nxt(qi))),
            pl.BlockSpec((None, lc, HEAD_PAD), lambda i, qi: (i, s // lc, 0)),
            pl.BlockSpec((None, hv, dv, lc), lambda i, qi: (i, 0, 0, s // lc)),
            pl.BlockSpec(bias.shape, lambda i, qi: (0, 0)),
            pl.BlockSpec(memory_space=pltpu.SMEM),
        ],
        out_specs=pl.BlockSpec((None, hq, dv, tq), lambda i, qi: (i, 0, 0, qi)),
        out_shape=jax.ShapeDtypeStruct((b, hq, dv, s), BF16),
        compiler_params=_cparams(("parallel", "parallel")),
        name="win_attn",
    )(q_t, k, k, k, v_t, v_t, v_t, k, v_t, bias, sink)


def _out_cd_kernel(x_ref, oc_ref, gb_ref, up_ref, u_ref, un_ref, cw_ref, sg_ref, gate_ref, w_ref, o_ref, *, ts):
    t = pl.program_id(1)
    nt = pl.num_programs(1)
    u = u_ref[...].astype(F32)
    lane = lax.broadcasted_iota(jnp.int32, (CONV_CH, ts), 1)
    halo_l = pltpu.roll(up_ref[...].astype(F32), 1, axis=1)
    halo_l = jnp.where(t > 0, halo_l, 0.0)
    halo_r = pltpu.roll(un_ref[...].astype(F32), LANES - 1, axis=1)
    halo_r = jnp.where(t < nt - 1, halo_r, 0.0)
    reps = ts // LANES
    um = jnp.where(lane == 0, jnp.tile(halo_l, (1, reps)), pltpu.roll(u, 1, axis=1))
    up = jnp.where(lane == ts - 1, jnp.tile(halo_r, (1, reps)), pltpu.roll(u, ts - 1, axis=1))
    w0 = jnp.tile(cw_ref[0], (1, reps))
    w1 = jnp.tile(cw_ref[1], (1, reps))
    w2 = jnp.tile(cw_ref[2], (1, reps))
    od = gb_ref[...].astype(F32) * (w0 * um + w1 * u + w2 * up)
    sg = sg_ref[...]
    half = D_MODEL - CONV_CH
    m = jnp.concatenate([oc_ref[...] * sg[0:half], od.astype(BF16) * sg[half:]], axis=0)
    y_t = _dot(w_ref[...], m)
    o_ref[...] = x_ref[...] + gate_ref[...] * y_t.T


def _out_cd(x, oc_t, gb_t, u_t, conv_wb, sg_t, gate, w_out_t, ts):
    b, n, d = x.shape
    r = ts // LANES
    nb = n // LANES
    return pl.pallas_call(
        functools.partial(_out_cd_kernel, ts=ts),
        grid=(b, n // ts),
        in_specs=[
            pl.BlockSpec((None, ts, d), lambda i, t: (i, t, 0)),
            pl.BlockSpec((None, d - CONV_CH, ts), lambda i, t: (i, 0, t)),
            pl.BlockSpec((None, CONV_CH, ts), lambda i, t: (i, 0, t)),
            pl.BlockSpec((None, CONV_CH, LANES), lambda i, t: (i, 0, jnp.maximum(t * r - 1, 0))),
            pl.BlockSpec((None, CONV_CH, ts), lambda i, t: (i, 0, t)),
            pl.BlockSpec((None, CONV_CH, LANES), lambda i, t: (i, 0, jnp.minimum((t + 1) * r, nb - 1))),
            pl.BlockSpec((CONV_K, CONV_CH, LANES), lambda i, t: (0, 0, 0)),
            pl.BlockSpec((None, d, ts), lambda i, t: (i, 0, t)),
            pl.BlockSpec((None, 1, d), lambda i, t: (i, 0, 0)),
            pl.BlockSpec((d, d), lambda i, t: (0, 0)),
        ],
        out_specs=pl.BlockSpec((None, ts, d), lambda i, t: (i, t, 0)),
        out_shape=jax.ShapeDtypeStruct((b, n, d), F32),
        compiler_params=_cparams(("parallel", "parallel")),
        name="out_cd",
    )(x, oc_t, gb_t, u_t, u_t, u_t, conv_wb, sg_t, gate, w_out_t)


def _rope_tables(s, tail, dd):
    q = dd // 4
    rows = s // GRID_W
    row = jnp.repeat(jnp.arange(rows, dtype=F32), GRID_W)
    col = jnp.tile(jnp.arange(GRID_W, dtype=F32), rows)
    inv = ROPE_BASE ** (-jnp.arange(q, dtype=F32) / q)
    ar = inv[:, None] * row[None, :]
    ac = inv[:, None] * col[None, :]
    cos = jnp.concatenate([jnp.cos(ar), jnp.cos(ar), jnp.cos(ac), jnp.cos(ac)], axis=0)
    sin = jnp.concatenate([-jnp.sin(ar), jnp.sin(ar), -jnp.sin(ac), jnp.sin(ac)], axis=0)
    return (jnp.concatenate([cos, jnp.ones((dd, tail), F32)], axis=1),
            jnp.concatenate([sin, jnp.zeros((dd, tail), F32)], axis=1))


def _tile_rows(n, cap):
    t = cap
    while n % t:
        t //= 2
    return t


def _key_tile(n, cap, step):
    t = (cap // step) * step
    while n % t:
        t -= step
    return t


def kernel(x, c, ctx, c_ctx, mod_w, mod_b, ab_w_in, ab_w_out, mla_cq_gain, mla_ckv_gain, mla_w_uq, mla_w_ukv,
           mla_q_gain, mla_k_gain, gqa_q_gain, gqa_k_gain, cd_w_in, cd_w_out, win_q_gain, win_k_gain,
           win_sink, conv_w):
    b, s, d = x.shape
    lc = ctx.shape[1]
    ts_p = _tile_rows(s, 4 * PROJ_PART)
    assert ts_p % lc == 0 and ts_p % PROJ_PART == 0 and s % lc == 0
    ts_x = _tile_rows(s, 1024)
    ts_c = _tile_rows(lc, 256)
    tq = _tile_rows(s, 512)
    n = s + lc
    tk = _key_tile(n, 3328, LANES)
    kc = 256 if tk % 256 == 0 else tk
    ahead = 2
    tq_w = _tile_rows(s, 1024)

    crows = jnp.zeros((8, d), F32).at[0:b].set(c).at[b].set(c_ctx)
    mods = _modulation(crows, mod_w, mod_b)

    def split_mod(layer):
        m = mods[layer]
        shift, scale, gate = m[:, 0:d], m[:, d:2 * d], m[:, 2 * d:3 * d]
        lat = jnp.stack([shift[0:b], scale[0:b]], axis=1)
        con = jnp.broadcast_to(jnp.stack([shift[b], scale[b]], axis=0)[None], (b, 2, d))
        gate_x = gate[0:b][:, None, :]
        gate_c = jnp.broadcast_to(gate[b][None, None, :], (b, 1, d))
        return jnp.stack([lat, con], axis=1), gate_x, gate_c

    tabs32 = _rope_tables(s, ts_p, MLA_ROPE)
    tabs64 = _rope_tables(s, ts_p, GQA_DIM)

    mod4, gate_x, gate_c = split_mod(0)
    w_in_t = ab_w_in[0].T.astype(BF16)
    w_uq_t = mla_w_uq[0].T.astype(BF16)
    w_ukv_t = mla_w_ukv[0].T.astype(BF16)
    w_out_t = ab_w_out[0].T.astype(BF16)
    gains = (mla_cq_gain[0], mla_ckv_gain[0], mla_q_gain[0], mla_k_gain[0], gqa_q_gain[0], gqa_k_gain[0])
    qa, ka, va, qb, kb, vb, sg = _proj_ab(x, ctx, mod4, w_in_t, w_uq_t, w_ukv_t, gains, tabs32 + tabs64, ts_p)
    kb = kb[:, None]
    oa = _stream_attn(qa, ka, va, s, n, tq=tq, tk=tk, kc=kc, ahead=ahead, name="flash_mla")
    ob = _stream_attn(qb, kb, vb, s, n, tq=tq, tk=tk, kc=kc, ahead=ahead, name="flash_gqa")
    x1 = _out_ab(x, oa.reshape(b, -1, s), ob.reshape(b, -1, s), sg, 0, gate_x, w_out_t, ts_x)
    oac = _ctx_attn(qa, ka, va, s, lc, "ctx_mla")
    obc = _ctx_attn(qb, kb, vb, s, lc, "ctx_gqa")
    xc1 = _out_ab(ctx, oac.reshape(b, -1, lc), obc.reshape(b, -1, lc), sg, s // ts_c, gate_c, w_out_t, ts_c)

    mod4, gate_x, gate_c = split_mod(1)
    w_in_t = cd_w_in[0].T.astype(BF16)
    w_out_t = cd_w_out[0].T.astype(BF16)
    q, k, v, gb, u, sg = _proj_cd(x1, xc1, mod4, w_in_t, (win_q_gain[0], win_k_gain[0]), tabs64, ts_p)
    oc = _win_attn(q, k, v, win_sink[0].astype(F32), s, lc, tq=tq_w, sub=min(tq_w, 256))
    conv_wb = jnp.broadcast_to(conv_w[0].astype(F32)[:, :, None], (CONV_K, CONV_CH, LANES))
    return _out_cd(x1, oc.reshape(b, -1, s), gb, u, conv_wb, sg, gate_x, w_out_t, ts_x)
```
